```python
import jax, jax.numpy as jnp
from jax import lax
import numpy as np

D_MODEL = 2048
BATCH = 2
SEQ = 4096
DEPTH = 1

MEM_LEN = 256
RMS_EPS = 1e-6

MLSTM_HEADS = 4
MLSTM_WIDTH = D_MODEL // 2
MLSTM_V_DIM = MLSTM_WIDTH // MLSTM_HEADS
MLSTM_QK_DIM = MLSTM_V_DIM // 2
MLSTM_QK_WIDTH = MLSTM_HEADS * MLSTM_QK_DIM
MLSTM_CHUNK = 64
GATE_SOFTCAP = 15.0

RWKV_WIDTH = D_MODEL // 2
RWKV_HEAD = 64
RWKV_HEADS = RWKV_WIDTH // RWKV_HEAD
RWKV_DECAY_RANK = 64
RWKV_A_RANK = 64
RWKV_GATE_RANK = 160
RWKV_GN_EPS = 64e-5

MLSTM_COLS = (MLSTM_QK_WIDTH, MLSTM_QK_WIDTH, MLSTM_WIDTH, MLSTM_WIDTH, MLSTM_HEADS, MLSTM_HEADS)
RWKV_COLS = (RWKV_WIDTH, RWKV_WIDTH, RWKV_WIDTH, RWKV_DECAY_RANK, RWKV_A_RANK, RWKV_GATE_RANK)
MLSTM_TOTAL = sum(MLSTM_COLS)
RWKV_TOTAL = sum(RWKV_COLS)
GATE_TOTAL = 2 * D_MODEL
IN_COLS = MLSTM_TOTAL + RWKV_TOTAL + GATE_TOTAL

XATTN_HEADS = 4
XATTN_HEAD_DIM = 128
XATTN_WIDTH = XATTN_HEADS * XATTN_HEAD_DIM

D_FF = 4 * D_MODEL
CONV_WIDTH = 3

kernel_name = "hybrid_mlstm_rwkv7_gated_xattn_convffn"


def rms_norm(x, g):
    xf = x.astype(jnp.float32)
    y = xf * lax.rsqrt(jnp.mean(xf * xf, axis=-1, keepdims=True) + RMS_EPS)
    return (y * g).astype(x.dtype)


def split_cols(t, sizes):
    idx = [int(i) for i in np.cumsum(sizes)[:-1]]
    return jnp.split(t, idx, axis=-1)


def shift_right(t, n):
    return jnp.pad(t, ((0, 0), (n, 0), (0, 0)))[:, : t.shape[1]]


def softcap(t, cap):
    return cap * jnp.tanh(t / cap)


def mlstm_chunkwise(q, k, v, ig, logf):
    B, S, H, dk = q.shape
    dv = v.shape[-1]
    L = MLSTM_CHUNK
    NC = S // L

    def to_chunks(t):
        return t.reshape(B, NC, L, H, t.shape[-1]).transpose(1, 0, 3, 2, 4)

    def gate_chunks(t):
        return t.reshape(B, NC, L, H).transpose(1, 0, 3, 2)

    causal = jnp.tril(jnp.ones((L, L), dtype=bool))

    def body(carry, inp):
        C, n, m = carry
        qc, kc, vc, ic, fc = inp
        b = jnp.cumsum(fc, axis=-1)
        dmat = jnp.where(causal, b[..., :, None] - b[..., None, :] + ic[..., None, :], -jnp.inf)
        inter = b + m[..., None]
        m_t = jnp.maximum(inter, jnp.max(dmat, axis=-1))
        dexp = jnp.exp(dmat - m_t[..., None])
        w_inter = jnp.exp(inter - m_t)
        s = jnp.einsum('bhtd,bhsd->bhts', qc, kc) * dexp
        num = w_inter[..., None] * jnp.einsum('bhvd,bhtd->bhtv', C, qc) + jnp.einsum('bhts,bhsv->bhtv', s, vc)
        den = w_inter * jnp.einsum('bhd,bhtd->bht', n, qc) + jnp.sum(s, axis=-1)
        h = num / jnp.maximum(jnp.abs(den), jnp.exp(-m_t))[..., None]
        bL = b[..., -1]
        gs = bL[..., None] - b + ic
        m_new = jnp.maximum(bL + m, jnp.max(gs, axis=-1))
        carry_w = jnp.exp(bL + m - m_new)
        ws = jnp.exp(gs - m_new[..., None])
        C = carry_w[..., None, None] * C + jnp.einsum('bhs,bhsv,bhsd->bhvd', ws, vc, kc)
        n = carry_w[..., None] * n + jnp.einsum('bhs,bhsd->bhd', ws, kc)
        return (C, n, m_new), h

    init = (jnp.zeros((B, H, dv, dk), jnp.float32), jnp.zeros((B, H, dk), jnp.float32), jnp.zeros((B, H), jnp.float32))
    _, h = lax.scan(body, init, (to_chunks(q), to_chunks(k), to_chunks(v), gate_chunks(ig), gate_chunks(logf)))
    return h.transpose(1, 0, 3, 2, 4).reshape(B, S, H, dv)


def rwkv7_scan(r, w, k, v, kk, kka):
    B, S, H, N = r.shape

    def step(state, inp):
        rt, wt, kt, vt, kkt, kat = inp
        state = (state * wt[:, :, None, :]
                 - jnp.einsum('bhvk,bhk->bhv', state, kkt)[..., None] * kat[:, :, None, :]
                 + vt[..., None] * kt[:, :, None, :])
        return state, jnp.einsum('bhvk,bhk->bhv', state, rt)

    seq_major = lambda t: t.transpose(1, 0, 2, 3)
    init = jnp.zeros((B, H, N, N), jnp.float32)
    _, y = lax.scan(step, init, tuple(seq_major(t) for t in (r, w, k, v, kk, kka)))
    return y.transpose(1, 0, 2, 3)


def hybrid_mixer(h, w_in, b_i, b_f, head_norm, mu, w0, w_up, a0, a_up, g_up, k_k, k_a, r_k, ln_g, ln_b,
                 w_branch_a, w_branch_b, w_out):
    B, S, _ = h.shape
    f32 = jnp.float32
    proj = h @ w_in
    mlstm_p, rwkv_p, gate_p = split_cols(proj, (MLSTM_TOTAL, RWKV_TOTAL, GATE_TOTAL))

    q, k, v, o, ir, fr = split_cols(mlstm_p, MLSTM_COLS)
    q = q.reshape(B, S, MLSTM_HEADS, MLSTM_QK_DIM).astype(f32)
    k = k.reshape(B, S, MLSTM_HEADS, MLSTM_QK_DIM).astype(f32) * (MLSTM_QK_DIM ** -0.5)
    v = v.reshape(B, S, MLSTM_HEADS, MLSTM_V_DIM).astype(f32)
    ig = softcap((ir + b_i).astype(f32), GATE_SOFTCAP)
    logf = jax.nn.log_sigmoid(softcap((fr + b_f).astype(f32), GATE_SOFTCAP))
    hm = mlstm_chunkwise(q, k, v, ig, logf)
    hm = hm * lax.rsqrt(jnp.mean(hm * hm, axis=-1, keepdims=True) + RMS_EPS) * head_norm.reshape(MLSTM_HEADS, MLSTM_V_DIM)
    ha = (jax.nn.sigmoid(o.astype(f32)) * hm.reshape(B, S, MLSTM_WIDTH)).astype(h.dtype)

    rp = rwkv_p + (shift_right(rwkv_p, 1) - rwkv_p) * mu
    r, kr, vr, wl, al, gl = split_cols(rp, RWKV_COLS)
    w_log = -jax.nn.softplus(-(w0 + jnp.tanh(wl) @ w_up)) - 0.5
    decay = jnp.exp(-jnp.exp(w_log.astype(f32)))
    a = jax.nn.sigmoid(a0 + al @ a_up)
    g = jax.nn.sigmoid(gl) @ g_up
    heads = lambda t: t.reshape(B, S, RWKV_HEADS, RWKV_HEAD).astype(f32)
    per_head = lambda p: p.reshape(RWKV_HEADS, RWKV_HEAD).astype(f32)
    kk = heads(kr * k_k)
    kk = kk / jnp.maximum(jnp.linalg.norm(kk, axis=-1, keepdims=True), 1e-12)
    a_h = heads(a)
    kr = heads(kr * (1.0 + (a - 1.0) * k_a))
    r_h, v_h = heads(r), heads(vr)
    y = rwkv7_scan(r_h, heads(decay), kr, v_h, kk, kk * a_h)
    mean = jnp.mean(y, axis=-1, keepdims=True)
    var = jnp.mean(jnp.square(y - mean), axis=-1, keepdims=True)
    y = (y - mean) * lax.rsqrt(var + RWKV_GN_EPS) * per_head(ln_g) + per_head(ln_b)
    y = y + jnp.sum(r_h * kr * r_k.astype(f32), axis=-1, keepdims=True) * v_h
    hb = (y.reshape(B, S, RWKV_WIDTH) * g).astype(h.dtype)

    g_a, g_b = split_cols(gate_p, (D_MODEL, D_MODEL))
    merged = jax.nn.sigmoid(g_a) * (ha @ w_branch_a) + jax.nn.sigmoid(g_b) * (hb @ w_branch_b)
    return merged @ w_out


def cross_attention(h, mem_n, wq, wkv, wo):
    B, S, _ = h.shape
    M = mem_n.shape[1]
    q = (h @ wq).reshape(B, S, XATTN_HEADS, XATTN_HEAD_DIM)
    k, v = split_cols(mem_n @ wkv, (XATTN_WIDTH, XATTN_WIDTH))
    k = k.reshape(B, M, XATTN_HEADS, XATTN_HEAD_DIM)
    v = v.reshape(B, M, XATTN_HEADS, XATTN_HEAD_DIM)
    scores = jnp.einsum('bshd,bmhd->bhsm', q, k).astype(jnp.float32) * (XATTN_HEAD_DIM ** -0.5)
    p = jax.nn.softmax(scores, axis=-1).astype(v.dtype)
    o = jnp.einsum('bhsm,bmhd->bshd', p, v).reshape(B, S, XATTN_WIDTH)
    return o @ wo


def conv_glu_ffn(h, w_up, conv_w, conv_b, w_down):
    u = h @ w_up
    uc = conv_b + sum(conv_w[j] * shift_right(u, CONV_WIDTH - 1 - j) for j in range(CONV_WIDTH))
    gate, up = split_cols(uc, (D_FF, D_FF))
    return (jax.nn.gelu(gate, approximate=True) * up) @ w_down


def setup_inputs(seed: int = 0) -> dict:
    key = jax.random.key(seed)
    ks = iter(jax.random.split(key, 40))
    nrm = lambda shape, scale: scale * jax.random.normal(next(ks), shape, jnp.float32)
    gain = lambda n: 1.0 + nrm((DEPTH, n), 0.05)
    Dp = DEPTH
    conv_w = nrm((Dp, CONV_WIDTH, 2 * D_FF), 0.2).at[:, CONV_WIDTH - 1].add(1.0)
    return {
        "x": nrm((BATCH, SEQ, D_MODEL), 1.0),
        "mem": nrm((BATCH, MEM_LEN, D_MODEL), 1.0),
        "mix_pre_norm": gain(D_MODEL),
        "w_in": nrm((Dp, D_MODEL, IN_COLS), D_MODEL ** -0.5),
        "mlstm_b_i": nrm((Dp, MLSTM_HEADS), 0.1),
        "mlstm_b_f": jnp.linspace(3.0, 6.0, MLSTM_HEADS)[None] + nrm((Dp, MLSTM_HEADS), 0.1),
        "mlstm_head_norm": gain(MLSTM_WIDTH),
        "rwkv_mu": jax.random.uniform(next(ks), (Dp, RWKV_TOTAL), jnp.float32),
        "rwkv_w0": (-6.5 + 5.0 * jnp.linspace(0.0, 1.0, RWKV_WIDTH) ** 0.85)[None] + nrm((Dp, RWKV_WIDTH), 0.1),
        "rwkv_w_up": nrm((Dp, RWKV_DECAY_RANK, RWKV_WIDTH), 0.5 * RWKV_DECAY_RANK ** -0.5),
        "rwkv_a0": nrm((Dp, RWKV_WIDTH), 0.1),
        "rwkv_a_up": nrm((Dp, RWKV_A_RANK, RWKV_WIDTH), RWKV_A_RANK ** -0.5),
        "rwkv_g_up": nrm((Dp, RWKV_GATE_RANK, RWKV_WIDTH), RWKV_GATE_RANK ** -0.5),
        "rwkv_k_k": 0.85 + nrm((Dp, RWKV_WIDTH), 0.05),
        "rwkv_k_a": 1.0 + nrm((Dp, RWKV_WIDTH), 0.05),
        "rwkv_r_k": nrm((Dp, RWKV_HEADS, RWKV_HEAD), 0.1),
        "rwkv_ln_g": gain(RWKV_WIDTH),
        "rwkv_ln_b": nrm((Dp, RWKV_WIDTH), 0.02),
        "w_branch_a": nrm((Dp, MLSTM_WIDTH, D_MODEL), MLSTM_WIDTH ** -0.5),
        "w_branch_b": nrm((Dp, RWKV_WIDTH, D_MODEL), RWKV_WIDTH ** -0.5),
        "w_mix_out": nrm((Dp, D_MODEL, D_MODEL), D_MODEL ** -0.5),
        "mix_post_norm": gain(D_MODEL),
        "xattn_pre_norm": gain(D_MODEL),
        "mem_norm": gain(D_MODEL),
        "xattn_wq": nrm((Dp, D_MODEL, XATTN_WIDTH), D_MODEL ** -0.5),
        "xattn_wkv": nrm((Dp, D_MODEL, 2 * XATTN_WIDTH), D_MODEL ** -0.5),
        "xattn_wo": nrm((Dp, XATTN_WIDTH, D_MODEL), XATTN_WIDTH ** -0.5),
        "xattn_post_norm": gain(D_MODEL),
        "ffn_pre_norm": gain(D_MODEL),
        "ffn_w_up": nrm((Dp, D_MODEL, 2 * D_FF), D_MODEL ** -0.5),
        "ffn_conv_w": conv_w,
        "ffn_conv_b": nrm((Dp, 2 * D_FF), 0.02),
        "ffn_w_down": nrm((Dp, D_FF, D_MODEL), D_FF ** -0.5),
        "ffn_post_norm": gain(D_MODEL),
    }


def reference(x, mem, mix_pre_norm, w_in, mlstm_b_i, mlstm_b_f, mlstm_head_norm, rwkv_mu, rwkv_w0, rwkv_w_up,
              rwkv_a0, rwkv_a_up, rwkv_g_up, rwkv_k_k, rwkv_k_a, rwkv_r_k, rwkv_ln_g, rwkv_ln_b, w_branch_a,
              w_branch_b, w_mix_out, mix_post_norm, xattn_pre_norm, mem_norm, xattn_wq, xattn_wkv, xattn_wo,
              xattn_post_norm, ffn_pre_norm, ffn_w_up, ffn_conv_w, ffn_conv_b, ffn_w_down, ffn_post_norm):
    for l in range(DEPTH):
        h = rms_norm(x, mix_pre_norm[l])
        y = hybrid_mixer(h, w_in[l], mlstm_b_i[l], mlstm_b_f[l], mlstm_head_norm[l], rwkv_mu[l], rwkv_w0[l],
                         rwkv_w_up[l], rwkv_a0[l], rwkv_a_up[l], rwkv_g_up[l], rwkv_k_k[l], rwkv_k_a[l],
                         rwkv_r_k[l], rwkv_ln_g[l], rwkv_ln_b[l], w_branch_a[l], w_branch_b[l], w_mix_out[l])
        x = x + rms_norm(y, mix_post_norm[l])
        h = rms_norm(x, xattn_pre_norm[l])
        m = rms_norm(mem, mem_norm[l])
        x = x + rms_norm(cross_attention(h, m, xattn_wq[l], xattn_wkv[l], xattn_wo[l]), xattn_post_norm[l])
        h = rms_norm(x, ffn_pre_norm[l])
        x = x + rms_norm(conv_glu_ffn(h, ffn_w_up[l], ffn_conv_w[l], ffn_conv_b[l], ffn_w_down[l]), ffn_post_norm[l])
    return x
```

```python
import functools

import jax
import jax.numpy as jnp
from jax import lax
from jax.experimental import pallas as pl
from jax.experimental.pallas import tpu as pltpu

D_MODEL = 2048
MEM_LEN = 256
RMS_EPS = 1e-6

MLSTM_HEADS = 4
MLSTM_WIDTH = D_MODEL // 2
MLSTM_V_DIM = MLSTM_WIDTH // MLSTM_HEADS
MLSTM_QK_DIM = MLSTM_V_DIM // 2
MLSTM_QK_WIDTH = MLSTM_HEADS * MLSTM_QK_DIM
GATE_SOFTCAP = 15.0

RWKV_WIDTH = D_MODEL // 2
RWKV_HEAD = 64
RWKV_HEADS = RWKV_WIDTH // RWKV_HEAD
RWKV_DECAY_RANK = 64
RWKV_A_RANK = 64
RWKV_GATE_RANK = 160
RWKV_GN_EPS = 64e-5

MLSTM_TOTAL = 2 * MLSTM_QK_WIDTH + 2 * MLSTM_WIDTH + 2 * MLSTM_HEADS
RWKV_TOTAL = 3 * RWKV_WIDTH + RWKV_DECAY_RANK + RWKV_A_RANK + RWKV_GATE_RANK

XATTN_HEADS = 4
XATTN_HEAD_DIM = 128
XATTN_WIDTH = XATTN_HEADS * XATTN_HEAD_DIM

D_FF = 4 * D_MODEL
CONV_WIDTH = 3

COL_MQ = 0
COL_MK = COL_MQ + MLSTM_QK_WIDTH
COL_MV = COL_MK + MLSTM_QK_WIDTH
COL_MO = COL_MV + MLSTM_WIDTH
COL_RR = COL_MO + MLSTM_WIDTH
COL_RK = COL_RR + RWKV_WIDTH
COL_RV = COL_RK + RWKV_WIDTH
COL_GA = COL_RV + RWKV_WIDTH
COL_GB = COL_GA + D_MODEL
COL_RG = COL_GB + D_MODEL
RG_PAD = 256
COL_WA = COL_RG + RG_PAD
WA_W = RWKV_DECAY_RANK + RWKV_A_RANK
COL_IF = COL_WA + WA_W
IF_W = 128
PROJ_COLS = COL_IF + IF_W

CHUNK = 64
RWKV_HEADS_PER_STEP = 4
VMEM_LIMIT = 56 * 1024 * 1024

F32 = jnp.float32
BF16 = jnp.bfloat16


def _bf(x):
    return x.astype(BF16)


def _mm(a, b):
    return jnp.dot(_bf(a), _bf(b), preferred_element_type=F32)


def _mm_nt(a, b):
    return lax.dot_general(_bf(a), _bf(b), (((1,), (1,)), ((), ())), preferred_element_type=F32)


def _mm_tn(a, b):
    return lax.dot_general(_bf(a), _bf(b), (((0,), (0,)), ((), ())), preferred_element_type=F32)


def _mm_exact_lhs(tri, x):
    hi = _bf(x)
    r1 = x - hi.astype(F32)
    mid = _bf(r1)
    lo = _bf(r1 - mid.astype(F32))
    t = _bf(tri)
    return (jnp.dot(t, hi, preferred_element_type=F32) + jnp.dot(t, mid, preferred_element_type=F32)
            + jnp.dot(t, lo, preferred_element_type=F32))


def _rms(x, g):
    return x * lax.rsqrt(jnp.mean(x * x, axis=-1, keepdims=True) + RMS_EPS) * g


def _softplus(x):
    return jnp.maximum(x, 0.0) + jnp.log1p(jnp.exp(-jnp.abs(x)))


def _sigmoid(x):
    return 1.0 / (1.0 + jnp.exp(-x))


def _params(sem):
    return pltpu.CompilerParams(dimension_semantics=sem, vmem_limit_bytes=VMEM_LIMIT)


def _norm_matmul_kernel(x_ref, g_ref, w_ref, o_ref, h_ref):
    @pl.when(pl.program_id(1) == 0)
    def _():
        h_ref[...] = _bf(_rms(x_ref[...], g_ref[...]))

    o_ref[...] = jnp.dot(h_ref[...], w_ref[...], preferred_element_type=F32).astype(o_ref.dtype)


def _norm_matmul(x, g, w, *, tm, tn, name):
    t, d = x.shape
    n = w.shape[1]
    return pl.pallas_call(
        _norm_matmul_kernel,
        out_shape=jax.ShapeDtypeStruct((t, n), F32),
        grid=(t // tm, n // tn),
        in_specs=[pl.BlockSpec((tm, d), lambda i, j: (i, 0)),
                  pl.BlockSpec((1, d), lambda i, j: (0, 0)),
                  pl.BlockSpec((d, tn), lambda i, j: (0, j))],
        out_specs=pl.BlockSpec((tm, tn), lambda i, j: (i, j)),
        scratch_shapes=[pltpu.VMEM((tm, d), BF16)],
        compiler_params=_params(("parallel", "arbitrary")),
        name=name,
    )(x, g, w)


def _mlstm_kernel(q_ref, k_ref, v_ref, o_ref, gate_ref, bias_ref, hn_ref, out_ref, ct_ref, n_ref, m_ref):
    L = CHUNK
    dk, dv = MLSTM_QK_DIM, MLSTM_V_DIM

    @pl.when(pl.program_id(1) == 0)
    def _():
        ct_ref[...] = jnp.zeros_like(ct_ref)
        n_ref[...] = jnp.zeros_like(n_ref)
        m_ref[...] = jnp.zeros_like(m_ref)

    pre = gate_ref[0] + bias_ref[...]
    capped = GATE_SOFTCAP * jnp.tanh(pre / GATE_SOFTCAP)
    logf = -_softplus(-capped)
    row = lax.broadcasted_iota(jnp.int32, (L, L), 0)
    col = lax.broadcasted_iota(jnp.int32, (L, L), 1)
    causal = row >= col
    bcum = _mm_exact_lhs(causal.astype(F32), logf)
    ig_t = capped.T
    bcum_t = bcum.T
    scale = MLSTM_QK_DIM ** -0.5

    for h in range(MLSTM_HEADS):
        q = q_ref[0, :, h * dk:(h + 1) * dk]
        k = k_ref[0, :, h * dk:(h + 1) * dk] * scale
        v = v_ref[0, :, h * dv:(h + 1) * dv]
        b_c = bcum[:, MLSTM_HEADS + h:MLSTM_HEADS + h + 1]
        b_r = bcum_t[MLSTM_HEADS + h:MLSTM_HEADS + h + 1, :]
        i_c = capped[:, h:h + 1]
        i_r = ig_t[h:h + 1, :]
        m_prev = m_ref[h][0:1, 0:1]
        ct = ct_ref[h]
        nrow = n_ref[h][0:1, :]

        dmat = jnp.where(causal, b_c - b_r + i_r, -jnp.inf)
        inter = b_c + m_prev
        m_t = jnp.maximum(inter, jnp.max(dmat, axis=-1, keepdims=True))
        dexp = jnp.exp(dmat - m_t)
        w_inter = jnp.exp(inter - m_t)
        s = _mm_nt(q, k) * dexp
        num = w_inter * _mm(q, ct) + _mm(s, v)
        den = w_inter * jnp.sum(q * nrow, axis=-1, keepdims=True) + jnp.sum(s, axis=-1, keepdims=True)
        hh = num / jnp.maximum(jnp.abs(den), jnp.exp(-m_t))

        b_last = b_c[L - 1:L, :]
        gs = b_last - b_c + i_c
        m_new = jnp.maximum(b_last + m_prev, jnp.max(gs, axis=0, keepdims=True))
        carry_w = jnp.exp(b_last + m_prev - m_new)
        ws = jnp.exp(gs - m_new)
        ct_ref[h] = carry_w * ct + _mm_tn(k, ws * v)
        n_ref[h] = jnp.broadcast_to(carry_w * nrow + jnp.sum(ws * k, axis=0, keepdims=True), n_ref.shape[1:])
        m_ref[h] = jnp.broadcast_to(m_new, m_ref.shape[1:])

        hm = hh * lax.rsqrt(jnp.mean(hh * hh, axis=-1, keepdims=True) + RMS_EPS) * hn_ref[:, h * dv:(h + 1) * dv]
        og = o_ref[0, :, h * dv:(h + 1) * dv]
        out_ref[0, :, h * dv:(h + 1) * dv] = (_sigmoid(og) * hm).astype(out_ref.dtype)


def _mlstm(proj3, gate_bias, head_norm):
    b, s, _ = proj3.shape
    L = CHUNK
    qw, vw = MLSTM_QK_WIDTH, MLSTM_WIDTH
    return pl.pallas_call(
        _mlstm_kernel,
        out_shape=jax.ShapeDtypeStruct((b, s, MLSTM_WIDTH), BF16),
        grid=(b, s // L),
        in_specs=[pl.BlockSpec((1, L, qw), lambda i, c: (i, c, COL_MQ // qw)),
                  pl.BlockSpec((1, L, qw), lambda i, c: (i, c, COL_MK // qw)),
                  pl.BlockSpec((1, L, vw), lambda i, c: (i, c, COL_MV // vw)),
                  pl.BlockSpec((1, L, vw), lambda i, c: (i, c, COL_MO // vw)),
                  pl.BlockSpec((1, L, IF_W), lambda i, c: (i, c, COL_IF // IF_W)),
                  pl.BlockSpec((1, IF_W), lambda i, c: (0, 0)),
                  pl.BlockSpec((1, vw), lambda i, c: (0, 0))],
        out_specs=pl.BlockSpec((1, L, vw), lambda i, c: (i, c, 0)),
        scratch_shapes=[pltpu.VMEM((MLSTM_HEADS, MLSTM_QK_DIM, MLSTM_V_DIM), F32),
                        pltpu.VMEM((MLSTM_HEADS, 8, MLSTM_QK_DIM), F32),
                        pltpu.VMEM((MLSTM_HEADS, 8, 128), F32)],
        compiler_params=_params(("parallel", "arbitrary")),
        name="mlstm",
    )(proj3, proj3, proj3, proj3, proj3, gate_bias, head_norm)


PV_MU_R, PV_MU_K, PV_MU_V, PV_W0, PV_A0, PV_KK, PV_KA, PV_RK, PV_LNG, PV_LNB = range(10)
PV_ROWS = 16


def _rwkv_kernel(r_ref, k_ref, v_ref, g_ref, wa_ref, pv_ref, mug_ref, muwa_ref, wup_ref, aup_ref, gup_ref,
                 out_ref, s_ref, pr_ref, pk_ref, pvv_ref, pg_ref, pwa_ref):
    L = CHUNK
    N = RWKV_HEAD
    HB = RWKV_HEADS_PER_STEP

    @pl.when(pl.program_id(2) == 0)
    def _():
        s_ref[...] = jnp.zeros_like(s_ref)
        pr_ref[...] = jnp.zeros_like(pr_ref)
        pk_ref[...] = jnp.zeros_like(pk_ref)
        pvv_ref[...] = jnp.zeros_like(pvv_ref)
        pg_ref[...] = jnp.zeros_like(pg_ref)
        pwa_ref[...] = jnp.zeros_like(pwa_ref)

    def shift_lerp(x_ref, prev_ref, mu):
        x = x_ref[0]
        rid = lax.broadcasted_iota(jnp.int32, x.shape, 0)
        xs = jnp.where(rid == 0, prev_ref[0:1, :], pltpu.roll(x, 1, 0))
        prev_ref[0:1, :] = x[L - 1:L, :]
        return x + (xs - x) * mu

    pv = pv_ref[...]
    row = lambda i: pv[i:i + 1, :]
    r = shift_lerp(r_ref, pr_ref, row(PV_MU_R))
    kr = shift_lerp(k_ref, pk_ref, row(PV_MU_K))
    v = shift_lerp(v_ref, pvv_ref, row(PV_MU_V))
    gl = shift_lerp(g_ref, pg_ref, mug_ref[...])
    wa = shift_lerp(wa_ref, pwa_ref, muwa_ref[...])

    wl = jnp.tanh(wa[:, 0:RWKV_DECAY_RANK])
    al = wa[:, RWKV_DECAY_RANK:WA_W]
    w_log = -_softplus(-(row(PV_W0) + jnp.dot(_bf(wl), wup_ref[...], preferred_element_type=F32))) - 0.5
    lw = -jnp.exp(w_log)
    a = _sigmoid(row(PV_A0) + jnp.dot(_bf(al), aup_ref[...], preferred_element_type=F32))
    gg = jnp.dot(_bf(_sigmoid(gl)), gup_ref[...], preferred_element_type=F32)

    ti = lax.broadcasted_iota(jnp.int32, (L, L), 0)
    si = lax.broadcasted_iota(jnp.int32, (L, L), 1)
    incl = ti >= si
    strict = ti > si
    eye = (ti == si).astype(F32)
    cs = _mm_exact_lhs(incl.astype(F32), lw)
    g_incl = jnp.exp(cs)
    g_excl = jnp.exp(cs - lw)
    g_inv = jnp.exp(-cs)
    g_last = g_incl[L - 1:L, :]

    kk0 = kr * row(PV_KK)
    kr2 = kr * (1.0 + (a - 1.0) * row(PV_KA))
    rt_all = r * g_incl
    kt_all = kr2 * g_inv
    bonus_all = r * kr2 * row(PV_RK)

    lvl_masks = []
    bsz = 1
    while bsz < L:
        same = (ti // (2 * bsz)) == (si // (2 * bsz))
        lvl_masks.append(jnp.where(same & ((ti & bsz) != 0) & ((si & bsz) == 0), 1.0, 0.0))
        bsz *= 2

    for h in range(HB):
        sl = slice(h * N, (h + 1) * N)
        kk = kk0[:, sl]
        kk = kk / jnp.maximum(jnp.sqrt(jnp.sum(kk * kk, axis=-1, keepdims=True)), 1e-12)
        p = g_excl[:, sl] * kk
        qt = kk * a[:, sl] * g_inv[:, sl]
        kt = kt_all[:, sl]
        rt = rt_all[:, sl]
        vh = v[:, sl]
        gl_h = g_last[:, sl]

        n_pq = jnp.where(strict, _mm_nt(p, qt), 0.0)
        a_pk = jnp.where(strict, _mm_nt(p, kt), 0.0)
        a_rq = jnp.where(incl, _mm_nt(rt, qt), 0.0)
        a_rk = jnp.where(incl, _mm_nt(rt, kt), 0.0)

        x = eye - n_pq * lvl_masks[0]
        for msk in lvl_masks[1:]:
            x = x - _mm(_mm(x, n_pq * msk), x)

        w = _mm(x, p)
        u0 = _mm(x, _mm(a_pk, vh))
        mmat = (eye - _mm_tn(w, qt)) * gl_h
        bmat = (_mm_tn(vh, kt) - _mm_tn(u0, qt)) * gl_h
        gmat = rt - _mm(a_rq, w)
        y0 = _mm(a_rk, vh) - _mm(a_rq, u0)

        s0 = s_ref[h]
        y = _mm_nt(gmat, s0) + y0
        s_ref[h] = _mm(s0, mmat) + bmat

        mean = jnp.mean(y, axis=-1, keepdims=True)
        yc = y - mean
        var = jnp.mean(yc * yc, axis=-1, keepdims=True)
        yn = yc * lax.rsqrt(var + RWKV_GN_EPS) * row(PV_LNG)[:, sl] + row(PV_LNB)[:, sl]
        yn = yn + jnp.sum(bonus_all[:, sl], axis=-1, keepdims=True) * vh
        out_ref[0, :, sl] = (yn * gg[:, sl]).astype(out_ref.dtype)


def _rwkv(proj3, pvec, mu_g, mu_wa, w_up, a_up, g_up):
    b, s, _ = proj3.shape
    L = CHUNK
    HB = RWKV_HEADS_PER_STEP
    hw = HB * RWKV_HEAD
    ngrp = RWKV_HEADS // HB
    return pl.pallas_call(
        _rwkv_kernel,
        out_shape=jax.ShapeDtypeStruct((b, s, RWKV_WIDTH), BF16),
        grid=(b, ngrp, s // L),
        in_specs=[pl.BlockSpec((1, L, hw), lambda i, g, c: (i, c, COL_RR // hw + g)),
                  pl.BlockSpec((1, L, hw), lambda i, g, c: (i, c, COL_RK // hw + g)),
                  pl.BlockSpec((1, L, hw), lambda i, g, c: (i, c, COL_RV // hw + g)),
                  pl.BlockSpec((1, L, RG_PAD), lambda i, g, c: (i, c, COL_RG // RG_PAD)),
                  pl.BlockSpec((1, L, WA_W), lambda i, g, c: (i, c, COL_WA // WA_W)),
                  pl.BlockSpec((PV_ROWS, hw), lambda i, g, c: (0, g)),
                  pl.BlockSpec((1, RG_PAD), lambda i, g, c: (0, 0)),
                  pl.BlockSpec((1, WA_W), lambda i, g, c: (0, 0)),
                  pl.BlockSpec((RWKV_DECAY_RANK, hw), lambda i, g, c: (0, g)),
                  pl.BlockSpec((RWKV_A_RANK, hw), lambda i, g, c: (0, g)),
                  pl.BlockSpec((RG_PAD, hw), lambda i, g, c: (0, g))],
        out_specs=pl.BlockSpec((1, L, hw), lambda i, g, c: (i, c, g)),
        scratch_shapes=[pltpu.VMEM((HB, RWKV_HEAD, RWKV_HEAD), F32),
                        pltpu.VMEM((8, hw), F32), pltpu.VMEM((8, hw), F32), pltpu.VMEM((8, hw), F32),
                        pltpu.VMEM((8, RG_PAD), F32), pltpu.VMEM((8, WA_W), F32)],
        compiler_params=_params(("parallel", "parallel", "arbitrary")),
        name="rwkv7",
    )(proj3, proj3, proj3, proj3, proj3, pvec, mu_g, mu_wa, w_up, a_up, g_up)


def _merge_kernel(ha_ref, hb_ref, ga_ref, gb_ref, wa_ref, wb_ref, o_ref):
    ya = jnp.dot(ha_ref[...], wa_ref[...], preferred_element_type=F32)
    yb = jnp.dot(hb_ref[...], wb_ref[...], preferred_element_type=F32)
    o_ref[...] = (_sigmoid(ga_ref[...]) * ya + _sigmoid(gb_ref[...]) * yb).astype(o_ref.dtype)


def _merge(ha, hb, proj, wa, wb, *, tm, tn):
    t, kdim = ha.shape
    n = wa.shape[1]
    return pl.pallas_call(
        _merge_kernel,
        out_shape=jax.ShapeDtypeStruct((t, n), BF16),
        grid=(t // tm, n // tn),
        in_specs=[pl.BlockSpec((tm, kdim), lambda i, j: (i, 0)),
                  pl.BlockSpec((tm, kdim), lambda i, j: (i, 0)),
                  pl.BlockSpec((tm, tn), lambda i, j: (i, COL_GA // tn + j)),
                  pl.BlockSpec((tm, tn), lambda i, j: (i, COL_GB // tn + j)),
                  pl.BlockSpec((kdim, tn), lambda i, j: (0, j)),
                  pl.BlockSpec((kdim, tn), lambda i, j: (0, j))],
        out_specs=pl.BlockSpec((tm, tn), lambda i, j: (i, j)),
        compiler_params=_params(("parallel", "arbitrary")),
        name="merge",
    )(ha, hb, proj, proj, wa, wb)


def _matmul_norm_res_kernel(a_ref, w_ref, g_ref, x_ref, o_ref):
    y = jnp.dot(a_ref[...], w_ref[...], preferred_element_type=F32)
    o_ref[...] = x_ref[...] + _rms(y, g_ref[...])


def _matmul_norm_res(a, w, g, resid, *, tm, name):
    t, kdim = a.shape
    n = w.shape[1]
    return pl.pallas_call(
        _matmul_norm_res_kernel,
        out_shape=jax.ShapeDtypeStruct((t, n), F32),
        grid=(t // tm,),
        in_specs=[pl.BlockSpec((tm, kdim), lambda i: (i, 0)),
                  pl.BlockSpec((kdim, n), lambda i: (0, 0)),
                  pl.BlockSpec((1, n), lambda i: (0, 0)),
                  pl.BlockSpec((tm, n), lambda i: (i, 0))],
        out_specs=pl.BlockSpec((tm, n), lambda i: (i, 0)),
        compiler_params=_params(("parallel",)),
        name=name,
    )(a, w, g, resid)


def _xattn_kernel(x_ref, gpre_ref, wq_ref, kv_ref, wo_ref, gpost_ref, o_ref):
    x = x_ref[...]
    h = _bf(_rms(x, gpre_ref[...]))
    q = jnp.dot(h, wq_ref[...], preferred_element_type=F32)
    scale = XATTN_HEAD_DIM ** -0.5
    outs = []
    for hd in range(XATTN_HEADS):
        sl = slice(hd * XATTN_HEAD_DIM, (hd + 1) * XATTN_HEAD_DIM)
        k = kv_ref[0, :, sl]
        v = kv_ref[0, :, XATTN_WIDTH + hd * XATTN_HEAD_DIM:XATTN_WIDTH + (hd + 1) * XATTN_HEAD_DIM]
        sc = _mm_nt(q[:, sl], k) * scale
        sc = sc - jnp.max(sc, axis=-1, keepdims=True)
        e = jnp.exp(sc)
        p = e / jnp.sum(e, axis=-1, keepdims=True)
        outs.append(_mm(p, v))
    o = jnp.concatenate(outs, axis=-1)
    y = jnp.dot(_bf(o), wo_ref[...], preferred_element_type=F32)
    o_ref[...] = x + _rms(y, gpost_ref[...])


def _xattn(x, gpre, wq, kv3, wo, gpost, *, tm, seq):
    t, d = x.shape
    per_seq = seq // tm
    return pl.pallas_call(
        _xattn_kernel,
        out_shape=jax.ShapeDtypeStruct((t, d), F32),
        grid=(t // tm,),
        in_specs=[pl.BlockSpec((tm, d), lambda i: (i, 0)),
                  pl.BlockSpec((1, d), lambda i: (0, 0)),
                  pl.BlockSpec((d, XATTN_WIDTH), lambda i: (0, 0)),
                  pl.BlockSpec((1, MEM_LEN, 2 * XATTN_WIDTH), lambda i: (i // per_seq, 0, 0)),
                  pl.BlockSpec((XATTN_WIDTH, d), lambda i: (0, 0)),
                  pl.BlockSpec((1, d), lambda i: (0, 0))],
        out_specs=pl.BlockSpec((tm, d), lambda i: (i, 0)),
        compiler_params=_params(("parallel",)),
        name="xattn",
    )(x, gpre, wq, kv3, wo, gpost)


FFN_HALO = 8


def _gelu_tanh(x):
    return 0.5 * x * (1.0 + jnp.tanh(0.7978845608028654 * (x + 0.044715 * x * x * x)))


def _ffn_kernel(x_ref, halo_ref, gpre_ref, wg_ref, wu_ref, cwg_ref, cwu_ref, cbg_ref, cbu_ref, wd_ref, gpost_ref,
                o_ref, h_ref, acc_ref, *, tiles_per_seq):
    i = pl.program_id(0)
    j = pl.program_id(1)
    tm = x_ref.shape[0]

    @pl.when(j == 0)
    def _():
        keep = jnp.where(i % tiles_per_seq == 0, 0.0, 1.0)
        h_ref[0:FFN_HALO, :] = _bf(_rms(halo_ref[...], gpre_ref[...]) * keep)
        h_ref[FFN_HALO:, :] = _bf(_rms(x_ref[...], gpre_ref[...]))
        acc_ref[...] = jnp.zeros_like(acc_ref)

    def conv(w_ref, cw_ref, cb_ref):
        u = jnp.dot(h_ref[...], w_ref[...], preferred_element_type=F32)
        u1 = pltpu.roll(u, 1, 0)
        u2 = pltpu.roll(u, 2, 0)
        cw = cw_ref[...]
        full = cb_ref[...] + cw[2:3, :] * u + cw[1:2, :] * u1 + cw[0:1, :] * u2
        return full[FFN_HALO:, :]

    gate = conv(wg_ref, cwg_ref, cbg_ref)
    up = conv(wu_ref, cwu_ref, cbu_ref)
    act = _bf(_gelu_tanh(gate) * up)
    acc_ref[...] += jnp.dot(act, wd_ref[...], preferred_element_type=F32)

    @pl.when(j == pl.num_programs(1) - 1)
    def _():
        o_ref[...] = x_ref[...] + _rms(acc_ref[...], gpost_ref[...])


def _ffn(x, gpre, w_up, conv_w, conv_b, w_down, gpost, *, tm, tn, seq):
    t, d = x.shape
    nj = D_FF // tn
    hb = tm // FFN_HALO
    return pl.pallas_call(
        functools.partial(_ffn_kernel, tiles_per_seq=seq // tm),
        out_shape=jax.ShapeDtypeStruct((t, d), F32),
        grid=(t // tm, nj),
        in_specs=[pl.BlockSpec((tm, d), lambda i, j: (i, 0)),
                  pl.BlockSpec((FFN_HALO, d), lambda i, j: (jnp.maximum(i * hb - 1, 0), 0)),
                  pl.BlockSpec((1, d), lambda i, j: (0, 0)),
                  pl.BlockSpec((d, tn), lambda i, j: (0, j)),
                  pl.BlockSpec((d, tn), lambda i, j: (0, nj + j)),
                  pl.BlockSpec((CONV_WIDTH, tn), lambda i, j: (0, j)),
                  pl.BlockSpec((CONV_WIDTH, tn), lambda i, j: (0, nj + j)),
                  pl.BlockSpec((1, tn), lambda i, j: (0, j)),
                  pl.BlockSpec((1, tn), lambda i, j: (0, nj + j)),
                  pl.BlockSpec((tn, d), lambda i, j: (j, 0)),
                  pl.BlockSpec((1, d), lambda i, j: (0, 0))],
        out_specs=pl.BlockSpec((tm, d), lambda i, j: (i, 0)),
        scratch_shapes=[pltpu.VMEM((tm + FFN_HALO, d), BF16), pltpu.VMEM((tm, d), F32)],
        compiler_params=_params(("parallel", "arbitrary")),
        name="conv_glu_ffn",
    )(x, x, gpre, w_up, w_up, conv_w, conv_w, conv_b, conv_b, w_down, gpost)


def _pack_in_proj(w_in):
    d = w_in.shape[0]
    m0, r0, g0 = 0, MLSTM_TOTAL, MLSTM_TOTAL + RWKV_TOTAL
    mlstm_main = w_in[:, m0:m0 + 2 * MLSTM_QK_WIDTH + 2 * MLSTM_WIDTH]
    mlstm_if = w_in[:, m0 + 2 * MLSTM_QK_WIDTH + 2 * MLSTM_WIDTH:r0]
    rwkv_main = w_in[:, r0:r0 + 3 * RWKV_WIDTH]
    rwkv_wa = w_in[:, r0 + 3 * RWKV_WIDTH:r0 + 3 * RWKV_WIDTH + WA_W]
    rwkv_g = w_in[:, r0 + 3 * RWKV_WIDTH + WA_W:g0]
    gates = w_in[:, g0:]
    z = lambda n: jnp.zeros((d, n), w_in.dtype)
    packed = jnp.concatenate([mlstm_main, rwkv_main, gates, rwkv_g, z(RG_PAD - RWKV_GATE_RANK), rwkv_wa,
                              mlstm_if, z(IF_W - 2 * MLSTM_HEADS)], axis=1)
    return packed.astype(BF16)


def _layer(x, mem, mix_pre_norm, w_in, mlstm_b_i, mlstm_b_f, mlstm_head_norm, rwkv_mu, rwkv_w0, rwkv_w_up, rwkv_a0,
           rwkv_a_up, rwkv_g_up, rwkv_k_k, rwkv_k_a, rwkv_r_k, rwkv_ln_g, rwkv_ln_b, w_branch_a, w_branch_b,
           w_mix_out, mix_post_norm, xattn_pre_norm, mem_norm, xattn_wq, xattn_wkv, xattn_wo, xattn_post_norm,
           ffn_pre_norm, ffn_w_up, ffn_conv_w, ffn_conv_b, ffn_w_down, ffn_post_norm):
    b, s, d = x.shape
    t = b * s
    x2d = x.reshape(t, d)
    row = lambda p: p.reshape(1, -1).astype(F32)

    w_all = _pack_in_proj(w_in)
    tm_in = min(1024, t)
    proj = _norm_matmul(x2d, row(mix_pre_norm), w_all, tm=tm_in, tn=512, name="in_proj")
    proj3 = proj.reshape(b, s, PROJ_COLS)

    gate_bias = jnp.concatenate([mlstm_b_i, mlstm_b_f, jnp.zeros((IF_W - 2 * MLSTM_HEADS,), F32)]).reshape(1, IF_W)
    ha = _mlstm(proj3, gate_bias, row(mlstm_head_norm))

    mu_r, mu_k, mu_v = (rwkv_mu[i * RWKV_WIDTH:(i + 1) * RWKV_WIDTH] for i in range(3))
    mu_wa = rwkv_mu[3 * RWKV_WIDTH:3 * RWKV_WIDTH + WA_W]
    mu_g = jnp.pad(rwkv_mu[3 * RWKV_WIDTH + WA_W:], (0, RG_PAD - RWKV_GATE_RANK))
    pvec = jnp.stack([mu_r, mu_k, mu_v, rwkv_w0, rwkv_a0, rwkv_k_k, rwkv_k_a, rwkv_r_k.reshape(-1), rwkv_ln_g,
                      rwkv_ln_b] + [jnp.zeros((RWKV_WIDTH,), F32)] * (PV_ROWS - 10))
    g_up = jnp.pad(rwkv_g_up, ((0, RG_PAD - RWKV_GATE_RANK), (0, 0)))
    hb = _rwkv(proj3, pvec, row(mu_g), row(mu_wa), _bf(rwkv_w_up), _bf(rwkv_a_up), _bf(g_up))

    tm = min(512, s)
    merged = _merge(ha.reshape(t, -1), hb.reshape(t, -1), proj, _bf(w_branch_a), _bf(w_branch_b), tm=tm, tn=512)
    x1 = _matmul_norm_res(merged, _bf(w_mix_out), row(mix_post_norm), x2d, tm=tm, name="mix_out")

    mem2d = mem.reshape(b * MEM_LEN, d)
    kv = _norm_matmul(mem2d, row(mem_norm), _bf(xattn_wkv), tm=MEM_LEN, tn=512, name="mem_kv")
    x2 = _xattn(x1, row(xattn_pre_norm), _bf(xattn_wq), kv.reshape(b, MEM_LEN, 2 * XATTN_WIDTH), _bf(xattn_wo),
                row(xattn_post_norm), tm=tm, seq=s)

    x3 = _ffn(x2, row(ffn_pre_norm), _bf(ffn_w_up), ffn_conv_w, row(ffn_conv_b), _bf(ffn_w_down),
              row(ffn_post_norm), tm=tm, tn=512, seq=s)
    return x3.reshape(b, s, d)


def kernel(x, mem, mix_pre_norm, w_in, mlstm_b_i, mlstm_b_f, mlstm_head_norm, rwkv_mu, rwkv_w0, rwkv_w_up, rwkv_a0, rwkv_a_up, rwkv_g_up, rwkv_k_k, rwkv_k_a, rwkv_r_k, rwkv_ln_g, rwkv_ln_b, w_branch_a, w_branch_b, w_mix_out, mix_post_norm, xattn_pre_norm, mem_norm, xattn_wq, xattn_wkv, xattn_wo, xattn_post_norm, ffn_pre_norm, ffn_w_up, ffn_conv_w, ffn_conv_b, ffn_w_down, ffn_post_norm):
    for l in range(mix_pre_norm.shape[0]):
        x = _layer(x, mem, mix_pre_norm[l], w_in[l], mlstm_b_i[l], mlstm_b_f[l], mlstm_head_norm[l], rwkv_mu[l],
                   rwkv_w0[l], rwkv_w_up[l], rwkv_a0[l], rwkv_a_up[l], rwkv_g_up[l], rwkv_k_k[l], rwkv_k_a[l],
                   rwkv_r_k[l], rwkv_ln_g[l], rwkv_ln_b[l], w_branch_a[l], w_branch_b[l], w_mix_out[l],
                   mix_post_norm[l], xattn_pre_norm[l], mem_norm[l], xattn_wq[l], xattn_wkv[l], xattn_wo[l],
                   xattn_post_norm[l], ffn_pre_norm[l], ffn_w_up[l], ffn_conv_w[l], ffn_conv_b[l], ffn_w_down[l],
                   ffn_post_norm[l])
    return x
```

```python
import functools

import jax
import jax.numpy as jnp
from jax import lax
from jax.experimental import pallas as pl
from jax.experimental.pallas import tpu as pltpu

D_MODEL = 2048
MEM_LEN = 256
RMS_EPS = 1e-6

MLSTM_HEADS = 4
MLSTM_WIDTH = D_MODEL // 2
MLSTM_V_DIM = MLSTM_WIDTH // MLSTM_HEADS
MLSTM_QK_DIM = MLSTM_V_DIM // 2
MLSTM_QK_WIDTH = MLSTM_HEADS * MLSTM_QK_DIM
GATE_SOFTCAP = 15.0

RWKV_WIDTH = D_MODEL // 2
RWKV_HEAD = 64
RWKV_HEADS = RWKV_WIDTH // RWKV_HEAD
RWKV_DECAY_RANK = 64
RWKV_A_RANK = 64
RWKV_GATE_RANK = 160
RWKV_GN_EPS = 64e-5

MLSTM_TOTAL = 2 * MLSTM_QK_WIDTH + 2 * MLSTM_WIDTH + 2 * MLSTM_HEADS
RWKV_TOTAL = 3 * RWKV_WIDTH + RWKV_DECAY_RANK + RWKV_A_RANK + RWKV_GATE_RANK

XATTN_HEADS = 4
XATTN_HEAD_DIM = 128
XATTN_WIDTH = XATTN_HEADS * XATTN_HEAD_DIM

D_FF = 4 * D_MODEL
CONV_WIDTH = 3

COL_MQ = 0
COL_MK = COL_MQ + MLSTM_QK_WIDTH
COL_MV = COL_MK + MLSTM_QK_WIDTH
COL_MO = COL_MV + MLSTM_WIDTH
COL_RR = COL_MO + MLSTM_WIDTH
COL_RK = COL_RR + RWKV_WIDTH
COL_RV = COL_RK + RWKV_WIDTH
COL_GA = COL_RV + RWKV_WIDTH
COL_GB = COL_GA + D_MODEL
COL_RG = COL_GB + D_MODEL
RG_PAD = 256
COL_WA = COL_RG + RG_PAD
WA_W = RWKV_DECAY_RANK + RWKV_A_RANK
COL_IF = COL_WA + WA_W
IF_W = 128
PROJ_COLS = COL_IF + IF_W

CHUNK = 64
RWKV_GROUP_W = 256
RWKV_GROUPS = RWKV_WIDTH // RWKV_GROUP_W
VMEM_LIMIT = 56 * 1024 * 1024

F32 = jnp.float32
BF16 = jnp.bfloat16


def _bf(x):
    return x.astype(BF16)


def _mm(a, b):
    return jnp.dot(_bf(a), _bf(b), preferred_element_type=F32)


def _mm_nt(a, b):
    return lax.dot_general(_bf(a), _bf(b), (((1,), (1,)), ((), ())), preferred_element_type=F32)


def _mm_tn(a, b):
    return lax.dot_general(_bf(a), _bf(b), (((0,), (0,)), ((), ())), preferred_element_type=F32)


def _mm_exact_lhs(tri, x):
    hi = _bf(x)
    r1 = x - hi.astype(F32)
    mid = _bf(r1)
    lo = _bf(r1 - mid.astype(F32))
    t = _bf(tri)
    return (jnp.dot(t, hi, preferred_element_type=F32) + jnp.dot(t, mid, preferred_element_type=F32)
            + jnp.dot(t, lo, preferred_element_type=F32))


def _rms(x, g):
    return x * lax.rsqrt(jnp.mean(x * x, axis=-1, keepdims=True) + RMS_EPS) * g


def _softplus(x):
    return jnp.maximum(x, 0.0) + jnp.log1p(jnp.exp(-jnp.abs(x)))


def _sigmoid(x):
    return 1.0 / (1.0 + jnp.exp(-x))


def _params(sem):
    return pltpu.CompilerParams(dimension_semantics=sem, vmem_limit_bytes=VMEM_LIMIT)


def _norm_matmul_kernel(x_ref, g_ref, w_ref, o_ref, h_ref):
    @pl.when(pl.program_id(1) == 0)
    def _():
        h_ref[...] = _bf(_rms(x_ref[...], g_ref[...]))

    o_ref[...] = jnp.dot(h_ref[...], w_ref[...], preferred_element_type=F32).astype(o_ref.dtype)


def _norm_matmul(x, g, w, *, tm, tn, name):
    t, d = x.shape
    n = w.shape[1]
    return pl.pallas_call(
        _norm_matmul_kernel,
        out_shape=jax.ShapeDtypeStruct((t, n), F32),
        grid=(t // tm, n // tn),
        in_specs=[pl.BlockSpec((tm, d), lambda i, j: (i, 0)),
                  pl.BlockSpec((1, d), lambda i, j: (0, 0)),
                  pl.BlockSpec((d, tn), lambda i, j: (0, j))],
        out_specs=pl.BlockSpec((tm, tn), lambda i, j: (i, j)),
        scratch_shapes=[pltpu.VMEM((tm, d), BF16)],
        compiler_params=_params(("parallel", "arbitrary")),
        name=name,
    )(x, g, w)


def _mlstm_kernel(q_ref, k_ref, v_ref, o_ref, gate_ref, bias_ref, hn_ref, out_ref, ct_ref, n_ref, m_ref):
    L = CHUNK
    dk, dv = MLSTM_QK_DIM, MLSTM_V_DIM

    @pl.when(pl.program_id(1) == 0)
    def _():
        ct_ref[...] = jnp.zeros_like(ct_ref)
        n_ref[...] = jnp.zeros_like(n_ref)
        m_ref[...] = jnp.zeros_like(m_ref)

    pre = gate_ref[0] + bias_ref[...]
    capped = GATE_SOFTCAP * jnp.tanh(pre / GATE_SOFTCAP)
    logf = -_softplus(-capped)
    row = lax.broadcasted_iota(jnp.int32, (L, L), 0)
    col = lax.broadcasted_iota(jnp.int32, (L, L), 1)
    causal = row >= col
    bcum = _mm_exact_lhs(causal.astype(F32), logf)
    ig_t = capped.T
    bcum_t = bcum.T
    scale = MLSTM_QK_DIM ** -0.5

    for h in range(MLSTM_HEADS):
        q = q_ref[0, :, h * dk:(h + 1) * dk]
        k = k_ref[0, :, h * dk:(h + 1) * dk] * scale
        v = v_ref[0, :, h * dv:(h + 1) * dv]
        b_c = bcum[:, MLSTM_HEADS + h:MLSTM_HEADS + h + 1]
        b_r = bcum_t[MLSTM_HEADS + h:MLSTM_HEADS + h + 1, :]
        i_c = capped[:, h:h + 1]
        i_r = ig_t[h:h + 1, :]
        m_prev = m_ref[h][0:1, 0:1]
        ct = ct_ref[h]
        nrow = n_ref[h][0:1, :]

        dmat = jnp.where(causal, b_c - b_r + i_r, -jnp.inf)
        inter = b_c + m_prev
        m_t = jnp.maximum(inter, jnp.max(dmat, axis=-1, keepdims=True))
        dexp = jnp.exp(dmat - m_t)
        w_inter = jnp.exp(inter - m_t)
        s = _mm_nt(q, k) * dexp
        num = w_inter * _mm(q, ct) + _mm(s, v)
        den = w_inter * jnp.sum(q * nrow, axis=-1, keepdims=True) + jnp.sum(s, axis=-1, keepdims=True)
        hh = num / jnp.maximum(jnp.abs(den), jnp.exp(-m_t))

        b_last = b_c[L - 1:L, :]
        gs = b_last - b_c + i_c
        m_new = jnp.maximum(b_last + m_prev, jnp.max(gs, axis=0, keepdims=True))
        carry_w = jnp.exp(b_last + m_prev - m_new)
        ws = jnp.exp(gs - m_new)
        ct_ref[h] = carry_w * ct + _mm_tn(k, ws * v)
        n_ref[h] = jnp.broadcast_to(carry_w * nrow + jnp.sum(ws * k, axis=0, keepdims=True), n_ref.shape[1:])
        m_ref[h] = jnp.broadcast_to(m_new, m_ref.shape[1:])

        hm = hh * lax.rsqrt(jnp.mean(hh * hh, axis=-1, keepdims=True) + RMS_EPS) * hn_ref[:, h * dv:(h + 1) * dv]
        og = o_ref[0, :, h * dv:(h + 1) * dv]
        out_ref[0, :, h * dv:(h + 1) * dv] = (_sigmoid(og) * hm).astype(out_ref.dtype)


def _mlstm(proj3, gate_bias, head_norm):
    b, s, _ = proj3.shape
    L = CHUNK
    qw, vw = MLSTM_QK_WIDTH, MLSTM_WIDTH
    return pl.pallas_call(
        _mlstm_kernel,
        out_shape=jax.ShapeDtypeStruct((b, s, MLSTM_WIDTH), BF16),
        grid=(b, s // L),
        in_specs=[pl.BlockSpec((1, L, qw), lambda i, c: (i, c, COL_MQ // qw)),
                  pl.BlockSpec((1, L, qw), lambda i, c: (i, c, COL_MK // qw)),
                  pl.BlockSpec((1, L, vw), lambda i, c: (i, c, COL_MV // vw)),
                  pl.BlockSpec((1, L, vw), lambda i, c: (i, c, COL_MO // vw)),
                  pl.BlockSpec((1, L, IF_W), lambda i, c: (i, c, COL_IF // IF_W)),
                  pl.BlockSpec((1, IF_W), lambda i, c: (0, 0)),
                  pl.BlockSpec((1, vw), lambda i, c: (0, 0))],
        out_specs=pl.BlockSpec((1, L, vw), lambda i, c: (i, c, 0)),
        scratch_shapes=[pltpu.VMEM((MLSTM_HEADS, MLSTM_QK_DIM, MLSTM_V_DIM), F32),
                        pltpu.VMEM((MLSTM_HEADS, 8, MLSTM_QK_DIM), F32),
                        pltpu.VMEM((MLSTM_HEADS, 8, 128), F32)],
        compiler_params=_params(("parallel", "arbitrary")),
        name="mlstm",
    )(proj3, proj3, proj3, proj3, proj3, gate_bias, head_norm)


PV_MU_R, PV_MU_K, PV_MU_V, PV_W0, PV_A0, PV_KK, PV_KA, PV_RK, PV_LNG, PV_LNB = range(10)
PV_ROWS = 16


def _rwkv_kernel(r_ref, k_ref, v_ref, g_ref, wa_ref, pv_ref, mug_ref, muwa_ref, wup_ref, aup_ref, gup_ref, seg_ref,
                 out_ref, s_ref, pr_ref, pk_ref, pvv_ref, pg_ref, pwa_ref):
    L = CHUNK
    N = RWKV_HEAD
    GW = RWKV_GROUP_W
    GH = GW // N

    @pl.when(pl.program_id(1) == 0)
    def _():
        s_ref[...] = jnp.zeros_like(s_ref)
        pr_ref[...] = jnp.zeros_like(pr_ref)
        pk_ref[...] = jnp.zeros_like(pk_ref)
        pvv_ref[...] = jnp.zeros_like(pvv_ref)
        pg_ref[...] = jnp.zeros_like(pg_ref)
        pwa_ref[...] = jnp.zeros_like(pwa_ref)

    def shift_lerp(x_ref, prev_ref, mu):
        x = x_ref[0]
        rid = lax.broadcasted_iota(jnp.int32, x.shape, 0)
        xs = jnp.where(rid == 0, prev_ref[0:1, :], pltpu.roll(x, 1, 0))
        prev_ref[0:1, :] = x[L - 1:L, :]
        return x + (xs - x) * mu

    pv = pv_ref[...]
    row = lambda i: pv[i:i + 1, :]
    r = shift_lerp(r_ref, pr_ref, row(PV_MU_R))
    kr = shift_lerp(k_ref, pk_ref, row(PV_MU_K))
    v = shift_lerp(v_ref, pvv_ref, row(PV_MU_V))
    gl = shift_lerp(g_ref, pg_ref, mug_ref[...])
    wa = shift_lerp(wa_ref, pwa_ref, muwa_ref[...])

    wl = jnp.tanh(wa[:, 0:RWKV_DECAY_RANK])
    al = wa[:, RWKV_DECAY_RANK:WA_W]
    w_log = -_softplus(-(row(PV_W0) + jnp.dot(_bf(wl), wup_ref[...], preferred_element_type=F32))) - 0.5
    lw = -jnp.exp(w_log)
    a = _sigmoid(row(PV_A0) + jnp.dot(_bf(al), aup_ref[...], preferred_element_type=F32))
    gg = jnp.dot(_bf(_sigmoid(gl)), gup_ref[...], preferred_element_type=F32)

    tri = lax.broadcasted_iota(jnp.int32, (L, L), 0) >= lax.broadcasted_iota(jnp.int32, (L, L), 1)
    cs = _mm_exact_lhs(tri.astype(F32), lw)
    g_incl = jnp.exp(cs)
    g_excl = jnp.exp(cs - lw)
    g_inv = jnp.exp(-cs)
    g_last = g_incl[L - 1:L, :]

    kk0 = kr * row(PV_KK)
    kr2 = kr * (1.0 + (a - 1.0) * row(PV_KA))
    rt_all = r * g_incl
    kt_all = kr2 * g_inv
    bonus_all = r * kr2 * row(PV_RK)

    seg = seg_ref[...]
    ti = lax.broadcasted_iota(jnp.int32, (L, GW), 0)
    lane = lax.broadcasted_iota(jnp.int32, (L, GW), 1)
    si = lane & (N - 1)
    lane_head = lane // N
    incl = ti >= si
    strict = ti > si
    eye = (ti == si).astype(F32)
    lvl_masks = []
    bsz = 1
    while bsz < L:
        same = (ti // (2 * bsz)) == (si // (2 * bsz))
        lvl_masks.append(jnp.where(same & ((ti & bsz) != 0) & ((si & bsz) == 0), 1.0, 0.0))
        bsz *= 2

    def bd(x):
        return jnp.concatenate([_bf(x)] * GH, axis=0) * seg

    def fold(z):
        acc = z[0:N]
        for hh in range(1, GH):
            acc = jnp.where(lane_head == hh, z[hh * N:(hh + 1) * N], acc)
        return acc

    def mmb(x, b):
        return jnp.dot(_bf(x), b, preferred_element_type=F32)

    def mmb_nt(x, b):
        return lax.dot_general(_bf(x), b, (((1,), (1,)), ((), ())), preferred_element_type=F32)

    def segsum(x):
        hi = _bf(x)
        lo = _bf(x - hi.astype(F32))
        return jnp.dot(hi, seg, preferred_element_type=F32) + jnp.dot(lo, seg, preferred_element_type=F32)

    cat = lambda u, w_: jnp.concatenate([u, w_], axis=0)
    groups = range(RWKV_GROUPS)
    grp = lambda x, g: x[:, g * GW:(g + 1) * GW]

    nrm2 = [segsum(grp(kk0, g) * grp(kk0, g)) for g in groups]
    kk = [grp(kk0, g) / jnp.maximum(jnp.sqrt(nrm2[g]), 1e-12) for g in groups]
    p = [grp(g_excl, g) * kk[g] for g in groups]
    qt = [kk[g] * grp(a, g) * grp(g_inv, g) for g in groups]
    kt = [grp(kt_all, g) for g in groups]
    rt = [grp(rt_all, g) for g in groups]
    vv = [grp(v, g) for g in groups]
    gl_ = [grp(g_last, g) for g in groups]
    qg = [qt[g] * gl_[g] for g in groups]
    kg = [kt[g] * gl_[g] for g in groups]

    lhs_pr = [cat(p[g], rt[g]) for g in groups]
    gq = [mmb_nt(lhs_pr[g], bd(qt[g])) for g in groups]
    gk = [mmb_nt(lhs_pr[g], bd(kt[g])) for g in groups]
    n_pq = [jnp.where(strict, gq[g][:L], 0.0) for g in groups]
    a_rq = [jnp.where(incl, gq[g][L:], 0.0) for g in groups]
    a_pk = [jnp.where(strict, gk[g][:L], 0.0) for g in groups]
    a_rk = [jnp.where(incl, gk[g][L:], 0.0) for g in groups]

    x = [eye - n_pq[g] * lvl_masks[0] for g in groups]
    for msk in lvl_masks[1:]:
        t1 = [mmb(x[g], bd(n_pq[g] * msk)) for g in groups]
        x = [x[g] - mmb(t1[g], bd(x[g])) for g in groups]

    av = [mmb(cat(a_pk[g], a_rk[g]), bd(vv[g])) for g in groups]
    w = [mmb(x[g], bd(p[g])) for g in groups]
    u0 = [mmb(x[g], bd(av[g][:L])) for g in groups]
    gmat = [rt[g] - mmb(a_rq[g], bd(w[g])) for g in groups]
    y0 = [av[g][L:] - mmb(a_rq[g], bd(u0[g])) for g in groups]
    mt = [eye * gl_[g] - fold(_mm_tn(qg[g], w[g])) for g in groups]
    bt = [fold(_mm_tn(cat(kg[g], -qg[g]), cat(vv[g], u0[g]))) for g in groups]

    ys = [mmb(cat(gmat[g], mt[g]), bd(s_ref[g])) for g in groups]
    y = [ys[g][:L] + y0[g] for g in groups]
    for g in groups:
        s_ref[g] = ys[g][L:] + bt[g]

    inv_n = 1.0 / N
    mean = [segsum(y[g]) * inv_n for g in groups]
    yc = [y[g] - mean[g] for g in groups]
    var = [segsum(yc[g] * yc[g]) * inv_n for g in groups]
    bonus = [segsum(grp(bonus_all, g)) for g in groups]
    for g in groups:
        yn = yc[g] * lax.rsqrt(var[g] + RWKV_GN_EPS) * grp(row(PV_LNG), g) + grp(row(PV_LNB), g)
        yn = yn + bonus[g] * vv[g]
        out_ref[0, :, g * GW:(g + 1) * GW] = (yn * grp(gg, g)).astype(out_ref.dtype)


def _rwkv(proj3, pvec, mu_g, mu_wa, w_up, a_up, g_up):
    b, s, _ = proj3.shape
    L = CHUNK
    rw = RWKV_WIDTH
    head_of = jnp.arange(RWKV_GROUP_W) // RWKV_HEAD
    seg = (head_of[:, None] == head_of[None, :]).astype(BF16)
    const = lambda shape: pl.BlockSpec(shape, lambda i, c: (0, 0))
    return pl.pallas_call(
        _rwkv_kernel,
        out_shape=jax.ShapeDtypeStruct((b, s, rw), BF16),
        grid=(b, s // L),
        in_specs=[pl.BlockSpec((1, L, rw), lambda i, c: (i, c, COL_RR // rw)),
                  pl.BlockSpec((1, L, rw), lambda i, c: (i, c, COL_RK // rw)),
                  pl.BlockSpec((1, L, rw), lambda i, c: (i, c, COL_RV // rw)),
                  pl.BlockSpec((1, L, RG_PAD), lambda i, c: (i, c, COL_RG // RG_PAD)),
                  pl.BlockSpec((1, L, WA_W), lambda i, c: (i, c, COL_WA // WA_W)),
                  const((PV_ROWS, rw)), const((1, RG_PAD)), const((1, WA_W)),
                  const((RWKV_DECAY_RANK, rw)), const((RWKV_A_RANK, rw)), const((RG_PAD, rw)),
                  const((RWKV_GROUP_W, RWKV_GROUP_W))],
        out_specs=pl.BlockSpec((1, L, rw), lambda i, c: (i, c, 0)),
        scratch_shapes=[pltpu.VMEM((RWKV_GROUPS, RWKV_HEAD, RWKV_GROUP_W), F32),
                        pltpu.VMEM((8, rw), F32), pltpu.VMEM((8, rw), F32), pltpu.VMEM((8, rw), F32),
                        pltpu.VMEM((8, RG_PAD), F32), pltpu.VMEM((8, WA_W), F32)],
        compiler_params=_params(("parallel", "arbitrary")),
        name="rwkv7",
    )(proj3, proj3, proj3, proj3, proj3, pvec, mu_g, mu_wa, w_up, a_up, g_up, seg)


def _merge_kernel(ha_ref, hb_ref, ga_ref, gb_ref, wa_ref, wb_ref, o_ref):
    ya = jnp.dot(ha_ref[...], wa_ref[...], preferred_element_type=F32)
    yb = jnp.dot(hb_ref[...], wb_ref[...], preferred_element_type=F32)
    o_ref[...] = (_sigmoid(ga_ref[...]) * ya + _sigmoid(gb_ref[...]) * yb).astype(o_ref.dtype)


def _merge(ha, hb, proj, wa, wb, *, tm, tn):
    t, kdim = ha.shape
    n = wa.shape[1]
    return pl.pallas_call(
        _merge_kernel,
        out_shape=jax.ShapeDtypeStruct((t, n), BF16),
        grid=(t // tm, n // tn),
        in_specs=[pl.BlockSpec((tm, kdim), lambda i, j: (i, 0)),
                  pl.BlockSpec((tm, kdim), lambda i, j: (i, 0)),
                  pl.BlockSpec((tm, tn), lambda i, j: (i, COL_GA // tn + j)),
                  pl.BlockSpec((tm, tn), lambda i, j: (i, COL_GB // tn + j)),
                  pl.BlockSpec((kdim, tn), lambda i, j: (0, j)),
                  pl.BlockSpec((kdim, tn), lambda i, j: (0, j))],
        out_specs=pl.BlockSpec((tm, tn), lambda i, j: (i, j)),
        compiler_params=_params(("parallel", "arbitrary")),
        name="merge",
    )(ha, hb, proj, proj, wa, wb)


def _matmul_norm_res_kernel(a_ref, w_ref, g_ref, x_ref, o_ref):
    y = jnp.dot(a_ref[...], w_ref[...], preferred_element_type=F32)
    o_ref[...] = x_ref[...] + _rms(y, g_ref[...])


def _matmul_norm_res(a, w, g, resid, *, tm, name):
    t, kdim = a.shape
    n = w.shape[1]
    return pl.pallas_call(
        _matmul_norm_res_kernel,
        out_shape=jax.ShapeDtypeStruct((t, n), F32),
        grid=(t // tm,),
        in_specs=[pl.BlockSpec((tm, kdim), lambda i: (i, 0)),
                  pl.BlockSpec((kdim, n), lambda i: (0, 0)),
                  pl.BlockSpec((1, n), lambda i: (0, 0)),
                  pl.BlockSpec((tm, n), lambda i: (i, 0))],
        out_specs=pl.BlockSpec((tm, n), lambda i: (i, 0)),
        compiler_params=_params(("parallel",)),
        name=name,
    )(a, w, g, resid)


def _xattn_kernel(x_ref, gpre_ref, wq_ref, kv_ref, wo_ref, gpost_ref, o_ref):
    x = x_ref[...]
    h = _bf(_rms(x, gpre_ref[...]))
    q = jnp.dot(h, wq_ref[...], preferred_element_type=F32)
    scale = XATTN_HEAD_DIM ** -0.5
    outs = []
    for hd in range(XATTN_HEADS):
        sl = slice(hd * XATTN_HEAD_DIM, (hd + 1) * XATTN_HEAD_DIM)
        k = kv_ref[0, :, sl]
        v = kv_ref[0, :, XATTN_WIDTH + hd * XATTN_HEAD_DIM:XATTN_WIDTH + (hd + 1) * XATTN_HEAD_DIM]
        sc = _mm_nt(q[:, sl], k) * scale
        sc = sc - jnp.max(sc, axis=-1, keepdims=True)
        e = jnp.exp(sc)
        p = e / jnp.sum(e, axis=-1, keepdims=True)
        outs.append(_mm(p, v))
    o = jnp.concatenate(outs, axis=-1)
    y = jnp.dot(_bf(o), wo_ref[...], preferred_element_type=F32)
    o_ref[...] = x + _rms(y, gpost_ref[...])


def _xattn(x, gpre, wq, kv3, wo, gpost, *, tm, seq):
    t, d = x.shape
    per_seq = seq // tm
    return pl.pallas_call(
        _xattn_kernel,
        out_shape=jax.ShapeDtypeStruct((t, d), F32),
        grid=(t // tm,),
        in_specs=[pl.BlockSpec((tm, d), lambda i: (i, 0)),
                  pl.BlockSpec((1, d), lambda i: (0, 0)),
                  pl.BlockSpec((d, XATTN_WIDTH), lambda i: (0, 0)),
                  pl.BlockSpec((1, MEM_LEN, 2 * XATTN_WIDTH), lambda i: (i // per_seq, 0, 0)),
                  pl.BlockSpec((XATTN_WIDTH, d), lambda i: (0, 0)),
                  pl.BlockSpec((1, d), lambda i: (0, 0))],
        out_specs=pl.BlockSpec((tm, d), lambda i: (i, 0)),
        compiler_params=_params(("parallel",)),
        name="xattn",
    )(x, gpre, wq, kv3, wo, gpost)


FFN_HALO = 8


def _gelu_tanh(x):
    return 0.5 * x * (1.0 + jnp.tanh(0.7978845608028654 * (x + 0.044715 * x * x * x)))


def _ffn_kernel(x_ref, halo_ref, gpre_ref, wg_ref, wu_ref, cwg_ref, cwu_ref, cbg_ref, cbu_ref, wd_ref, gpost_ref,
                o_ref, h_ref, acc_ref, *, tiles_per_seq):
    i = pl.program_id(0)
    j = pl.program_id(1)
    tm = x_ref.shape[0]

    @pl.when(j == 0)
    def _():
        keep = jnp.where(i % tiles_per_seq == 0, 0.0, 1.0)
        h_ref[0:FFN_HALO, :] = _bf(_rms(halo_ref[...], gpre_ref[...]) * keep)
        h_ref[FFN_HALO:, :] = _bf(_rms(x_ref[...], gpre_ref[...]))
        acc_ref[...] = jnp.zeros_like(acc_ref)

    def conv(w_ref, cw_ref, cb_ref):
        u = jnp.dot(h_ref[...], w_ref[...], preferred_element_type=F32)
        u1 = pltpu.roll(u, 1, 0)
        u2 = pltpu.roll(u, 2, 0)
        cw = cw_ref[...]
        full = cb_ref[...] + cw[2:3, :] * u + cw[1:2, :] * u1 + cw[0:1, :] * u2
        return full[FFN_HALO:, :]

    gate = conv(wg_ref, cwg_ref, cbg_ref)
    up = conv(wu_ref, cwu_ref, cbu_ref)
    act = _bf(_gelu_tanh(gate) * up)
    acc_ref[...] += jnp.dot(act, wd_ref[...], preferred_element_type=F32)

    @pl.when(j == pl.num_programs(1) - 1)
    def _():
        o_ref[...] = x_ref[...] + _rms(acc_ref[...], gpost_ref[...])


def _ffn(x, gpre, w_up, conv_w, conv_b, w_down, gpost, *, tm, tn, seq):
    t, d = x.shape
    nj = D_FF // tn
    hb = tm // FFN_HALO
    return pl.pallas_call(
        functools.partial(_ffn_kernel, tiles_per_seq=seq // tm),
        out_shape=jax.ShapeDtypeStruct((t, d), F32),
        grid=(t // tm, nj),
        in_specs=[pl.BlockSpec((tm, d), lambda i, j: (i, 0)),
                  pl.BlockSpec((FFN_HALO, d), lambda i, j: (jnp.maximum(i * hb - 1, 0), 0)),
                  pl.BlockSpec((1, d), lambda i, j: (0, 0)),
                  pl.BlockSpec((d, tn), lambda i, j: (0, j)),
                  pl.BlockSpec((d, tn), lambda i, j: (0, nj + j)),
                  pl.BlockSpec((CONV_WIDTH, tn), lambda i, j: (0, j)),
                  pl.BlockSpec((CONV_WIDTH, tn), lambda i, j: (0, nj + j)),
                  pl.BlockSpec((1, tn), lambda i, j: (0, j)),
                  pl.BlockSpec((1, tn), lambda i, j: (0, nj + j)),
                  pl.BlockSpec((tn, d), lambda i, j: (j, 0)),
                  pl.BlockSpec((1, d), lambda i, j: (0, 0))],
        out_specs=pl.BlockSpec((tm, d), lambda i, j: (i, 0)),
        scratch_shapes=[pltpu.VMEM((tm + FFN_HALO, d), BF16), pltpu.VMEM((tm, d), F32)],
        compiler_params=_params(("parallel", "arbitrary")),
        name="conv_glu_ffn",
    )(x, x, gpre, w_up, w_up, conv_w, conv_w, conv_b, conv_b, w_down, gpost)


def _pack_in_proj(w_in):
    d = w_in.shape[0]
    m0, r0, g0 = 0, MLSTM_TOTAL, MLSTM_TOTAL + RWKV_TOTAL
    mlstm_main = w_in[:, m0:m0 + 2 * MLSTM_QK_WIDTH + 2 * MLSTM_WIDTH]
    mlstm_if = w_in[:, m0 + 2 * MLSTM_QK_WIDTH + 2 * MLSTM_WIDTH:r0]
    rwkv_main = w_in[:, r0:r0 + 3 * RWKV_WIDTH]
    rwkv_wa = w_in[:, r0 + 3 * RWKV_WIDTH:r0 + 3 * RWKV_WIDTH + WA_W]
    rwkv_g = w_in[:, r0 + 3 * RWKV_WIDTH + WA_W:g0]
    gates = w_in[:, g0:]
    z = lambda n: jnp.zeros((d, n), w_in.dtype)
    packed = jnp.concatenate([mlstm_main, rwkv_main, gates, rwkv_g, z(RG_PAD - RWKV_GATE_RANK), rwkv_wa,
                              mlstm_if, z(IF_W - 2 * MLSTM_HEADS)], axis=1)
    return packed.astype(BF16)


def _layer(x, mem, mix_pre_norm, w_in, mlstm_b_i, mlstm_b_f, mlstm_head_norm, rwkv_mu, rwkv_w0, rwkv_w_up, rwkv_a0,
           rwkv_a_up, rwkv_g_up, rwkv_k_k, rwkv_k_a, rwkv_r_k, rwkv_ln_g, rwkv_ln_b, w_branch_a, w_branch_b,
           w_mix_out, mix_post_norm, xattn_pre_norm, mem_norm, xattn_wq, xattn_wkv, xattn_wo, xattn_post_norm,
           ffn_pre_norm, ffn_w_up, ffn_conv_w, ffn_conv_b, ffn_w_down, ffn_post_norm):
    b, s, d = x.shape
    t = b * s
    x2d = x.reshape(t, d)
    row = lambda p: p.reshape(1, -1).astype(F32)

    w_all = _pack_in_proj(w_in)
    tm_in = min(1024, t)
    proj = _norm_matmul(x2d, row(mix_pre_norm), w_all, tm=tm_in, tn=512, name="in_proj")
    proj3 = proj.reshape(b, s, PROJ_COLS)

    gate_bias = jnp.concatenate([mlstm_b_i, mlstm_b_f, jnp.zeros((IF_W - 2 * MLSTM_HEADS,), F32)]).reshape(1, IF_W)
    ha = _mlstm(proj3, gate_bias, row(mlstm_head_norm))

    mu_r, mu_k, mu_v = (rwkv_mu[i * RWKV_WIDTH:(i + 1) * RWKV_WIDTH] for i in range(3))
    mu_wa = rwkv_mu[3 * RWKV_WIDTH:3 * RWKV_WIDTH + WA_W]
    mu_g = jnp.pad(rwkv_mu[3 * RWKV_WIDTH + WA_W:], (0, RG_PAD - RWKV_GATE_RANK))
    pvec = jnp.stack([mu_r, mu_k, mu_v, rwkv_w0, rwkv_a0, rwkv_k_k, rwkv_k_a, rwkv_r_k.reshape(-1), rwkv_ln_g,
                      rwkv_ln_b] + [jnp.zeros((RWKV_WIDTH,), F32)] * (PV_ROWS - 10))
    g_up = jnp.pad(rwkv_g_up, ((0, RG_PAD - RWKV_GATE_RANK), (0, 0)))
    hb = _rwkv(proj3, pvec, row(mu_g), row(mu_wa), _bf(rwkv_w_up), _bf(rwkv_a_up), _bf(g_up))

    tm = min(512, s)
    merged = _merge(ha.reshape(t, -1), hb.reshape(t, -1), proj, _bf(w_branch_a), _bf(w_branch_b), tm=tm, tn=512)
    x1 = _matmul_norm_res(merged, _bf(w_mix_out), row(mix_post_norm), x2d, tm=tm, name="mix_out")

    mem2d = mem.reshape(b * MEM_LEN, d)
    kv = _norm_matmul(mem2d, row(mem_norm), _bf(xattn_wkv), tm=MEM_LEN, tn=512, name="mem_kv")
    x2 = _xattn(x1, row(xattn_pre_norm), _bf(xattn_wq), kv.reshape(b, MEM_LEN, 2 * XATTN_WIDTH), _bf(xattn_wo),
                row(xattn_post_norm), tm=tm, seq=s)

    x3 = _ffn(x2, row(ffn_pre_norm), _bf(ffn_w_up), ffn_conv_w, row(ffn_conv_b), _bf(ffn_w_down),
              row(ffn_post_norm), tm=tm, tn=512, seq=s)
    return x3.reshape(b, s, d)


def kernel(x, mem, mix_pre_norm, w_in, mlstm_b_i, mlstm_b_f, mlstm_head_norm, rwkv_mu, rwkv_w0, rwkv_w_up, rwkv_a0, rwkv_a_up, rwkv_g_up, rwkv_k_k, rwkv_k_a, rwkv_r_k, rwkv_ln_g, rwkv_ln_b, w_branch_a, w_branch_b, w_mix_out, mix_post_norm, xattn_pre_norm, mem_norm, xattn_wq, xattn_wkv, xattn_wo, xattn_post_norm, ffn_pre_norm, ffn_w_up, ffn_conv_w, ffn_conv_b, ffn_w_down, ffn_post_norm):
    for l in range(mix_pre_norm.shape[0]):
        x = _layer(x, mem, mix_pre_norm[l], w_in[l], mlstm_b_i[l], mlstm_b_f[l], mlstm_head_norm[l], rwkv_mu[l],
                   rwkv_w0[l], rwkv_w_up[l], rwkv_a0[l], rwkv_a_up[l], rwkv_g_up[l], rwkv_k_k[l], rwkv_k_a[l],
                   rwkv_r_k[l], rwkv_ln_g[l], rwkv_ln_b[l], w_branch_a[l], w_branch_b[l], w_mix_out[l],
                   mix_post_norm[l], xattn_pre_norm[l], mem_norm[l], xattn_wq[l], xattn_wkv[l], xattn_wo[l],
                   xattn_post_norm[l], ffn_pre_norm[l], ffn_w_up[l], ffn_conv_w[l], ffn_conv_b[l], ffn_w_down[l],
                   ffn_post_norm[l])
    return x
```

```python
import functools

import jax
import jax.numpy as jnp
from jax import lax
from jax.experimental import pallas as pl
from jax.experimental.pallas import tpu as pltpu

D_MODEL = 2048
MEM_LEN = 256
RMS_EPS = 1e-6

MLSTM_HEADS = 4
MLSTM_WIDTH = D_MODEL // 2
MLSTM_V_DIM = MLSTM_WIDTH // MLSTM_HEADS
MLSTM_QK_DIM = MLSTM_V_DIM // 2
MLSTM_QK_WIDTH = MLSTM_HEADS * MLSTM_QK_DIM
GATE_SOFTCAP = 15.0

RWKV_WIDTH = D_MODEL // 2
RWKV_HEAD = 64
RWKV_HEADS = RWKV_WIDTH // RWKV_HEAD
RWKV_DECAY_RANK = 64
RWKV_A_RANK = 64
RWKV_GATE_RANK = 160
RWKV_GN_EPS = 64e-5

MLSTM_TOTAL = 2 * MLSTM_QK_WIDTH + 2 * MLSTM_WIDTH + 2 * MLSTM_HEADS
RWKV_TOTAL = 3 * RWKV_WIDTH + RWKV_DECAY_RANK + RWKV_A_RANK + RWKV_GATE_RANK

XATTN_HEADS = 4
XATTN_HEAD_DIM = 128
XATTN_WIDTH = XATTN_HEADS * XATTN_HEAD_DIM

D_FF = 4 * D_MODEL
CONV_WIDTH = 3

COL_MQ = 0
COL_MK = COL_MQ + MLSTM_QK_WIDTH
COL_MV = COL_MK + MLSTM_QK_WIDTH
COL_MO = COL_MV + MLSTM_WIDTH
COL_RR = COL_MO + MLSTM_WIDTH
COL_RK = COL_RR + RWKV_WIDTH
COL_RV = COL_RK + RWKV_WIDTH
COL_GA = COL_RV + RWKV_WIDTH
COL_GB = COL_GA + D_MODEL
COL_RG = COL_GB + D_MODEL
RG_PAD = 256
COL_WA = COL_RG + RG_PAD
WA_W = RWKV_DECAY_RANK + RWKV_A_RANK
COL_IF = COL_WA + WA_W
IF_W = 128
PROJ_COLS = COL_IF + IF_W

CHUNK = 64
RWKV_GROUP_W = 256
RWKV_GROUPS = RWKV_WIDTH // RWKV_GROUP_W
VMEM_LIMIT = 56 * 1024 * 1024

F32 = jnp.float32
BF16 = jnp.bfloat16


def _bf(x):
    return x.astype(BF16)


def _mm(a, b):
    return jnp.dot(_bf(a), _bf(b), preferred_element_type=F32)


def _mm_nt(a, b):
    return lax.dot_general(_bf(a), _bf(b), (((1,), (1,)), ((), ())), preferred_element_type=F32)


def _mm_tn(a, b):
    return lax.dot_general(_bf(a), _bf(b), (((0,), (0,)), ((), ())), preferred_element_type=F32)


def _mm_exact_lhs(tri, x):
    hi = _bf(x)
    r1 = x - hi.astype(F32)
    mid = _bf(r1)
    lo = _bf(r1 - mid.astype(F32))
    t = _bf(tri)
    return (jnp.dot(t, hi, preferred_element_type=F32) + jnp.dot(t, mid, preferred_element_type=F32)
            + jnp.dot(t, lo, preferred_element_type=F32))


def _rms(x, g):
    return x * lax.rsqrt(jnp.mean(x * x, axis=-1, keepdims=True) + RMS_EPS) * g


def _softplus(x):
    return jnp.maximum(x, 0.0) + jnp.log1p(jnp.exp(-jnp.abs(x)))


def _sigmoid(x):
    return 1.0 / (1.0 + jnp.exp(-x))


def _params(sem):
    return pltpu.CompilerParams(dimension_semantics=sem, vmem_limit_bytes=VMEM_LIMIT)


def _norm_matmul_kernel(x_ref, g_ref, w_ref, o_ref, h_ref):
    @pl.when(pl.program_id(1) == 0)
    def _():
        h_ref[...] = _bf(_rms(x_ref[...], g_ref[...]))

    o_ref[...] = jnp.dot(h_ref[...], w_ref[...], preferred_element_type=F32).astype(o_ref.dtype)


def _norm_matmul(x, g, w, *, tm, tn, name):
    t, d = x.shape
    n = w.shape[1]
    return pl.pallas_call(
        _norm_matmul_kernel,
        out_shape=jax.ShapeDtypeStruct((t, n), F32),
        grid=(t // tm, n // tn),
        in_specs=[pl.BlockSpec((tm, d), lambda i, j: (i, 0)),
                  pl.BlockSpec((1, d), lambda i, j: (0, 0)),
                  pl.BlockSpec((d, tn), lambda i, j: (0, j))],
        out_specs=pl.BlockSpec((tm, tn), lambda i, j: (i, j)),
        scratch_shapes=[pltpu.VMEM((tm, d), BF16)],
        compiler_params=_params(("parallel", "arbitrary")),
        name=name,
    )(x, g, w)


def _mlstm_kernel(q_ref, k_ref, v_ref, o_ref, gate_ref, bias_ref, hn_ref, out_ref, ct_ref, n_ref, m_ref):
    L = CHUNK
    dk, dv = MLSTM_QK_DIM, MLSTM_V_DIM

    @pl.when(pl.program_id(1) == 0)
    def _():
        ct_ref[...] = jnp.zeros_like(ct_ref)
        n_ref[...] = jnp.zeros_like(n_ref)
        m_ref[...] = jnp.zeros_like(m_ref)

    pre = gate_ref[0] + bias_ref[...]
    capped = GATE_SOFTCAP * jnp.tanh(pre / GATE_SOFTCAP)
    logf = -_softplus(-capped)
    row = lax.broadcasted_iota(jnp.int32, (L, L), 0)
    col = lax.broadcasted_iota(jnp.int32, (L, L), 1)
    causal = row >= col
    bcum = _mm_exact_lhs(causal.astype(F32), logf)
    ig_t = capped.T
    bcum_t = bcum.T
    scale = MLSTM_QK_DIM ** -0.5

    H = range(MLSTM_HEADS)
    q = [q_ref[0, :, h * dk:(h + 1) * dk] for h in H]
    k = [k_ref[0, :, h * dk:(h + 1) * dk] * scale for h in H]
    v = [v_ref[0, :, h * dv:(h + 1) * dv] for h in H]
    b_c = [bcum[:, MLSTM_HEADS + h:MLSTM_HEADS + h + 1] for h in H]
    b_r = [bcum_t[MLSTM_HEADS + h:MLSTM_HEADS + h + 1, :] for h in H]
    i_c = [capped[:, h:h + 1] for h in H]
    i_r = [ig_t[h:h + 1, :] for h in H]
    m_prev = [m_ref[h][0:1, 0:1] for h in H]
    ct = [ct_ref[h] for h in H]
    nrow = [n_ref[h][0:1, :] for h in H]

    dmat = [jnp.where(causal, b_c[h] - b_r[h] + i_r[h], -jnp.inf) for h in H]
    inter = [b_c[h] + m_prev[h] for h in H]
    m_t = [jnp.maximum(inter[h], jnp.max(dmat[h], axis=-1, keepdims=True)) for h in H]
    dexp = [jnp.exp(dmat[h] - m_t[h]) for h in H]
    w_inter = [jnp.exp(inter[h] - m_t[h]) for h in H]
    qk = [_mm_nt(q[h], k[h]) for h in H]
    qc = [_mm(q[h], ct[h]) for h in H]
    s = [qk[h] * dexp[h] for h in H]
    sv = [_mm(s[h], v[h]) for h in H]
    num = [w_inter[h] * qc[h] + sv[h] for h in H]
    den = [w_inter[h] * jnp.sum(q[h] * nrow[h], axis=-1, keepdims=True) + jnp.sum(s[h], axis=-1, keepdims=True)
           for h in H]
    hh = [num[h] / jnp.maximum(jnp.abs(den[h]), jnp.exp(-m_t[h])) for h in H]

    b_last = [b_c[h][L - 1:L, :] for h in H]
    gs = [b_last[h] - b_c[h] + i_c[h] for h in H]
    m_new = [jnp.maximum(b_last[h] + m_prev[h], jnp.max(gs[h], axis=0, keepdims=True)) for h in H]
    carry_w = [jnp.exp(b_last[h] + m_prev[h] - m_new[h]) for h in H]
    ws = [jnp.exp(gs[h] - m_new[h]) for h in H]
    kv = [_mm_tn(k[h], ws[h] * v[h]) for h in H]
    for h in H:
        ct_ref[h] = carry_w[h] * ct[h] + kv[h]
        n_ref[h] = jnp.broadcast_to(carry_w[h] * nrow[h] + jnp.sum(ws[h] * k[h], axis=0, keepdims=True),
                                    n_ref.shape[1:])
        m_ref[h] = jnp.broadcast_to(m_new[h], m_ref.shape[1:])

    ms = [jnp.mean(hh[h] * hh[h], axis=-1, keepdims=True) for h in H]
    for h in H:
        hm = hh[h] * lax.rsqrt(ms[h] + RMS_EPS) * hn_ref[:, h * dv:(h + 1) * dv]
        og = o_ref[0, :, h * dv:(h + 1) * dv]
        out_ref[0, :, h * dv:(h + 1) * dv] = (_sigmoid(og) * hm).astype(out_ref.dtype)


def _mlstm(proj3, gate_bias, head_norm):
    b, s, _ = proj3.shape
    L = CHUNK
    qw, vw = MLSTM_QK_WIDTH, MLSTM_WIDTH
    return pl.pallas_call(
        _mlstm_kernel,
        out_shape=jax.ShapeDtypeStruct((b, s, MLSTM_WIDTH), BF16),
        grid=(b, s // L),
        in_specs=[pl.BlockSpec((1, L, qw), lambda i, c: (i, c, COL_MQ // qw)),
                  pl.BlockSpec((1, L, qw), lambda i, c: (i, c, COL_MK // qw)),
                  pl.BlockSpec((1, L, vw), lambda i, c: (i, c, COL_MV // vw)),
                  pl.BlockSpec((1, L, vw), lambda i, c: (i, c, COL_MO // vw)),
                  pl.BlockSpec((1, L, IF_W), lambda i, c: (i, c, COL_IF // IF_W)),
                  pl.BlockSpec((1, IF_W), lambda i, c: (0, 0)),
                  pl.BlockSpec((1, vw), lambda i, c: (0, 0))],
        out_specs=pl.BlockSpec((1, L, vw), lambda i, c: (i, c, 0)),
        scratch_shapes=[pltpu.VMEM((MLSTM_HEADS, MLSTM_QK_DIM, MLSTM_V_DIM), F32),
                        pltpu.VMEM((MLSTM_HEADS, 8, MLSTM_QK_DIM), F32),
                        pltpu.VMEM((MLSTM_HEADS, 8, 128), F32)],
        compiler_params=_params(("parallel", "arbitrary")),
        name="mlstm",
    )(proj3, proj3, proj3, proj3, proj3, gate_bias, head_norm)


PV_MU_R, PV_MU_K, PV_MU_V, PV_W0, PV_A0, PV_KK, PV_KA, PV_RK, PV_LNG, PV_LNB = range(10)
PV_ROWS = 16


def _rwkv_kernel(r_ref, k_ref, v_ref, g_ref, wa_ref, pv_ref, mug_ref, muwa_ref, wup_ref, aup_ref, gup_ref, seg_ref,
                 out_ref, s_ref, pr_ref, pk_ref, pvv_ref, pg_ref, pwa_ref):
    L = CHUNK
    N = RWKV_HEAD
    GW = RWKV_GROUP_W
    GH = GW // N

    @pl.when(pl.program_id(1) == 0)
    def _():
        s_ref[...] = jnp.zeros_like(s_ref)
        pr_ref[...] = jnp.zeros_like(pr_ref)
        pk_ref[...] = jnp.zeros_like(pk_ref)
        pvv_ref[...] = jnp.zeros_like(pvv_ref)
        pg_ref[...] = jnp.zeros_like(pg_ref)
        pwa_ref[...] = jnp.zeros_like(pwa_ref)

    def shift_lerp(x_ref, prev_ref, mu):
        x = x_ref[0]
        rid = lax.broadcasted_iota(jnp.int32, x.shape, 0)
        xs = jnp.where(rid == 0, prev_ref[0:1, :], pltpu.roll(x, 1, 0))
        prev_ref[0:1, :] = x[L - 1:L, :]
        return x + (xs - x) * mu

    pv = pv_ref[...]
    row = lambda i: pv[i:i + 1, :]
    r = shift_lerp(r_ref, pr_ref, row(PV_MU_R))
    kr = shift_lerp(k_ref, pk_ref, row(PV_MU_K))
    v = shift_lerp(v_ref, pvv_ref, row(PV_MU_V))
    gl = shift_lerp(g_ref, pg_ref, mug_ref[...])
    wa = shift_lerp(wa_ref, pwa_ref, muwa_ref[...])

    wl = jnp.tanh(wa[:, 0:RWKV_DECAY_RANK])
    al = wa[:, RWKV_DECAY_RANK:WA_W]
    w_log = -_softplus(-(row(PV_W0) + jnp.dot(_bf(wl), wup_ref[...], preferred_element_type=F32))) - 0.5
    lw = -jnp.exp(w_log)
    a = _sigmoid(row(PV_A0) + jnp.dot(_bf(al), aup_ref[...], preferred_element_type=F32))
    gg = jnp.dot(_bf(_sigmoid(gl)), gup_ref[...], preferred_element_type=F32)

    tri = lax.broadcasted_iota(jnp.int32, (L, L), 0) >= lax.broadcasted_iota(jnp.int32, (L, L), 1)
    cs = _mm_exact_lhs(tri.astype(F32), lw)
    g_incl = jnp.exp(cs)
    g_excl = jnp.exp(cs - lw)
    g_inv = jnp.exp(-cs)
    g_last = g_incl[L - 1:L, :]

    kk0 = kr * row(PV_KK)
    kr2 = kr * (1.0 + (a - 1.0) * row(PV_KA))
    rt_all = r * g_incl
    kt_all = kr2 * g_inv
    bonus_all = r * kr2 * row(PV_RK)

    seg = seg_ref[...]
    ti = lax.broadcasted_iota(jnp.int32, (L, GW), 0)
    lane = lax.broadcasted_iota(jnp.int32, (L, GW), 1)
    si = lane & (N - 1)
    lane_head = lane // N
    incl = ti >= si
    strict = ti > si
    eye = (ti == si).astype(F32)
    lvl_masks = []
    bsz = 1
    while bsz < L:
        same = (ti // (2 * bsz)) == (si // (2 * bsz))
        lvl_masks.append(jnp.where(same & ((ti & bsz) != 0) & ((si & bsz) == 0), 1.0, 0.0))
        bsz *= 2

    def bd(x):
        return jnp.concatenate([_bf(x)] * GH, axis=0) * seg

    def fold(z):
        acc = z[0:N]
        for hh in range(1, GH):
            acc = jnp.where(lane_head == hh, z[hh * N:(hh + 1) * N], acc)
        return acc

    def mmb(x, b):
        return jnp.dot(_bf(x), b, preferred_element_type=F32)

    def mmb_nt(x, b):
        return lax.dot_general(_bf(x), b, (((1,), (1,)), ((), ())), preferred_element_type=F32)

    def segsum(x):
        hi = _bf(x)
        lo = _bf(x - hi.astype(F32))
        return jnp.dot(hi, seg, preferred_element_type=F32) + jnp.dot(lo, seg, preferred_element_type=F32)

    cat = lambda u, w_: jnp.concatenate([u, w_], axis=0)
    groups = range(RWKV_GROUPS)
    grp = lambda x, g: x[:, g * GW:(g + 1) * GW]

    nrm2 = [segsum(grp(kk0, g) * grp(kk0, g)) for g in groups]
    kk = [grp(kk0, g) / jnp.maximum(jnp.sqrt(nrm2[g]), 1e-12) for g in groups]
    p = [grp(g_excl, g) * kk[g] for g in groups]
    qt = [kk[g] * grp(a, g) * grp(g_inv, g) for g in groups]
    kt = [grp(kt_all, g) for g in groups]
    rt = [grp(rt_all, g) for g in groups]
    vv = [grp(v, g) for g in groups]
    gl_ = [grp(g_last, g) for g in groups]
    qg = [qt[g] * gl_[g] for g in groups]
    kg = [kt[g] * gl_[g] for g in groups]

    lhs_pr = [cat(p[g], rt[g]) for g in groups]
    gq = [mmb_nt(lhs_pr[g], bd(qt[g])) for g in groups]
    gk = [mmb_nt(lhs_pr[g], bd(kt[g])) for g in groups]
    n_pq = [jnp.where(strict, gq[g][:L], 0.0) for g in groups]
    a_rq = [jnp.where(incl, gq[g][L:], 0.0) for g in groups]
    a_pk = [jnp.where(strict, gk[g][:L], 0.0) for g in groups]
    a_rk = [jnp.where(incl, gk[g][L:], 0.0) for g in groups]

    x = [eye - n_pq[g] * lvl_masks[0] for g in groups]
    for msk in lvl_masks[1:]:
        t1 = [mmb(x[g], bd(n_pq[g] * msk)) for g in groups]
        x = [x[g] - mmb(t1[g], bd(x[g])) for g in groups]

    av = [mmb(cat(a_pk[g], a_rk[g]), bd(vv[g])) for g in groups]
    w = [mmb(x[g], bd(p[g])) for g in groups]
    u0 = [mmb(x[g], bd(av[g][:L])) for g in groups]
    gmat = [rt[g] - mmb(a_rq[g], bd(w[g])) for g in groups]
    y0 = [av[g][L:] - mmb(a_rq[g], bd(u0[g])) for g in groups]
    mt = [eye * gl_[g] - fold(_mm_tn(qg[g], w[g])) for g in groups]
    bt = [fold(_mm_tn(cat(kg[g], -qg[g]), cat(vv[g], u0[g]))) for g in groups]

    ys = [mmb(cat(gmat[g], mt[g]), bd(s_ref[g])) for g in groups]
    y = [ys[g][:L] + y0[g] for g in groups]
    for g in groups:
        s_ref[g] = ys[g][L:] + bt[g]

    inv_n = 1.0 / N
    mean = [segsum(y[g]) * inv_n for g in groups]
    yc = [y[g] - mean[g] for g in groups]
    var = [segsum(yc[g] * yc[g]) * inv_n for g in groups]
    bonus = [segsum(grp(bonus_all, g)) for g in groups]
    for g in groups:
        yn = yc[g] * lax.rsqrt(var[g] + RWKV_GN_EPS) * grp(row(PV_LNG), g) + grp(row(PV_LNB), g)
        yn = yn + bonus[g] * vv[g]
        out_ref[0, :, g * GW:(g + 1) * GW] = (yn * grp(gg, g)).astype(out_ref.dtype)


def _rwkv(proj3, pvec, mu_g, mu_wa, w_up, a_up, g_up):
    b, s, _ = proj3.shape
    L = CHUNK
    rw = RWKV_WIDTH
    head_of = jnp.arange(RWKV_GROUP_W) // RWKV_HEAD
    seg = (head_of[:, None] == head_of[None, :]).astype(BF16)
    const = lambda shape: pl.BlockSpec(shape, lambda i, c: (0, 0))
    return pl.pallas_call(
        _rwkv_kernel,
        out_shape=jax.ShapeDtypeStruct((b, s, rw), BF16),
        grid=(b, s // L),
        in_specs=[pl.BlockSpec((1, L, rw), lambda i, c: (i, c, COL_RR // rw)),
                  pl.BlockSpec((1, L, rw), lambda i, c: (i, c, COL_RK // rw)),
                  pl.BlockSpec((1, L, rw), lambda i, c: (i, c, COL_RV // rw)),
                  pl.BlockSpec((1, L, RG_PAD), lambda i, c: (i, c, COL_RG // RG_PAD)),
                  pl.BlockSpec((1, L, WA_W), lambda i, c: (i, c, COL_WA // WA_W)),
                  const((PV_ROWS, rw)), const((1, RG_PAD)), const((1, WA_W)),
                  const((RWKV_DECAY_RANK, rw)), const((RWKV_A_RANK, rw)), const((RG_PAD, rw)),
                  const((RWKV_GROUP_W, RWKV_GROUP_W))],
        out_specs=pl.BlockSpec((1, L, rw), lambda i, c: (i, c, 0)),
        scratch_shapes=[pltpu.VMEM((RWKV_GROUPS, RWKV_HEAD, RWKV_GROUP_W), F32),
                        pltpu.VMEM((8, rw), F32), pltpu.VMEM((8, rw), F32), pltpu.VMEM((8, rw), F32),
                        pltpu.VMEM((8, RG_PAD), F32), pltpu.VMEM((8, WA_W), F32)],
        compiler_params=_params(("parallel", "arbitrary")),
        name="rwkv7",
    )(proj3, proj3, proj3, proj3, proj3, pvec, mu_g, mu_wa, w_up, a_up, g_up, seg)


def _merge_kernel(ha_ref, hb_ref, ga_ref, gb_ref, wa_ref, wb_ref, o_ref):
    ya = jnp.dot(ha_ref[...], wa_ref[...], preferred_element_type=F32)
    yb = jnp.dot(hb_ref[...], wb_ref[...], preferred_element_type=F32)
    o_ref[...] = (_sigmoid(ga_ref[...]) * ya + _sigmoid(gb_ref[...]) * yb).astype(o_ref.dtype)


def _merge(ha, hb, proj, wa, wb, *, tm, tn):
    t, kdim = ha.shape
    n = wa.shape[1]
    return pl.pallas_call(
        _merge_kernel,
        out_shape=jax.ShapeDtypeStruct((t, n), BF16),
        grid=(t // tm, n // tn),
        in_specs=[pl.BlockSpec((tm, kdim), lambda i, j: (i, 0)),
                  pl.BlockSpec((tm, kdim), lambda i, j: (i, 0)),
                  pl.BlockSpec((tm, tn), lambda i, j: (i, COL_GA // tn + j)),
                  pl.BlockSpec((tm, tn), lambda i, j: (i, COL_GB // tn + j)),
                  pl.BlockSpec((kdim, tn), lambda i, j: (0, j)),
                  pl.BlockSpec((kdim, tn), lambda i, j: (0, j))],
        out_specs=pl.BlockSpec((tm, tn), lambda i, j: (i, j)),
        compiler_params=_params(("parallel", "arbitrary")),
        name="merge",
    )(ha, hb, proj, proj, wa, wb)


def _matmul_norm_res_kernel(a_ref, w_ref, g_ref, x_ref, o_ref):
    y = jnp.dot(a_ref[...], w_ref[...], preferred_element_type=F32)
    o_ref[...] = x_ref[...] + _rms(y, g_ref[...])


def _matmul_norm_res(a, w, g, resid, *, tm, name):
    t, kdim = a.shape
    n = w.shape[1]
    return pl.pallas_call(
        _matmul_norm_res_kernel,
        out_shape=jax.ShapeDtypeStruct((t, n), F32),
        grid=(t // tm,),
        in_specs=[pl.BlockSpec((tm, kdim), lambda i: (i, 0)),
                  pl.BlockSpec((kdim, n), lambda i: (0, 0)),
                  pl.BlockSpec((1, n), lambda i: (0, 0)),
                  pl.BlockSpec((tm, n), lambda i: (i, 0))],
        out_specs=pl.BlockSpec((tm, n), lambda i: (i, 0)),
        compiler_params=_params(("parallel",)),
        name=name,
    )(a, w, g, resid)


def _xattn_kernel(x_ref, gpre_ref, wq_ref, kv_ref, wo_ref, gpost_ref, o_ref):
    x = x_ref[...]
    h = _bf(_rms(x, gpre_ref[...]))
    q = jnp.dot(h, wq_ref[...], preferred_element_type=F32)
    scale = XATTN_HEAD_DIM ** -0.5
    outs = []
    for hd in range(XATTN_HEADS):
        sl = slice(hd * XATTN_HEAD_DIM, (hd + 1) * XATTN_HEAD_DIM)
        k = kv_ref[0, :, sl]
        v = kv_ref[0, :, XATTN_WIDTH + hd * XATTN_HEAD_DIM:XATTN_WIDTH + (hd + 1) * XATTN_HEAD_DIM]
        sc = _mm_nt(q[:, sl], k) * scale
        sc = sc - jnp.max(sc, axis=-1, keepdims=True)
        e = jnp.exp(sc)
        p = e / jnp.sum(e, axis=-1, keepdims=True)
        outs.append(_mm(p, v))
    o = jnp.concatenate(outs, axis=-1)
    y = jnp.dot(_bf(o), wo_ref[...], preferred_element_type=F32)
    o_ref[...] = x + _rms(y, gpost_ref[...])


def _xattn(x, gpre, wq, kv3, wo, gpost, *, tm, seq):
    t, d = x.shape
    per_seq = seq // tm
    return pl.pallas_call(
        _xattn_kernel,
        out_shape=jax.ShapeDtypeStruct((t, d), F32),
        grid=(t // tm,),
        in_specs=[pl.BlockSpec((tm, d), lambda i: (i, 0)),
                  pl.BlockSpec((1, d), lambda i: (0, 0)),
                  pl.BlockSpec((d, XATTN_WIDTH), lambda i: (0, 0)),
                  pl.BlockSpec((1, MEM_LEN, 2 * XATTN_WIDTH), lambda i: (i // per_seq, 0, 0)),
                  pl.BlockSpec((XATTN_WIDTH, d), lambda i: (0, 0)),
                  pl.BlockSpec((1, d), lambda i: (0, 0))],
        out_specs=pl.BlockSpec((tm, d), lambda i: (i, 0)),
        compiler_params=_params(("parallel",)),
        name="xattn",
    )(x, gpre, wq, kv3, wo, gpost)


FFN_HALO = 8


def _gelu_tanh(x):
    return 0.5 * x * (1.0 + jnp.tanh(0.7978845608028654 * (x + 0.044715 * x * x * x)))


def _ffn_kernel(x_ref, halo_ref, gpre_ref, wg_ref, wu_ref, cwg_ref, cwu_ref, cbg_ref, cbu_ref, wd_ref, gpost_ref,
                o_ref, h_ref, *, tiles_per_seq):
    i = pl.program_id(0)
    j = pl.program_id(1)
    tm = x_ref.shape[0]

    @pl.when(j == 0)
    def _():
        keep = jnp.where(i % tiles_per_seq == 0, 0.0, 1.0)
        h_ref[0:FFN_HALO, :] = _bf(_rms(halo_ref[...], gpre_ref[...]) * keep)
        h_ref[FFN_HALO:, :] = _bf(_rms(x_ref[...], gpre_ref[...]))
        o_ref[...] = jnp.zeros_like(o_ref)

    def conv(w_ref, cw_ref, cb_ref):
        u = jnp.dot(h_ref[...], w_ref[...], preferred_element_type=F32)
        u1 = pltpu.roll(u, 1, 0)
        u2 = pltpu.roll(u, 2, 0)
        cw = cw_ref[...]
        full = cb_ref[...] + cw[2:3, :] * u + cw[1:2, :] * u1 + cw[0:1, :] * u2
        return full[FFN_HALO:, :]

    gate = conv(wg_ref, cwg_ref, cbg_ref)
    up = conv(wu_ref, cwu_ref, cbu_ref)
    act = _bf(_gelu_tanh(gate) * up)
    o_ref[...] += jnp.dot(act, wd_ref[...], preferred_element_type=F32)

    @pl.when(j == pl.num_programs(1) - 1)
    def _():
        o_ref[...] = x_ref[...] + _rms(o_ref[...], gpost_ref[...])


def _ffn(x, gpre, w_up, conv_w, conv_b, w_down, gpost, *, tm, tn, seq):
    t, d = x.shape
    nj = D_FF // tn
    hb = tm // FFN_HALO
    return pl.pallas_call(
        functools.partial(_ffn_kernel, tiles_per_seq=seq // tm),
        out_shape=jax.ShapeDtypeStruct((t, d), F32),
        grid=(t // tm, nj),
        in_specs=[pl.BlockSpec((tm, d), lambda i, j: (i, 0), pipeline_mode=pl.Buffered(1)),
                  pl.BlockSpec((FFN_HALO, d), lambda i, j: (jnp.maximum(i * hb - 1, 0), 0)),
                  pl.BlockSpec((1, d), lambda i, j: (0, 0)),
                  pl.BlockSpec((d, tn), lambda i, j: (0, j)),
                  pl.BlockSpec((d, tn), lambda i, j: (0, nj + j)),
                  pl.BlockSpec((CONV_WIDTH, tn), lambda i, j: (0, j)),
                  pl.BlockSpec((CONV_WIDTH, tn), lambda i, j: (0, nj + j)),
                  pl.BlockSpec((1, tn), lambda i, j: (0, j)),
                  pl.BlockSpec((1, tn), lambda i, j: (0, nj + j)),
                  pl.BlockSpec((tn, d), lambda i, j: (j, 0)),
                  pl.BlockSpec((1, d), lambda i, j: (0, 0))],
        out_specs=pl.BlockSpec((tm, d), lambda i, j: (i, 0), pipeline_mode=pl.Buffered(1)),
        scratch_shapes=[pltpu.VMEM((tm + FFN_HALO, d), BF16)],
        compiler_params=_params(("parallel", "arbitrary")),
        name="conv_glu_ffn",
    )(x, x, gpre, w_up, w_up, conv_w, conv_w, conv_b, conv_b, w_down, gpost)


def _pack_in_proj(w_in):
    d = w_in.shape[0]
    m0, r0, g0 = 0, MLSTM_TOTAL, MLSTM_TOTAL + RWKV_TOTAL
    mlstm_main = w_in[:, m0:m0 + 2 * MLSTM_QK_WIDTH + 2 * MLSTM_WIDTH]
    mlstm_if = w_in[:, m0 + 2 * MLSTM_QK_WIDTH + 2 * MLSTM_WIDTH:r0]
    rwkv_main = w_in[:, r0:r0 + 3 * RWKV_WIDTH]
    rwkv_wa = w_in[:, r0 + 3 * RWKV_WIDTH:r0 + 3 * RWKV_WIDTH + WA_W]
    rwkv_g = w_in[:, r0 + 3 * RWKV_WIDTH + WA_W:g0]
    gates = w_in[:, g0:]
    z = lambda n: jnp.zeros((d, n), w_in.dtype)
    packed = jnp.concatenate([mlstm_main, rwkv_main, gates, rwkv_g, z(RG_PAD - RWKV_GATE_RANK), rwkv_wa,
                              mlstm_if, z(IF_W - 2 * MLSTM_HEADS)], axis=1)
    return packed.astype(BF16)


def _layer(x, mem, mix_pre_norm, w_in, mlstm_b_i, mlstm_b_f, mlstm_head_norm, rwkv_mu, rwkv_w0, rwkv_w_up, rwkv_a0,
           rwkv_a_up, rwkv_g_up, rwkv_k_k, rwkv_k_a, rwkv_r_k, rwkv_ln_g, rwkv_ln_b, w_branch_a, w_branch_b,
           w_mix_out, mix_post_norm, xattn_pre_norm, mem_norm, xattn_wq, xattn_wkv, xattn_wo, xattn_post_norm,
           ffn_pre_norm, ffn_w_up, ffn_conv_w, ffn_conv_b, ffn_w_down, ffn_post_norm):
    b, s, d = x.shape
    t = b * s
    x2d = x.reshape(t, d)
    row = lambda p: p.reshape(1, -1).astype(F32)

    w_all = _pack_in_proj(w_in)
    tm_in = min(1024, t)
    proj = _norm_matmul(x2d, row(mix_pre_norm), w_all, tm=tm_in, tn=512, name="in_proj")
    proj3 = proj.reshape(b, s, PROJ_COLS)

    gate_bias = jnp.concatenate([mlstm_b_i, mlstm_b_f, jnp.zeros((IF_W - 2 * MLSTM_HEADS,), F32)]).reshape(1, IF_W)
    ha = _mlstm(proj3, gate_bias, row(mlstm_head_norm))

    mu_r, mu_k, mu_v = (rwkv_mu[i * RWKV_WIDTH:(i + 1) * RWKV_WIDTH] for i in range(3))
    mu_wa = rwkv_mu[3 * RWKV_WIDTH:3 * RWKV_WIDTH + WA_W]
    mu_g = jnp.pad(rwkv_mu[3 * RWKV_WIDTH + WA_W:], (0, RG_PAD - RWKV_GATE_RANK))
    pvec = jnp.stack([mu_r, mu_k, mu_v, rwkv_w0, rwkv_a0, rwkv_k_k, rwkv_k_a, rwkv_r_k.reshape(-1), rwkv_ln_g,
                      rwkv_ln_b] + [jnp.zeros((RWKV_WIDTH,), F32)] * (PV_ROWS - 10))
    g_up = jnp.pad(rwkv_g_up, ((0, RG_PAD - RWKV_GATE_RANK), (0, 0)))
    hb = _rwkv(proj3, pvec, row(mu_g), row(mu_wa), _bf(rwkv_w_up), _bf(rwkv_a_up), _bf(g_up))

    tm = min(512, s)
    merged = _merge(ha.reshape(t, -1), hb.reshape(t, -1), proj, _bf(w_branch_a), _bf(w_branch_b), tm=tm, tn=512)
    x1 = _matmul_norm_res(merged, _bf(w_mix_out), row(mix_post_norm), x2d, tm=tm, name="mix_out")

    mem2d = mem.reshape(b * MEM_LEN, d)
    kv = _norm_matmul(mem2d, row(mem_norm), _bf(xattn_wkv), tm=MEM_LEN, tn=512, name="mem_kv")
    x2 = _xattn(x1, row(xattn_pre_norm), _bf(xattn_wq), kv.reshape(b, MEM_LEN, 2 * XATTN_WIDTH), _bf(xattn_wo),
                row(xattn_post_norm), tm=tm, seq=s)

    x3 = _ffn(x2, row(ffn_pre_norm), _bf(ffn_w_up), ffn_conv_w, row(ffn_conv_b), _bf(ffn_w_down),
              row(ffn_post_norm), tm=min(1024, s), tn=512, seq=s)
    return x3.reshape(b, s, d)


def kernel(x, mem, mix_pre_norm, w_in, mlstm_b_i, mlstm_b_f, mlstm_head_norm, rwkv_mu, rwkv_w0, rwkv_w_up, rwkv_a0, rwkv_a_up, rwkv_g_up, rwkv_k_k, rwkv_k_a, rwkv_r_k, rwkv_ln_g, rwkv_ln_b, w_branch_a, w_branch_b, w_mix_out, mix_post_norm, xattn_pre_norm, mem_norm, xattn_wq, xattn_wkv, xattn_wo, xattn_post_norm, ffn_pre_norm, ffn_w_up, ffn_conv_w, ffn_conv_b, ffn_w_down, ffn_post_norm):
    for l in range(mix_pre_norm.shape[0]):
        x = _layer(x, mem, mix_pre_norm[l], w_in[l], mlstm_b_i[l], mlstm_b_f[l], mlstm_head_norm[l], rwkv_mu[l],
                   rwkv_w0[l], rwkv_w_up[l], rwkv_a0[l], rwkv_a_up[l], rwkv_g_up[l], rwkv_k_k[l], rwkv_k_a[l],
                   rwkv_r_k[l], rwkv_ln_g[l], rwkv_ln_b[l], w_branch_a[l], w_branch_b[l], w_mix_out[l],
                   mix_post_norm[l], xattn_pre_norm[l], mem_norm[l], xattn_wq[l], xattn_wkv[l], xattn_wo[l],
                   xattn_post_norm[l], ffn_pre_norm[l], ffn_w_up[l], ffn_conv_w[l], ffn_conv_b[l], ffn_w_down[l],
                   ffn_post_norm[l])
    return x
```

```python
import functools

import jax
import jax.numpy as jnp
from jax import lax
from jax.experimental import pallas as pl
from jax.experimental.pallas import tpu as pltpu

D_MODEL = 2048
MEM_LEN = 256
RMS_EPS = 1e-6

MLSTM_HEADS = 4
MLSTM_WIDTH = D_MODEL // 2
MLSTM_V_DIM = MLSTM_WIDTH // MLSTM_HEADS
MLSTM_QK_DIM = MLSTM_V_DIM // 2
MLSTM_QK_WIDTH = MLSTM_HEADS * MLSTM_QK_DIM
GATE_SOFTCAP = 15.0

RWKV_WIDTH = D_MODEL // 2
RWKV_HEAD = 64
RWKV_HEADS = RWKV_WIDTH // RWKV_HEAD
RWKV_DECAY_RANK = 64
RWKV_A_RANK = 64
RWKV_GATE_RANK = 160
RWKV_GN_EPS = 64e-5

MLSTM_MAIN = 2 * MLSTM_QK_WIDTH + 2 * MLSTM_WIDTH
MLSTM_TOTAL = MLSTM_MAIN + 2 * MLSTM_HEADS
RWKV_MAIN = 3 * RWKV_WIDTH
RWKV_TOTAL = RWKV_MAIN + RWKV_DECAY_RANK + RWKV_A_RANK + RWKV_GATE_RANK

XATTN_HEADS = 4
XATTN_HEAD_DIM = 128
XATTN_WIDTH = XATTN_HEADS * XATTN_HEAD_DIM

D_FF = 4 * D_MODEL
CONV_WIDTH = 3

RG_PAD = 256
WA_W = RWKV_DECAY_RANK + RWKV_A_RANK
IF_W = 128
COL_RG = 0
COL_WA = COL_RG + RG_PAD
COL_IF = COL_WA + WA_W
SMALL_COLS = COL_IF + IF_W

MLSTM_CHUNK = 256
RWKV_CHUNK = 64
RWKV_CHUNKS_PER_STEP = 2
RWKV_GROUP_W = 256
RWKV_GROUPS = RWKV_WIDTH // RWKV_GROUP_W
VMEM_LIMIT = 56 * 1024 * 1024

TM_IN_PROJ = 1024
TM_PLAIN = 2048
TM_ROW = 512
TM_FFN = 1024
TN = 512

F32 = jnp.float32
BF16 = jnp.bfloat16


def _bf(x):
    return x.astype(BF16)


def _mm(a, b):
    return jnp.dot(_bf(a), _bf(b), preferred_element_type=F32)


def _mm_nt(a, b):
    return lax.dot_general(_bf(a), _bf(b), (((1,), (1,)), ((), ())), preferred_element_type=F32)


def _mm_tn(a, b):
    return lax.dot_general(_bf(a), _bf(b), (((0,), (0,)), ((), ())), preferred_element_type=F32)


def _mm_exact_lhs(tri, x):
    hi = _bf(x)
    r1 = x - hi.astype(F32)
    mid = _bf(r1)
    lo = _bf(r1 - mid.astype(F32))
    t = _bf(tri)
    return (jnp.dot(t, hi, preferred_element_type=F32) + jnp.dot(t, mid, preferred_element_type=F32)
            + jnp.dot(t, lo, preferred_element_type=F32))


def _rms(x, g):
    return x * lax.rsqrt(jnp.mean(x * x, axis=-1, keepdims=True) + RMS_EPS) * g


def _softplus(x):
    return jnp.maximum(x, 0.0) + jnp.log(1.0 + jnp.exp(-jnp.abs(x)))


def _sigmoid(x):
    return 1.0 / (1.0 + jnp.exp(-x))


def _params(sem):
    return pltpu.CompilerParams(dimension_semantics=sem, vmem_limit_bytes=VMEM_LIMIT)


def _norm_matmul_kernel(x_ref, g_ref, w_ref, o_ref, h_ref):
    @pl.when(pl.program_id(1) == 0)
    def _():
        h_ref[...] = _bf(_rms(x_ref[...], g_ref[...]))

    o_ref[...] = jnp.dot(h_ref[...], w_ref[...], preferred_element_type=F32).astype(o_ref.dtype)


def _norm_matmul(x, g, w, *, tm, tn, out_dtype, name):
    t, d = x.shape
    n = w.shape[1]
    return pl.pallas_call(
        _norm_matmul_kernel,
        out_shape=(jax.ShapeDtypeStruct((t, n), out_dtype), jax.ShapeDtypeStruct((t, d), BF16)),
        grid=(t // tm, n // tn),
        in_specs=[pl.BlockSpec((tm, d), lambda i, j: (i, 0)),
                  pl.BlockSpec((1, d), lambda i, j: (0, 0)),
                  pl.BlockSpec((d, tn), lambda i, j: (0, j))],
        out_specs=(pl.BlockSpec((tm, tn), lambda i, j: (i, j)),
                   pl.BlockSpec((tm, d), lambda i, j: (i, 0))),
        compiler_params=_params(("parallel", "arbitrary")),
        name=name,
    )(x, g, w)


def _matmul_kernel(a_ref, w_ref, o_ref):
    o_ref[...] = jnp.dot(a_ref[...], w_ref[...], preferred_element_type=F32).astype(o_ref.dtype)


def _matmul(a, w, *, tm, tn, out_dtype, name):
    t, d = a.shape
    n = w.shape[1]
    return pl.pallas_call(
        _matmul_kernel,
        out_shape=jax.ShapeDtypeStruct((t, n), out_dtype),
        grid=(t // tm, n // tn),
        in_specs=[pl.BlockSpec((tm, d), lambda i, j: (i, 0)),
                  pl.BlockSpec((d, tn), lambda i, j: (0, j))],
        out_specs=pl.BlockSpec((tm, tn), lambda i, j: (i, j)),
        compiler_params=_params(("parallel", "arbitrary")),
        name=name,
    )(a, w)


def _mlstm_kernel(q_ref, k_ref, v_ref, o_ref, gate_ref, bias_ref, hn_ref, out_ref, ct_ref, n_ref, m_ref):
    L = MLSTM_CHUNK
    dk, dv = MLSTM_QK_DIM, MLSTM_V_DIM

    @pl.when(pl.program_id(1) == 0)
    def _():
        ct_ref[...] = jnp.zeros_like(ct_ref)
        n_ref[...] = jnp.zeros_like(n_ref)
        m_ref[...] = jnp.zeros_like(m_ref)

    pre = gate_ref[0] + bias_ref[...]
    capped = GATE_SOFTCAP * jnp.tanh(pre / GATE_SOFTCAP)
    logf = -_softplus(-capped)
    row = lax.broadcasted_iota(jnp.int32, (L, L), 0)
    col = lax.broadcasted_iota(jnp.int32, (L, L), 1)
    causal = row >= col
    bcum = _mm_exact_lhs(causal.astype(F32), logf)
    ig_t = capped.T
    bcum_t = bcum.T
    scale = MLSTM_QK_DIM ** -0.5

    H = range(MLSTM_HEADS)
    q = [q_ref[0, :, h * dk:(h + 1) * dk] for h in H]
    k = [k_ref[0, :, h * dk:(h + 1) * dk] for h in H]
    v = [v_ref[0, :, h * dv:(h + 1) * dv] for h in H]
    b_c = [bcum[:, MLSTM_HEADS + h:MLSTM_HEADS + h + 1] for h in H]
    b_r = [bcum_t[MLSTM_HEADS + h:MLSTM_HEADS + h + 1, :] for h in H]
    i_c = [capped[:, h:h + 1] for h in H]
    i_r = [ig_t[h:h + 1, :] for h in H]
    m_prev = [m_ref[h][0:1, 0:1] for h in H]
    ct = [ct_ref[h] for h in H]
    nrow = [n_ref[h][0:1, :] for h in H]

    dmat = [jnp.where(causal, b_c[h] - b_r[h] + i_r[h], -jnp.inf) for h in H]
    inter = [b_c[h] + m_prev[h] for h in H]
    m_t = [jnp.maximum(inter[h], jnp.max(dmat[h], axis=-1, keepdims=True)) for h in H]
    dexp = [jnp.exp(dmat[h] - m_t[h]) * scale for h in H]
    w_inter = [jnp.exp(inter[h] - m_t[h]) for h in H]
    qk = [_mm_nt(q[h], k[h]) for h in H]
    qc = [_mm(q[h], ct[h]) for h in H]
    s = [qk[h] * dexp[h] for h in H]
    sv = [_mm(s[h], v[h]) for h in H]
    num = [w_inter[h] * qc[h] + sv[h] for h in H]
    den = [w_inter[h] * jnp.sum(q[h].astype(F32) * nrow[h], axis=-1, keepdims=True)
           + jnp.sum(s[h], axis=-1, keepdims=True) for h in H]
    hh = [num[h] / jnp.maximum(jnp.abs(den[h]), jnp.exp(-m_t[h])) for h in H]

    b_last = [b_c[h][L - 1:L, :] for h in H]
    gs = [b_last[h] - b_c[h] + i_c[h] for h in H]
    m_new = [jnp.maximum(b_last[h] + m_prev[h], jnp.max(gs[h], axis=0, keepdims=True)) for h in H]
    carry_w = [jnp.exp(b_last[h] + m_prev[h] - m_new[h]) for h in H]
    ws = [jnp.exp(gs[h] - m_new[h]) * scale for h in H]
    kv = [_mm_tn(k[h], ws[h] * v[h].astype(F32)) for h in H]
    for h in H:
        ct_ref[h] = carry_w[h] * ct[h] + kv[h]
        n_ref[h] = jnp.broadcast_to(
            carry_w[h] * nrow[h] + jnp.sum(ws[h] * k[h].astype(F32), axis=0, keepdims=True), n_ref.shape[1:])
        m_ref[h] = jnp.broadcast_to(m_new[h], m_ref.shape[1:])

    ms = [jnp.mean(hh[h] * hh[h], axis=-1, keepdims=True) for h in H]
    for h in H:
        hm = hh[h] * lax.rsqrt(ms[h] + RMS_EPS) * hn_ref[:, h * dv:(h + 1) * dv]
        og = o_ref[0, :, h * dv:(h + 1) * dv].astype(F32)
        out_ref[0, :, h * dv:(h + 1) * dv] = (_sigmoid(og) * hm).astype(out_ref.dtype)


def _mlstm(main3, small3, gate_bias, head_norm):
    b, s, _ = main3.shape
    L = min(MLSTM_CHUNK, s)
    assert L == MLSTM_CHUNK and s % L == 0
    qw, vw = MLSTM_QK_WIDTH, MLSTM_WIDTH
    return pl.pallas_call(
        _mlstm_kernel,
        out_shape=jax.ShapeDtypeStruct((b, s, MLSTM_WIDTH), BF16),
        grid=(b, s // L),
        in_specs=[pl.BlockSpec((1, L, qw), lambda i, c: (i, c, 0)),
                  pl.BlockSpec((1, L, qw), lambda i, c: (i, c, 1)),
                  pl.BlockSpec((1, L, vw), lambda i, c: (i, c, 1)),
                  pl.BlockSpec((1, L, vw), lambda i, c: (i, c, 2)),
                  pl.BlockSpec((1, L, IF_W), lambda i, c: (i, c, COL_IF // IF_W)),
                  pl.BlockSpec((1, IF_W), lambda i, c: (0, 0)),
                  pl.BlockSpec((1, vw), lambda i, c: (0, 0))],
        out_specs=pl.BlockSpec((1, L, vw), lambda i, c: (i, c, 0)),
        scratch_shapes=[pltpu.VMEM((MLSTM_HEADS, MLSTM_QK_DIM, MLSTM_V_DIM), F32),
                        pltpu.VMEM((MLSTM_HEADS, 8, MLSTM_QK_DIM), F32),
                        pltpu.VMEM((MLSTM_HEADS, 8, 128), F32)],
        compiler_params=_params(("parallel", "arbitrary")),
        name="mlstm",
    )(main3, main3, main3, main3, small3, gate_bias, head_norm)


PV_MU_R, PV_MU_K, PV_MU_V, PV_W0, PV_A0, PV_KK, PV_KA, PV_RK, PV_LNG, PV_LNB = range(10)
PV_ROWS = 16


def _rwkv_kernel(r_ref, k_ref, v_ref, g_ref, wa_ref, pv_ref, mug_ref, muwa_ref, wup_ref, aup_ref, gup_ref, seg_ref,
                 out_ref, s_ref, pr_ref, pk_ref, pvv_ref, pg_ref, pwa_ref):
    L = RWKV_CHUNK
    CH = RWKV_CHUNKS_PER_STEP
    R = CH * L
    N = RWKV_HEAD
    GW = RWKV_GROUP_W
    GH = GW // N

    @pl.when(pl.program_id(1) == 0)
    def _():
        s_ref[...] = jnp.zeros_like(s_ref)
        pr_ref[...] = jnp.zeros_like(pr_ref)
        pk_ref[...] = jnp.zeros_like(pk_ref)
        pvv_ref[...] = jnp.zeros_like(pvv_ref)
        pg_ref[...] = jnp.zeros_like(pg_ref)
        pwa_ref[...] = jnp.zeros_like(pwa_ref)

    def shift_lerp(x_ref, prev_ref, mu):
        x = x_ref[0].astype(F32)
        rid = lax.broadcasted_iota(jnp.int32, x.shape, 0)
        xs = jnp.where(rid == 0, prev_ref[0:1, :], pltpu.roll(x, 1, 0))
        prev_ref[0:1, :] = x[R - 1:R, :]
        return x + (xs - x) * mu

    pv = pv_ref[...]
    row = lambda i: pv[i:i + 1, :]
    r = shift_lerp(r_ref, pr_ref, row(PV_MU_R))
    kr = shift_lerp(k_ref, pk_ref, row(PV_MU_K))
    v = shift_lerp(v_ref, pvv_ref, row(PV_MU_V))
    gl = shift_lerp(g_ref, pg_ref, mug_ref[...])
    wa = shift_lerp(wa_ref, pwa_ref, muwa_ref[...])

    wl = jnp.tanh(wa[:, 0:RWKV_DECAY_RANK])
    al = wa[:, RWKV_DECAY_RANK:WA_W]
    w_log = -_softplus(-(row(PV_W0) + jnp.dot(_bf(wl), wup_ref[...], preferred_element_type=F32))) - 0.5
    lw = -jnp.exp(w_log)
    a = _sigmoid(row(PV_A0) + jnp.dot(_bf(al), aup_ref[...], preferred_element_type=F32))
    gg = jnp.dot(_bf(_sigmoid(gl)), gup_ref[...], preferred_element_type=F32)

    tr = lax.broadcasted_iota(jnp.int32, (R, R), 0)
    tc = lax.broadcasted_iota(jnp.int32, (R, R), 1)
    tri = (tr >= tc) & ((tr // L) == (tc // L))
    cs = _mm_exact_lhs(tri.astype(F32), lw)
    g_incl = jnp.exp(cs)
    g_excl = jnp.exp(cs - lw)
    g_inv = jnp.exp(-cs)

    kk0 = kr * row(PV_KK)
    kr2 = kr * (1.0 + (a - 1.0) * row(PV_KA))
    rt_all = r * g_incl
    kt_all = kr2 * g_inv
    bonus_all = r * kr2 * row(PV_RK)

    seg = seg_ref[...]
    ti = lax.broadcasted_iota(jnp.int32, (L, GW), 0)
    lane = lax.broadcasted_iota(jnp.int32, (L, GW), 1)
    si = lane & (N - 1)
    lane_head = lane // N
    incl = ti >= si
    strict = ti > si
    eye = (ti == si).astype(F32)
    lvl_masks = []
    bsz = 1
    while bsz < L:
        same = (ti // (2 * bsz)) == (si // (2 * bsz))
        lvl_masks.append(jnp.where(same & ((ti & bsz) != 0) & ((si & bsz) == 0), 1.0, 0.0))
        bsz *= 2

    def bd(x):
        return jnp.concatenate([_bf(x)] * GH, axis=0) * seg

    def fold(z):
        acc = z[0:N]
        for hh in range(1, GH):
            acc = jnp.where(lane_head == hh, z[hh * N:(hh + 1) * N], acc)
        return acc

    def mmb(x, b):
        return jnp.dot(_bf(x), b, preferred_element_type=F32)

    def mmb_nt(x, b):
        return lax.dot_general(_bf(x), b, (((1,), (1,)), ((), ())), preferred_element_type=F32)

    cat = lambda u, w_: jnp.concatenate([u, w_], axis=0)
    groups = range(RWKV_GROUPS)
    E = [(c, g) for c in range(CH) for g in groups]
    rows = lambda c: slice(c * L, (c + 1) * L)
    lanes = lambda g: slice(g * GW, (g + 1) * GW)
    blk = lambda x: [x[rows(c), lanes(g)] for c, g in E]
    n_e = range(len(E))

    kk0_e, bonus_src, gex_e, a_e, ginv_e = blk(kk0), blk(bonus_all), blk(g_excl), blk(a), blk(g_inv)
    kt, rt, vv = blk(kt_all), blk(rt_all), blk(v)
    gl_ = [g_incl[c * L + L - 1:c * L + L, lanes(g)] for c, g in E]
    sums0 = [mmb(cat(kk0_e[e] * kk0_e[e], bonus_src[e]), seg) for e in n_e]
    kk = [kk0_e[e] * lax.rsqrt(jnp.maximum(sums0[e][:L], 1e-24)) for e in n_e]
    bonus = [sums0[e][L:] for e in n_e]
    p = [gex_e[e] * kk[e] for e in n_e]
    qt = [kk[e] * a_e[e] * ginv_e[e] for e in n_e]
    qg = [qt[e] * gl_[e] for e in n_e]
    kg = [kt[e] * gl_[e] for e in n_e]

    lhs_pr = [cat(p[e], rt[e]) for e in n_e]
    gq = [mmb_nt(lhs_pr[e], bd(qt[e])) for e in n_e]
    gk = [mmb_nt(lhs_pr[e], bd(kt[e])) for e in n_e]
    n_pq = [jnp.where(strict, gq[e][:L], 0.0) for e in n_e]
    a_rq = [jnp.where(incl, gq[e][L:], 0.0) for e in n_e]
    a_pk = [jnp.where(strict, gk[e][:L], 0.0) for e in n_e]
    a_rk = [jnp.where(incl, gk[e][L:], 0.0) for e in n_e]

    x = [eye - n_pq[e] * lvl_masks[0] for e in n_e]
    bd_n = [bd(n_pq[e]) for e in n_e]
    for msk in lvl_masks[1:]:
        t1 = [mmb(x[e], bd_n[e]) for e in n_e]
        x = [x[e] - msk * mmb(t1[e], bd(x[e])) for e in n_e]

    av = [mmb(cat(a_pk[e], a_rk[e]), bd(vv[e])) for e in n_e]
    w = [mmb(x[e], bd(p[e])) for e in n_e]
    u0 = [mmb(x[e], bd(av[e][:L])) for e in n_e]
    gmat = [rt[e] - mmb(a_rq[e], bd(w[e])) for e in n_e]
    y0 = [av[e][L:] - mmb(a_rq[e], bd(u0[e])) for e in n_e]
    mt = [eye * gl_[e] - fold(_mm_tn(qg[e], w[e])) for e in n_e]
    bt = [fold(_mm_tn(cat(kg[e], -qg[e]), cat(vv[e], u0[e]))) for e in n_e]

    state = [s_ref[g] for g in groups]
    y = [None] * len(E)
    for c in range(CH):
        es = [c * RWKV_GROUPS + g for g in groups]
        ys = [mmb(cat(gmat[e], mt[e]), bd(state[g])) for g, e in zip(groups, es)]
        for g, e in zip(groups, es):
            y[e] = ys[g][:L] + y0[e]
            state[g] = ys[g][L:] + bt[e]
    for g in groups:
        s_ref[g] = state[g]

    inv_n = 1.0 / N
    sums1 = [mmb(cat(y[e], y[e] * y[e]), seg) for e in n_e]
    for e, (c, g) in enumerate(E):
        mean = sums1[e][:L] * inv_n
        var = sums1[e][L:] * inv_n - mean * mean
        yn = (y[e] - mean) * lax.rsqrt(var + RWKV_GN_EPS) * row(PV_LNG)[:, lanes(g)] + row(PV_LNB)[:, lanes(g)]
        yn = yn + bonus[e] * vv[e]
        out_ref[0, rows(c), lanes(g)] = (yn * gg[rows(c), lanes(g)]).astype(out_ref.dtype)


def _rwkv(main3, small3, pvec, mu_g, mu_wa, w_up, a_up, g_up):
    b, s, _ = main3.shape
    L = RWKV_CHUNK * RWKV_CHUNKS_PER_STEP
    rw = RWKV_WIDTH
    head_of = jnp.arange(RWKV_GROUP_W) // RWKV_HEAD
    seg = (head_of[:, None] == head_of[None, :]).astype(BF16)
    const = lambda shape: pl.BlockSpec(shape, lambda i, c: (0, 0))
    return pl.pallas_call(
        _rwkv_kernel,
        out_shape=jax.ShapeDtypeStruct((b, s, rw), BF16),
        grid=(b, s // L),
        in_specs=[pl.BlockSpec((1, L, rw), lambda i, c: (i, c, 0)),
                  pl.BlockSpec((1, L, rw), lambda i, c: (i, c, 1)),
                  pl.BlockSpec((1, L, rw), lambda i, c: (i, c, 2)),
                  pl.BlockSpec((1, L, RG_PAD), lambda i, c: (i, c, COL_RG // RG_PAD)),
                  pl.BlockSpec((1, L, WA_W), lambda i, c: (i, c, COL_WA // WA_W)),
                  const((PV_ROWS, rw)), const((1, RG_PAD)), const((1, WA_W)),
                  const((RWKV_DECAY_RANK, rw)), const((RWKV_A_RANK, rw)), const((RG_PAD, rw)),
                  const((RWKV_GROUP_W, RWKV_GROUP_W))],
        out_specs=pl.BlockSpec((1, L, rw), lambda i, c: (i, c, 0)),
        scratch_shapes=[pltpu.VMEM((RWKV_GROUPS, RWKV_HEAD, RWKV_GROUP_W), F32),
                        pltpu.VMEM((8, rw), F32), pltpu.VMEM((8, rw), F32), pltpu.VMEM((8, rw), F32),
                        pltpu.VMEM((8, RG_PAD), F32), pltpu.VMEM((8, WA_W), F32)],
        compiler_params=_params(("parallel", "arbitrary")),
        name="rwkv7",
    )(main3, main3, main3, small3, small3, pvec, mu_g, mu_wa, w_up, a_up, g_up, seg)


def _mix_out_kernel(ha_ref, hb_ref, ga_ref, gb_ref, wa_ref, wb_ref, wo_ref, g_ref, x_ref, o_ref):
    ya = jnp.dot(ha_ref[...], wa_ref[...], preferred_element_type=F32)
    yb = jnp.dot(hb_ref[...], wb_ref[...], preferred_element_type=F32)
    merged = _bf(_sigmoid(ga_ref[...].astype(F32)) * ya + _sigmoid(gb_ref[...].astype(F32)) * yb)
    y = jnp.dot(merged, wo_ref[...], preferred_element_type=F32)
    o_ref[...] = x_ref[...] + _rms(y, g_ref[...])


def _mix_out(ha, hb, gates, wa, wb, wo, g, x, *, tm):
    t, kdim = ha.shape
    d = wo.shape[1]
    resident = lambda shape: pl.BlockSpec(shape, lambda i: (0, 0), pipeline_mode=pl.Buffered(1))
    return pl.pallas_call(
        _mix_out_kernel,
        out_shape=jax.ShapeDtypeStruct((t, d), F32),
        grid=(t // tm,),
        in_specs=[pl.BlockSpec((tm, kdim), lambda i: (i, 0)),
                  pl.BlockSpec((tm, kdim), lambda i: (i, 0)),
                  pl.BlockSpec((tm, d), lambda i: (i, 0)),
                  pl.BlockSpec((tm, d), lambda i: (i, 1)),
                  resident((kdim, d)), resident((kdim, d)), resident((d, d)),
                  pl.BlockSpec((1, d), lambda i: (0, 0)),
                  pl.BlockSpec((tm, d), lambda i: (i, 0))],
        out_specs=pl.BlockSpec((tm, d), lambda i: (i, 0)),
        compiler_params=_params(("parallel",)),
        name="mix_out",
    )(ha, hb, gates, gates, wa, wb, wo, g, x)


def _xattn_kernel(x_ref, gpre_ref, wq_ref, kv_ref, wo_ref, gpost_ref, o_ref):
    x = x_ref[...]
    h = _bf(_rms(x, gpre_ref[...]))
    q = jnp.dot(h, wq_ref[...], preferred_element_type=F32)
    scale = XATTN_HEAD_DIM ** -0.5
    outs = []
    for hd in range(XATTN_HEADS):
        sl = slice(hd * XATTN_HEAD_DIM, (hd + 1) * XATTN_HEAD_DIM)
        k = kv_ref[0, :, sl]
        v = kv_ref[0, :, XATTN_WIDTH + hd * XATTN_HEAD_DIM:XATTN_WIDTH + (hd + 1) * XATTN_HEAD_DIM]
        sc = _mm_nt(q[:, sl], k) * scale
        sc = sc - jnp.max(sc, axis=-1, keepdims=True)
        e = jnp.exp(sc)
        p = e / jnp.sum(e, axis=-1, keepdims=True)
        outs.append(_mm(p, v))
    o = jnp.concatenate(outs, axis=-1)
    y = jnp.dot(_bf(o), wo_ref[...], preferred_element_type=F32)
    o_ref[...] = x + _rms(y, gpost_ref[...])


def _xattn(x, gpre, wq, kv3, wo, gpost, *, tm, seq):
    t, d = x.shape
    per_seq = seq // tm
    return pl.pallas_call(
        _xattn_kernel,
        out_shape=jax.ShapeDtypeStruct((t, d), F32),
        grid=(t // tm,),
        in_specs=[pl.BlockSpec((tm, d), lambda i: (i, 0)),
                  pl.BlockSpec((1, d), lambda i: (0, 0)),
                  pl.BlockSpec((d, XATTN_WIDTH), lambda i: (0, 0)),
                  pl.BlockSpec((1, MEM_LEN, 2 * XATTN_WIDTH), lambda i: (i // per_seq, 0, 0)),
                  pl.BlockSpec((XATTN_WIDTH, d), lambda i: (0, 0)),
                  pl.BlockSpec((1, d), lambda i: (0, 0))],
        out_specs=pl.BlockSpec((tm, d), lambda i: (i, 0)),
        compiler_params=_params(("parallel",)),
        name="xattn",
    )(x, gpre, wq, kv3, wo, gpost)


FFN_HALO = 8


def _gelu_tanh(x):
    return 0.5 * x * (1.0 + jnp.tanh(0.7978845608028654 * (x + 0.044715 * x * x * x)))


def _ffn_kernel(x_ref, halo_ref, gpre_ref, wg_ref, wu_ref, cwg_ref, cwu_ref, cbg_ref, cbu_ref, wd_ref, gpost_ref,
                o_ref, h_ref, *, tiles_per_seq):
    i = pl.program_id(0)
    j = pl.program_id(1)

    @pl.when(j == 0)
    def _():
        keep = jnp.where(i % tiles_per_seq == 0, 0.0, 1.0)
        h_ref[0:FFN_HALO, :] = _bf(_rms(halo_ref[...], gpre_ref[...]) * keep)
        h_ref[FFN_HALO:, :] = _bf(_rms(x_ref[...], gpre_ref[...]))
        o_ref[...] = jnp.zeros_like(o_ref)

    def conv(w_ref, cw_ref, cb_ref):
        u = jnp.dot(h_ref[...], w_ref[...], preferred_element_type=F32)
        u1 = pltpu.roll(u, 1, 0)
        u2 = pltpu.roll(u, 2, 0)
        cw = cw_ref[...]
        full = cb_ref[...] + cw[2:3, :] * u + cw[1:2, :] * u1 + cw[0:1, :] * u2
        return full[FFN_HALO:, :]

    gate = conv(wg_ref, cwg_ref, cbg_ref)
    up = conv(wu_ref, cwu_ref, cbu_ref)
    act = _bf(_gelu_tanh(gate) * up)
    o_ref[...] += jnp.dot(act, wd_ref[...], preferred_element_type=F32)

    @pl.when(j == pl.num_programs(1) - 1)
    def _():
        o_ref[...] = x_ref[...] + _rms(o_ref[...], gpost_ref[...])


def _ffn(x, gpre, w_up, conv_w, conv_b, w_down, gpost, *, tm, tn, seq):
    t, d = x.shape
    nj = D_FF // tn
    hb = tm // FFN_HALO
    return pl.pallas_call(
        functools.partial(_ffn_kernel, tiles_per_seq=seq // tm),
        out_shape=jax.ShapeDtypeStruct((t, d), F32),
        grid=(t // tm, nj),
        in_specs=[pl.BlockSpec((tm, d), lambda i, j: (i, 0), pipeline_mode=pl.Buffered(1)),
                  pl.BlockSpec((FFN_HALO, d), lambda i, j: (jnp.maximum(i * hb - 1, 0), 0)),
                  pl.BlockSpec((1, d), lambda i, j: (0, 0)),
                  pl.BlockSpec((d, tn), lambda i, j: (0, j)),
                  pl.BlockSpec((d, tn), lambda i, j: (0, nj + j)),
                  pl.BlockSpec((CONV_WIDTH, tn), lambda i, j: (0, j)),
                  pl.BlockSpec((CONV_WIDTH, tn), lambda i, j: (0, nj + j)),
                  pl.BlockSpec((1, tn), lambda i, j: (0, j)),
                  pl.BlockSpec((1, tn), lambda i, j: (0, nj + j)),
                  pl.BlockSpec((tn, d), lambda i, j: (j, 0)),
                  pl.BlockSpec((1, d), lambda i, j: (0, 0))],
        out_specs=pl.BlockSpec((tm, d), lambda i, j: (i, 0), pipeline_mode=pl.Buffered(1)),
        scratch_shapes=[pltpu.VMEM((tm + FFN_HALO, d), BF16)],
        compiler_params=_params(("parallel", "arbitrary")),
        name="conv_glu_ffn",
    )(x, x, gpre, w_up, w_up, conv_w, conv_w, conv_b, conv_b, w_down, gpost)


def _split_in_proj(w_in):
    d = w_in.shape[0]
    r0, g0 = MLSTM_TOTAL, MLSTM_TOTAL + RWKV_TOTAL
    mlstm_main = _bf(w_in[:, 0:MLSTM_MAIN])
    mlstm_if = w_in[:, MLSTM_MAIN:r0]
    rwkv_main = _bf(w_in[:, r0:r0 + RWKV_MAIN])
    rwkv_wa = w_in[:, r0 + RWKV_MAIN:r0 + RWKV_MAIN + WA_W]
    rwkv_g = w_in[:, r0 + RWKV_MAIN + WA_W:g0]
    gates = _bf(w_in[:, g0:])
    z = lambda n: jnp.zeros((d, n), w_in.dtype)
    small = _bf(jnp.concatenate([rwkv_g, z(RG_PAD - RWKV_GATE_RANK), rwkv_wa, mlstm_if, z(IF_W - 2 * MLSTM_HEADS)],
                                axis=1))
    return mlstm_main, rwkv_main, gates, small


def _layer(x, mem, mix_pre_norm, w_in, mlstm_b_i, mlstm_b_f, mlstm_head_norm, rwkv_mu, rwkv_w0, rwkv_w_up, rwkv_a0,
           rwkv_a_up, rwkv_g_up, rwkv_k_k, rwkv_k_a, rwkv_r_k, rwkv_ln_g, rwkv_ln_b, w_branch_a, w_branch_b,
           w_mix_out, mix_post_norm, xattn_pre_norm, mem_norm, xattn_wq, xattn_wkv, xattn_wo, xattn_post_norm,
           ffn_pre_norm, ffn_w_up, ffn_conv_w, ffn_conv_b, ffn_w_down, ffn_post_norm):
    b, s, d = x.shape
    t = b * s
    x2d = x.reshape(t, d)
    row = lambda p: p.reshape(1, -1).astype(F32)
    tm_row = min(TM_ROW, s)

    w_mlstm, w_rwkv, w_gates, w_small = _split_in_proj(w_in)
    tm_plain = min(TM_PLAIN, t)
    mlstm_main, h = _norm_matmul(x2d, row(mix_pre_norm), w_mlstm, tm=min(TM_IN_PROJ, t), tn=TN, out_dtype=BF16,
                                 name="in_proj_mlstm")
    rwkv_main = _matmul(h, w_rwkv, tm=tm_plain, tn=TN, out_dtype=BF16, name="in_proj_rwkv")
    gates = _matmul(h, w_gates, tm=tm_plain, tn=TN, out_dtype=BF16, name="in_proj_gates")
    small = _matmul(h, w_small, tm=tm_plain, tn=SMALL_COLS, out_dtype=F32, name="in_proj_small")
    small3 = small.reshape(b, s, SMALL_COLS)

    gate_bias = jnp.concatenate([mlstm_b_i, mlstm_b_f, jnp.zeros((IF_W - 2 * MLSTM_HEADS,), F32)]).reshape(1, IF_W)
    ha = _mlstm(mlstm_main.reshape(b, s, MLSTM_MAIN), small3, gate_bias, row(mlstm_head_norm))

    mu_r, mu_k, mu_v = (rwkv_mu[i * RWKV_WIDTH:(i + 1) * RWKV_WIDTH] for i in range(3))
    mu_wa = rwkv_mu[RWKV_MAIN:RWKV_MAIN + WA_W]
    mu_g = jnp.pad(rwkv_mu[RWKV_MAIN + WA_W:], (0, RG_PAD - RWKV_GATE_RANK))
    pvec = jnp.stack([mu_r, mu_k, mu_v, rwkv_w0, rwkv_a0, rwkv_k_k, rwkv_k_a, rwkv_r_k.reshape(-1), rwkv_ln_g,
                      rwkv_ln_b] + [jnp.zeros((RWKV_WIDTH,), F32)] * (PV_ROWS - 10))
    g_up = jnp.pad(rwkv_g_up, ((0, RG_PAD - RWKV_GATE_RANK), (0, 0)))
    hb = _rwkv(rwkv_main.reshape(b, s, RWKV_MAIN), small3, pvec, row(mu_g), row(mu_wa), _bf(rwkv_w_up),
               _bf(rwkv_a_up), _bf(g_up))

    x1 = _mix_out(ha.reshape(t, -1), hb.reshape(t, -1), gates, _bf(w_branch_a), _bf(w_branch_b), _bf(w_mix_out),
                  row(mix_post_norm), x2d, tm=tm_row)

    mem2d = mem.reshape(b * MEM_LEN, d)
    kv, _ = _norm_matmul(mem2d, row(mem_norm), _bf(xattn_wkv), tm=MEM_LEN, tn=TN, out_dtype=F32, name="mem_kv")
    x2 = _xattn(x1, row(xattn_pre_norm), _bf(xattn_wq), kv.reshape(b, MEM_LEN, 2 * XATTN_WIDTH), _bf(xattn_wo),
                row(xattn_post_norm), tm=tm_row, seq=s)

    x3 = _ffn(x2, row(ffn_pre_norm), _bf(ffn_w_up), ffn_conv_w, row(ffn_conv_b), _bf(ffn_w_down),
              row(ffn_post_norm), tm=min(TM_FFN, s), tn=TN, seq=s)
    return x3.reshape(b, s, d)


def kernel(x, mem, mix_pre_norm, w_in, mlstm_b_i, mlstm_b_f, mlstm_head_norm, rwkv_mu, rwkv_w0, rwkv_w_up, rwkv_a0, rwkv_a_up, rwkv_g_up, rwkv_k_k, rwkv_k_a, rwkv_r_k, rwkv_ln_g, rwkv_ln_b, w_branch_a, w_branch_b, w_mix_out, mix_post_norm, xattn_pre_norm, mem_norm, xattn_wq, xattn_wkv, xattn_wo, xattn_post_norm, ffn_pre_norm, ffn_w_up, ffn_conv_w, ffn_conv_b, ffn_w_down, ffn_post_norm):
    for l in range(mix_pre_norm.shape[0]):
        x = _layer(x, mem, mix_pre_norm[l], w_in[l], mlstm_b_i[l], mlstm_b_f[l], mlstm_head_norm[l], rwkv_mu[l],
                   rwkv_w0[l], rwkv_w_up[l], rwkv_a0[l], rwkv_a_up[l], rwkv_g_up[l], rwkv_k_k[l], rwkv_k_a[l],
                   rwkv_r_k[l], rwkv_ln_g[l], rwkv_ln_b[l], w_branch_a[l], w_branch_b[l], w_mix_out[l],
                   mix_post_norm[l], xattn_pre_norm[l], mem_norm[l], xattn_wq[l], xattn_wkv[l], xattn_wo[l],
                   xattn_post_norm[l], ffn_pre_norm[l], ffn_w_up[l], ffn_conv_w[l], ffn_conv_b[l], ffn_w_down[l],
                   ffn_post_norm[l])
    return x
```

```python
import functools

import jax
import jax.numpy as jnp
from jax import lax
from jax.experimental import pallas as pl
from jax.experimental.pallas import tpu as pltpu

D_MODEL = 2048
MEM_LEN = 256
RMS_EPS = 1e-6

MLSTM_HEADS = 4
MLSTM_WIDTH = D_MODEL // 2
MLSTM_V_DIM = MLSTM_WIDTH // MLSTM_HEADS
MLSTM_QK_DIM = MLSTM_V_DIM // 2
MLSTM_QK_WIDTH = MLSTM_HEADS * MLSTM_QK_DIM
GATE_SOFTCAP = 15.0

RWKV_WIDTH = D_MODEL // 2
RWKV_HEAD = 64
RWKV_HEADS = RWKV_WIDTH // RWKV_HEAD
RWKV_DECAY_RANK = 64
RWKV_A_RANK = 64
RWKV_GATE_RANK = 160
RWKV_GN_EPS = 64e-5

MLSTM_MAIN = 2 * MLSTM_QK_WIDTH + 2 * MLSTM_WIDTH
MLSTM_TOTAL = MLSTM_MAIN + 2 * MLSTM_HEADS
RWKV_MAIN = 3 * RWKV_WIDTH
RWKV_TOTAL = RWKV_MAIN + RWKV_DECAY_RANK + RWKV_A_RANK + RWKV_GATE_RANK

XATTN_HEADS = 4
XATTN_HEAD_DIM = 128
XATTN_WIDTH = XATTN_HEADS * XATTN_HEAD_DIM

D_FF = 4 * D_MODEL
CONV_WIDTH = 3

RG_PAD = 256
WA_W = RWKV_DECAY_RANK + RWKV_A_RANK
IF_W = 128
COL_RG = 0
COL_WA = COL_RG + RG_PAD
COL_IF = COL_WA + WA_W
SMALL_COLS = COL_IF + IF_W

MLSTM_CHUNK = 256
RWKV_CHUNK = 64
RWKV_CHUNKS_PER_STEP = 2
RWKV_GROUP_W = 256
RWKV_GROUPS = RWKV_WIDTH // RWKV_GROUP_W
VMEM_LIMIT = 56 * 1024 * 1024

LANES = 128
TM_IN_PROJ = 1024
TN_IN_PROJ = 1024
TM_ROW = 512
TM_FFN = 1024
TN = 512

F32 = jnp.float32
BF16 = jnp.bfloat16


def _bf(x):
    return x.astype(BF16)


def _mm(a, b):
    return jnp.dot(_bf(a), _bf(b), preferred_element_type=F32)


def _mm_nt(a, b):
    return lax.dot_general(_bf(a), _bf(b), (((1,), (1,)), ((), ())), preferred_element_type=F32)


def _mm_tn(a, b):
    return lax.dot_general(_bf(a), _bf(b), (((0,), (0,)), ((), ())), preferred_element_type=F32)


def _mm_exact_lhs(tri, x):
    hi = _bf(x)
    r1 = x - hi.astype(F32)
    mid = _bf(r1)
    lo = _bf(r1 - mid.astype(F32))
    t = _bf(tri)
    return (jnp.dot(t, hi, preferred_element_type=F32) + jnp.dot(t, mid, preferred_element_type=F32)
            + jnp.dot(t, lo, preferred_element_type=F32))


def _rms(x, g):
    return x * lax.rsqrt(jnp.mean(x * x, axis=-1, keepdims=True) + RMS_EPS) * g


def _softplus(x):
    return jnp.maximum(x, 0.0) + jnp.log(1.0 + jnp.exp(-jnp.abs(x)))


def _sigmoid(x):
    return 1.0 / (1.0 + jnp.exp(-x))


def _params(sem):
    return pltpu.CompilerParams(dimension_semantics=sem, vmem_limit_bytes=VMEM_LIMIT)


def _norm_matmul_kernel(x_ref, g_ref, w_ref, o_ref, h_ref):
    @pl.when(pl.program_id(1) == 0)
    def _():
        h_ref[...] = _bf(_rms(x_ref[...], g_ref[...]))

    o_ref[...] = jnp.dot(h_ref[...], _bf(w_ref[...]), preferred_element_type=F32).astype(o_ref.dtype)


def _norm_matmul(x, g, w, *, tm, tn, out_dtype, name):
    t, d = x.shape
    n = w.shape[1]
    return pl.pallas_call(
        _norm_matmul_kernel,
        out_shape=(jax.ShapeDtypeStruct((t, n), out_dtype), jax.ShapeDtypeStruct((t, d), BF16)),
        grid=(t // tm, n // tn),
        in_specs=[pl.BlockSpec((tm, d), lambda i, j: (i, 0)),
                  pl.BlockSpec((1, d), lambda i, j: (0, 0)),
                  pl.BlockSpec((d, tn), lambda i, j: (0, j))],
        out_specs=(pl.BlockSpec((tm, tn), lambda i, j: (i, j)),
                   pl.BlockSpec((tm, d), lambda i, j: (i, 0))),
        compiler_params=_params(("parallel", "arbitrary")),
        name=name,
    )(x, g, w)


def _colrange_matmul_kernel(a_ref, w_ref, o_ref, wb_ref, *, shift):
    @pl.when(pl.program_id(1) == 0)
    def _():
        wb_ref[...] = _bf(w_ref[:, shift:shift + wb_ref.shape[1]])

    o_ref[...] = jnp.dot(a_ref[...], wb_ref[...], preferred_element_type=F32).astype(o_ref.dtype)


def _colrange_matmul(a, w, *, col0, ncols, tm, tn, out_dtype, name):
    t, d = a.shape
    base = (col0 // LANES) * LANES
    shift = col0 - base
    window = tn + (LANES if shift else 0)
    return pl.pallas_call(
        functools.partial(_colrange_matmul_kernel, shift=shift),
        out_shape=jax.ShapeDtypeStruct((t, ncols), out_dtype),
        grid=(ncols // tn, t // tm),
        in_specs=[pl.BlockSpec((tm, d), lambda j, i: (i, 0)),
                  pl.BlockSpec((pl.Element(d), pl.Element(window)), lambda j, i: (0, pl.multiple_of(base + tn * j, LANES)))],
        out_specs=pl.BlockSpec((tm, tn), lambda j, i: (i, j)),
        scratch_shapes=[pltpu.VMEM((d, tn), BF16)],
        compiler_params=_params(("parallel", "arbitrary")),
        name=name,
    )(a, w)


def _mlstm_kernel(q_ref, k_ref, v_ref, o_ref, gate_ref, bias_ref, hn_ref, out_ref, ct_ref, n_ref, m_ref):
    L = MLSTM_CHUNK
    dk, dv = MLSTM_QK_DIM, MLSTM_V_DIM

    @pl.when(pl.program_id(1) == 0)
    def _():
        ct_ref[...] = jnp.zeros_like(ct_ref)
        n_ref[...] = jnp.zeros_like(n_ref)
        m_ref[...] = jnp.zeros_like(m_ref)

    pre = gate_ref[0] + bias_ref[...]
    capped = GATE_SOFTCAP * jnp.tanh(pre / GATE_SOFTCAP)
    logf = -_softplus(-capped)
    row = lax.broadcasted_iota(jnp.int32, (L, L), 0)
    col = lax.broadcasted_iota(jnp.int32, (L, L), 1)
    causal = row >= col
    bcum = _mm_exact_lhs(causal.astype(F32), logf)
    ig_t = capped.T
    bcum_t = bcum.T
    scale = MLSTM_QK_DIM ** -0.5

    H = range(MLSTM_HEADS)
    q = [q_ref[0, :, h * dk:(h + 1) * dk] for h in H]
    k = [k_ref[0, :, h * dk:(h + 1) * dk] for h in H]
    v = [v_ref[0, :, h * dv:(h + 1) * dv] for h in H]
    b_c = [bcum[:, MLSTM_HEADS + h:MLSTM_HEADS + h + 1] for h in H]
    b_r = [bcum_t[MLSTM_HEADS + h:MLSTM_HEADS + h + 1, :] for h in H]
    i_c = [capped[:, h:h + 1] for h in H]
    i_r = [ig_t[h:h + 1, :] for h in H]
    m_prev = [m_ref[h][0:1, 0:1] for h in H]
    ct = [ct_ref[h] for h in H]
    nrow = [n_ref[h][0:1, :] for h in H]

    dmat = [jnp.where(causal, b_c[h] - b_r[h] + i_r[h], -jnp.inf) for h in H]
    inter = [b_c[h] + m_prev[h] for h in H]
    m_t = [jnp.maximum(inter[h], jnp.max(dmat[h], axis=-1, keepdims=True)) for h in H]
    dexp = [jnp.exp(dmat[h] - m_t[h]) * scale for h in H]
    w_inter = [jnp.exp(inter[h] - m_t[h]) for h in H]
    qk = [_mm_nt(q[h], k[h]) for h in H]
    qc = [_mm(q[h], ct[h]) for h in H]
    s = [qk[h] * dexp[h] for h in H]
    sv = [_mm(s[h], v[h]) for h in H]
    num = [w_inter[h] * qc[h] + sv[h] for h in H]
    den = [w_inter[h] * jnp.sum(q[h].astype(F32) * nrow[h], axis=-1, keepdims=True)
           + jnp.sum(s[h], axis=-1, keepdims=True) for h in H]
    hh = [num[h] / jnp.maximum(jnp.abs(den[h]), jnp.exp(-m_t[h])) for h in H]

    b_last = [b_c[h][L - 1:L, :] for h in H]
    gs = [b_last[h] - b_c[h] + i_c[h] for h in H]
    m_new = [jnp.maximum(b_last[h] + m_prev[h], jnp.max(gs[h], axis=0, keepdims=True)) for h in H]
    carry_w = [jnp.exp(b_last[h] + m_prev[h] - m_new[h]) for h in H]
    ws = [jnp.exp(gs[h] - m_new[h]) * scale for h in H]
    kv = [_mm_tn(k[h], ws[h] * v[h].astype(F32)) for h in H]
    for h in H:
        ct_ref[h] = carry_w[h] * ct[h] + kv[h]
        n_ref[h] = jnp.broadcast_to(
            carry_w[h] * nrow[h] + jnp.sum(ws[h] * k[h].astype(F32), axis=0, keepdims=True), n_ref.shape[1:])
        m_ref[h] = jnp.broadcast_to(m_new[h], m_ref.shape[1:])

    ms = [jnp.mean(hh[h] * hh[h], axis=-1, keepdims=True) for h in H]
    for h in H:
        hm = hh[h] * lax.rsqrt(ms[h] + RMS_EPS) * hn_ref[:, h * dv:(h + 1) * dv]
        og = o_ref[0, :, h * dv:(h + 1) * dv].astype(F32)
        out_ref[0, :, h * dv:(h + 1) * dv] = (_sigmoid(og) * hm).astype(out_ref.dtype)


def _mlstm(main3, small3, gate_bias, head_norm):
    b, s, _ = main3.shape
    L = min(MLSTM_CHUNK, s)
    assert L == MLSTM_CHUNK and s % L == 0
    qw, vw = MLSTM_QK_WIDTH, MLSTM_WIDTH
    return pl.pallas_call(
        _mlstm_kernel,
        out_shape=jax.ShapeDtypeStruct((b, s, MLSTM_WIDTH), BF16),
        grid=(b, s // L),
        in_specs=[pl.BlockSpec((1, L, qw), lambda i, c: (i, c, 0)),
                  pl.BlockSpec((1, L, qw), lambda i, c: (i, c, 1)),
                  pl.BlockSpec((1, L, vw), lambda i, c: (i, c, 1)),
                  pl.BlockSpec((1, L, vw), lambda i, c: (i, c, 2)),
                  pl.BlockSpec((1, L, IF_W), lambda i, c: (i, c, COL_IF // IF_W)),
                  pl.BlockSpec((1, IF_W), lambda i, c: (0, 0)),
                  pl.BlockSpec((1, vw), lambda i, c: (0, 0))],
        out_specs=pl.BlockSpec((1, L, vw), lambda i, c: (i, c, 0)),
        scratch_shapes=[pltpu.VMEM((MLSTM_HEADS, MLSTM_QK_DIM, MLSTM_V_DIM), F32),
                        pltpu.VMEM((MLSTM_HEADS, 8, MLSTM_QK_DIM), F32),
                        pltpu.VMEM((MLSTM_HEADS, 8, 128), F32)],
        compiler_params=_params(("parallel", "arbitrary")),
        name="mlstm",
    )(main3, main3, main3, main3, small3, gate_bias, head_norm)


PV_MU_R, PV_MU_K, PV_MU_V, PV_W0, PV_A0, PV_KK, PV_KA, PV_RK, PV_LNG, PV_LNB = range(10)
PV_ROWS = 16


def _rwkv_kernel(r_ref, k_ref, v_ref, g_ref, wa_ref, pv_ref, mug_ref, muwa_ref, wup_ref, aup_ref, gup_ref, seg_ref,
                 out_ref, s_ref, pr_ref, pk_ref, pvv_ref, pg_ref, pwa_ref):
    L = RWKV_CHUNK
    CH = RWKV_CHUNKS_PER_STEP
    R = CH * L
    N = RWKV_HEAD
    GW = RWKV_GROUP_W
    GH = GW // N

    @pl.when(pl.program_id(1) == 0)
    def _():
        s_ref[...] = jnp.zeros_like(s_ref)
        pr_ref[...] = jnp.zeros_like(pr_ref)
        pk_ref[...] = jnp.zeros_like(pk_ref)
        pvv_ref[...] = jnp.zeros_like(pvv_ref)
        pg_ref[...] = jnp.zeros_like(pg_ref)
        pwa_ref[...] = jnp.zeros_like(pwa_ref)

    def shift_lerp(x_ref, prev_ref, mu):
        x = x_ref[0].astype(F32)
        rid = lax.broadcasted_iota(jnp.int32, x.shape, 0)
        xs = jnp.where(rid == 0, prev_ref[0:1, :], pltpu.roll(x, 1, 0))
        prev_ref[0:1, :] = x[R - 1:R, :]
        return x + (xs - x) * mu

    pv = pv_ref[...]
    row = lambda i: pv[i:i + 1, :]
    r = shift_lerp(r_ref, pr_ref, row(PV_MU_R))
    kr = shift_lerp(k_ref, pk_ref, row(PV_MU_K))
    v = shift_lerp(v_ref, pvv_ref, row(PV_MU_V))
    gl = shift_lerp(g_ref, pg_ref, mug_ref[...])
    wa = shift_lerp(wa_ref, pwa_ref, muwa_ref[...])

    wl = jnp.tanh(wa[:, 0:RWKV_DECAY_RANK])
    al = wa[:, RWKV_DECAY_RANK:WA_W]
    w_log = -_softplus(-(row(PV_W0) + jnp.dot(_bf(wl), wup_ref[...], preferred_element_type=F32))) - 0.5
    lw = -jnp.exp(w_log)
    a = _sigmoid(row(PV_A0) + jnp.dot(_bf(al), aup_ref[...], preferred_element_type=F32))
    gg = jnp.dot(_bf(_sigmoid(gl)), gup_ref[...], preferred_element_type=F32)

    tr = lax.broadcasted_iota(jnp.int32, (R, R), 0)
    tc = lax.broadcasted_iota(jnp.int32, (R, R), 1)
    tri = (tr >= tc) & ((tr // L) == (tc // L))
    cs = _mm_exact_lhs(tri.astype(F32), lw)
    g_incl = jnp.exp(cs)
    g_excl = jnp.exp(cs - lw)
    g_inv = jnp.exp(-cs)

    kk0 = kr * row(PV_KK)
    kr2 = kr * (1.0 + (a - 1.0) * row(PV_KA))
    rt_all = r * g_incl
    kt_all = kr2 * g_inv
    bonus_all = r * kr2 * row(PV_RK)

    seg = seg_ref[...]
    ti = lax.broadcasted_iota(jnp.int32, (L, GW), 0)
    lane = lax.broadcasted_iota(jnp.int32, (L, GW), 1)
    si = lane & (N - 1)
    lane_head = lane // N
    incl = ti >= si
    strict = ti > si
    eye = (ti == si).astype(F32)
    lvl_masks = []
    bsz = 1
    while bsz < L:
        same = (ti // (2 * bsz)) == (si // (2 * bsz))
        lvl_masks.append(jnp.where(same & ((ti & bsz) != 0) & ((si & bsz) == 0), 1.0, 0.0))
        bsz *= 2

    def bd(x):
        return jnp.concatenate([_bf(x)] * GH, axis=0) * seg

    def fold(z):
        acc = z[0:N]
        for hh in range(1, GH):
            acc = jnp.where(lane_head == hh, z[hh * N:(hh + 1) * N], acc)
        return acc

    def mmb(x, b):
        return jnp.dot(_bf(x), b, preferred_element_type=F32)

    def mmb_nt(x, b):
        return lax.dot_general(_bf(x), b, (((1,), (1,)), ((), ())), preferred_element_type=F32)

    cat = lambda u, w_: jnp.concatenate([u, w_], axis=0)
    groups = range(RWKV_GROUPS)
    E = [(c, g) for c in range(CH) for g in groups]
    rows = lambda c: slice(c * L, (c + 1) * L)
    lanes = lambda g: slice(g * GW, (g + 1) * GW)
    blk = lambda x: [x[rows(c), lanes(g)] for c, g in E]
    n_e = range(len(E))

    kk0_e, bonus_src, gex_e, a_e, ginv_e = blk(kk0), blk(bonus_all), blk(g_excl), blk(a), blk(g_inv)
    kt, rt, vv = blk(kt_all), blk(rt_all), blk(v)
    gl_ = [g_incl[c * L + L - 1:c * L + L, lanes(g)] for c, g in E]
    sums0 = [mmb(cat(kk0_e[e] * kk0_e[e], bonus_src[e]), seg) for e in n_e]
    kk = [kk0_e[e] * lax.rsqrt(jnp.maximum(sums0[e][:L], 1e-24)) for e in n_e]
    bonus = [sums0[e][L:] for e in n_e]
    p = [gex_e[e] * kk[e] for e in n_e]
    qt = [kk[e] * a_e[e] * ginv_e[e] for e in n_e]
    qg = [qt[e] * gl_[e] for e in n_e]
    kg = [kt[e] * gl_[e] for e in n_e]

    lhs_pr = [cat(p[e], rt[e]) for e in n_e]
    gq = [mmb_nt(lhs_pr[e], bd(qt[e])) for e in n_e]
    gk = [mmb_nt(lhs_pr[e], bd(kt[e])) for e in n_e]
    n_pq = [jnp.where(strict, gq[e][:L], 0.0) for e in n_e]
    a_rq = [jnp.where(incl, gq[e][L:], 0.0) for e in n_e]
    a_pk = [jnp.where(strict, gk[e][:L], 0.0) for e in n_e]
    a_rk = [jnp.where(incl, gk[e][L:], 0.0) for e in n_e]

    x = [eye - n_pq[e] * lvl_masks[0] for e in n_e]
    bd_n = [bd(n_pq[e]) for e in n_e]
    for msk in lvl_masks[1:]:
        t1 = [mmb(x[e], bd_n[e]) for e in n_e]
        x = [x[e] - msk * mmb(t1[e], bd(x[e])) for e in n_e]

    av = [mmb(cat(a_pk[e], a_rk[e]), bd(vv[e])) for e in n_e]
    w = [mmb(x[e], bd(p[e])) for e in n_e]
    u0 = [mmb(x[e], bd(av[e][:L])) for e in n_e]
    gmat = [rt[e] - mmb(a_rq[e], bd(w[e])) for e in n_e]
    y0 = [av[e][L:] - mmb(a_rq[e], bd(u0[e])) for e in n_e]
    mt = [eye * gl_[e] - fold(_mm_tn(qg[e], w[e])) for e in n_e]
    bt = [fold(_mm_tn(cat(kg[e], -qg[e]), cat(vv[e], u0[e]))) for e in n_e]

    state = [s_ref[g] for g in groups]
    y = [None] * len(E)
    for c in range(CH):
        es = [c * RWKV_GROUPS + g for g in groups]
        ys = [mmb(cat(gmat[e], mt[e]), bd(state[g])) for g, e in zip(groups, es)]
        for g, e in zip(groups, es):
            y[e] = ys[g][:L] + y0[e]
            state[g] = ys[g][L:] + bt[e]
    for g in groups:
        s_ref[g] = state[g]

    inv_n = 1.0 / N
    sums1 = [mmb(cat(y[e], y[e] * y[e]), seg) for e in n_e]
    for e, (c, g) in enumerate(E):
        mean = sums1[e][:L] * inv_n
        var = sums1[e][L:] * inv_n - mean * mean
        yn = (y[e] - mean) * lax.rsqrt(var + RWKV_GN_EPS) * row(PV_LNG)[:, lanes(g)] + row(PV_LNB)[:, lanes(g)]
        yn = yn + bonus[e] * vv[e]
        out_ref[0, rows(c), lanes(g)] = (yn * gg[rows(c), lanes(g)]).astype(out_ref.dtype)


def _rwkv(main3, small3, pvec, mu_g, mu_wa, w_up, a_up, g_up):
    b, s, _ = main3.shape
    L = RWKV_CHUNK * RWKV_CHUNKS_PER_STEP
    rw = RWKV_WIDTH
    head_of = jnp.arange(RWKV_GROUP_W) // RWKV_HEAD
    seg = (head_of[:, None] == head_of[None, :]).astype(BF16)
    const = lambda shape: pl.BlockSpec(shape, lambda i, c: (0, 0))
    return pl.pallas_call(
        _rwkv_kernel,
        out_shape=jax.ShapeDtypeStruct((b, s, rw), BF16),
        grid=(b, s // L),
        in_specs=[pl.BlockSpec((1, L, rw), lambda i, c: (i, c, 0)),
                  pl.BlockSpec((1, L, rw), lambda i, c: (i, c, 1)),
                  pl.BlockSpec((1, L, rw), lambda i, c: (i, c, 2)),
                  pl.BlockSpec((1, L, RG_PAD), lambda i, c: (i, c, COL_RG // RG_PAD)),
                  pl.BlockSpec((1, L, WA_W), lambda i, c: (i, c, COL_WA // WA_W)),
                  const((PV_ROWS, rw)), const((1, RG_PAD)), const((1, WA_W)),
                  const((RWKV_DECAY_RANK, rw)), const((RWKV_A_RANK, rw)), const((RG_PAD, rw)),
                  const((RWKV_GROUP_W, RWKV_GROUP_W))],
        out_specs=pl.BlockSpec((1, L, rw), lambda i, c: (i, c, 0)),
        scratch_shapes=[pltpu.VMEM((RWKV_GROUPS, RWKV_HEAD, RWKV_GROUP_W), F32),
                        pltpu.VMEM((8, rw), F32), pltpu.VMEM((8, rw), F32), pltpu.VMEM((8, rw), F32),
                        pltpu.VMEM((8, RG_PAD), F32), pltpu.VMEM((8, WA_W), F32)],
        compiler_params=_params(("parallel", "arbitrary")),
        name="rwkv7",
    )(main3, main3, main3, small3, small3, pvec, mu_g, mu_wa, w_up, a_up, g_up, seg)


def _mix_out_kernel(ha_ref, hb_ref, ga_ref, gb_ref, wa_ref, wb_ref, wo_ref, g_ref, x_ref, o_ref):
    ya = jnp.dot(ha_ref[...], wa_ref[...], preferred_element_type=F32)
    yb = jnp.dot(hb_ref[...], wb_ref[...], preferred_element_type=F32)
    merged = _bf(_sigmoid(ga_ref[...].astype(F32)) * ya + _sigmoid(gb_ref[...].astype(F32)) * yb)
    y = jnp.dot(merged, wo_ref[...], preferred_element_type=F32)
    o_ref[...] = x_ref[...] + _rms(y, g_ref[...])


def _mix_out(ha, hb, gates, wa, wb, wo, g, x, *, tm):
    t, kdim = ha.shape
    d = wo.shape[1]
    resident = lambda shape: pl.BlockSpec(shape, lambda i: (0, 0), pipeline_mode=pl.Buffered(1))
    return pl.pallas_call(
        _mix_out_kernel,
        out_shape=jax.ShapeDtypeStruct((t, d), F32),
        grid=(t // tm,),
        in_specs=[pl.BlockSpec((tm, kdim), lambda i: (i, 0)),
                  pl.BlockSpec((tm, kdim), lambda i: (i, 0)),
                  pl.BlockSpec((tm, d), lambda i: (i, 0)),
                  pl.BlockSpec((tm, d), lambda i: (i, 1)),
                  resident((kdim, d)), resident((kdim, d)), resident((d, d)),
                  pl.BlockSpec((1, d), lambda i: (0, 0)),
                  pl.BlockSpec((tm, d), lambda i: (i, 0))],
        out_specs=pl.BlockSpec((tm, d), lambda i: (i, 0)),
        compiler_params=_params(("parallel",)),
        name="mix_out",
    )(ha, hb, gates, gates, wa, wb, wo, g, x)


def _xattn_kernel(x_ref, gpre_ref, wq_ref, kv_ref, wo_ref, gpost_ref, o_ref):
    x = x_ref[...]
    h = _bf(_rms(x, gpre_ref[...]))
    q = jnp.dot(h, wq_ref[...], preferred_element_type=F32)
    scale = XATTN_HEAD_DIM ** -0.5
    outs = []
    for hd in range(XATTN_HEADS):
        sl = slice(hd * XATTN_HEAD_DIM, (hd + 1) * XATTN_HEAD_DIM)
        k = kv_ref[0, :, sl]
        v = kv_ref[0, :, XATTN_WIDTH + hd * XATTN_HEAD_DIM:XATTN_WIDTH + (hd + 1) * XATTN_HEAD_DIM]
        sc = _mm_nt(q[:, sl], k) * scale
        sc = sc - jnp.max(sc, axis=-1, keepdims=True)
        e = jnp.exp(sc)
        p = e / jnp.sum(e, axis=-1, keepdims=True)
        outs.append(_mm(p, v))
    o = jnp.concatenate(outs, axis=-1)
    y = jnp.dot(_bf(o), wo_ref[...], preferred_element_type=F32)
    o_ref[...] = x + _rms(y, gpost_ref[...])


def _xattn(x, gpre, wq, kv3, wo, gpost, *, tm, seq):
    t, d = x.shape
    per_seq = seq // tm
    return pl.pallas_call(
        _xattn_kernel,
        out_shape=jax.ShapeDtypeStruct((t, d), F32),
        grid=(t // tm,),
        in_specs=[pl.BlockSpec((tm, d), lambda i: (i, 0)),
                  pl.BlockSpec((1, d), lambda i: (0, 0)),
                  pl.BlockSpec((d, XATTN_WIDTH), lambda i: (0, 0)),
                  pl.BlockSpec((1, MEM_LEN, 2 * XATTN_WIDTH), lambda i: (i // per_seq, 0, 0)),
                  pl.BlockSpec((XATTN_WIDTH, d), lambda i: (0, 0)),
                  pl.BlockSpec((1, d), lambda i: (0, 0))],
        out_specs=pl.BlockSpec((tm, d), lambda i: (i, 0)),
        compiler_params=_params(("parallel",)),
        name="xattn",
    )(x, gpre, wq, kv3, wo, gpost)


FFN_HALO = 8


def _gelu_tanh(x):
    return 0.5 * x * (1.0 + jnp.tanh(0.7978845608028654 * (x + 0.044715 * x * x * x)))


def _ffn_kernel(x_ref, halo_ref, gpre_ref, wg_ref, wu_ref, cwg_ref, cwu_ref, cbg_ref, cbu_ref, wd_ref, gpost_ref,
                o_ref, h_ref, *, tiles_per_seq):
    i = pl.program_id(0)
    j = pl.program_id(1)

    @pl.when(j == 0)
    def _():
        keep = jnp.where(i % tiles_per_seq == 0, 0.0, 1.0)
        h_ref[0:FFN_HALO, :] = _bf(_rms(halo_ref[...], gpre_ref[...]) * keep)
        h_ref[FFN_HALO:, :] = _bf(_rms(x_ref[...], gpre_ref[...]))
        o_ref[...] = jnp.zeros_like(o_ref)

    def conv(w_ref, cw_ref, cb_ref):
        u = jnp.dot(h_ref[...], w_ref[...], preferred_element_type=F32)
        u1 = pltpu.roll(u, 1, 0)
        u2 = pltpu.roll(u, 2, 0)
        cw = cw_ref[...]
        full = cb_ref[...] + cw[2:3, :] * u + cw[1:2, :] * u1 + cw[0:1, :] * u2
        return full[FFN_HALO:, :]

    gate = conv(wg_ref, cwg_ref, cbg_ref)
    up = conv(wu_ref, cwu_ref, cbu_ref)
    act = _bf(_gelu_tanh(gate) * up)
    o_ref[...] += jnp.dot(act, wd_ref[...], preferred_element_type=F32)

    @pl.when(j == pl.num_programs(1) - 1)
    def _():
        o_ref[...] = x_ref[...] + _rms(o_ref[...], gpost_ref[...])


def _ffn(x, gpre, w_up, conv_w, conv_b, w_down, gpost, *, tm, tn, seq):
    t, d = x.shape
    nj = D_FF // tn
    hb = tm // FFN_HALO
    return pl.pallas_call(
        functools.partial(_ffn_kernel, tiles_per_seq=seq // tm),
        out_shape=jax.ShapeDtypeStruct((t, d), F32),
        grid=(t // tm, nj),
        in_specs=[pl.BlockSpec((tm, d), lambda i, j: (i, 0)),
                  pl.BlockSpec((FFN_HALO, d), lambda i, j: (jnp.maximum(i * hb - 1, 0), 0)),
                  pl.BlockSpec((1, d), lambda i, j: (0, 0)),
                  pl.BlockSpec((d, tn), lambda i, j: (0, j)),
                  pl.BlockSpec((d, tn), lambda i, j: (0, nj + j)),
                  pl.BlockSpec((CONV_WIDTH, tn), lambda i, j: (0, j)),
                  pl.BlockSpec((CONV_WIDTH, tn), lambda i, j: (0, nj + j)),
                  pl.BlockSpec((1, tn), lambda i, j: (0, j)),
                  pl.BlockSpec((1, tn), lambda i, j: (0, nj + j)),
                  pl.BlockSpec((tn, d), lambda i, j: (j, 0)),
                  pl.BlockSpec((1, d), lambda i, j: (0, 0))],
        out_specs=pl.BlockSpec((tm, d), lambda i, j: (i, 0), pipeline_mode=pl.Buffered(1)),
        scratch_shapes=[pltpu.VMEM((tm + FFN_HALO, d), BF16)],
        compiler_params=_params(("parallel", "arbitrary")),
        name="conv_glu_ffn",
    )(x, x, gpre, w_up, w_up, conv_w, conv_w, conv_b, conv_b, w_down, gpost)


def _small_in_proj(w_in):
    d = w_in.shape[0]
    r0, g0 = MLSTM_TOTAL, MLSTM_TOTAL + RWKV_TOTAL
    mlstm_if = w_in[:, MLSTM_MAIN:r0]
    rwkv_wa = w_in[:, r0 + RWKV_MAIN:r0 + RWKV_MAIN + WA_W]
    rwkv_g = w_in[:, r0 + RWKV_MAIN + WA_W:g0]
    z = lambda n: jnp.zeros((d, n), w_in.dtype)
    return jnp.concatenate([rwkv_g, z(RG_PAD - RWKV_GATE_RANK), rwkv_wa, mlstm_if, z(IF_W - 2 * MLSTM_HEADS)], axis=1)


def _layer(x, mem, mix_pre_norm, w_in, mlstm_b_i, mlstm_b_f, mlstm_head_norm, rwkv_mu, rwkv_w0, rwkv_w_up, rwkv_a0,
           rwkv_a_up, rwkv_g_up, rwkv_k_k, rwkv_k_a, rwkv_r_k, rwkv_ln_g, rwkv_ln_b, w_branch_a, w_branch_b,
           w_mix_out, mix_post_norm, xattn_pre_norm, mem_norm, xattn_wq, xattn_wkv, xattn_wo, xattn_post_norm,
           ffn_pre_norm, ffn_w_up, ffn_conv_w, ffn_conv_b, ffn_w_down, ffn_post_norm):
    b, s, d = x.shape
    t = b * s
    x2d = x.reshape(t, d)
    row = lambda p: p.reshape(1, -1).astype(F32)
    tm_row = min(TM_ROW, s)

    tm_in = min(TM_IN_PROJ, t)
    small, h = _norm_matmul(x2d, row(mix_pre_norm), _small_in_proj(w_in), tm=tm_in, tn=SMALL_COLS, out_dtype=F32,
                            name="in_proj_small")
    small3 = small.reshape(b, s, SMALL_COLS)
    wide = functools.partial(_colrange_matmul, h, w_in, tm=tm_in, tn=TN_IN_PROJ, out_dtype=BF16)
    mlstm_main = wide(col0=0, ncols=MLSTM_MAIN, name="in_proj_mlstm")
    rwkv_main = wide(col0=MLSTM_TOTAL, ncols=RWKV_MAIN, name="in_proj_rwkv")
    gates = wide(col0=MLSTM_TOTAL + RWKV_TOTAL, ncols=2 * D_MODEL, name="in_proj_gates")

    gate_bias = jnp.concatenate([mlstm_b_i, mlstm_b_f, jnp.zeros((IF_W - 2 * MLSTM_HEADS,), F32)]).reshape(1, IF_W)
    ha = _mlstm(mlstm_main.reshape(b, s, MLSTM_MAIN), small3, gate_bias, row(mlstm_head_norm))

    mu_r, mu_k, mu_v = (rwkv_mu[i * RWKV_WIDTH:(i + 1) * RWKV_WIDTH] for i in range(3))
    mu_wa = rwkv_mu[RWKV_MAIN:RWKV_MAIN + WA_W]
    mu_g = jnp.pad(rwkv_mu[RWKV_MAIN + WA_W:], (0, RG_PAD - RWKV_GATE_RANK))
    pvec = jnp.stack([mu_r, mu_k, mu_v, rwkv_w0, rwkv_a0, rwkv_k_k, rwkv_k_a, rwkv_r_k.reshape(-1), rwkv_ln_g,
                      rwkv_ln_b] + [jnp.zeros((RWKV_WIDTH,), F32)] * (PV_ROWS - 10))
    g_up = jnp.pad(rwkv_g_up, ((0, RG_PAD - RWKV_GATE_RANK), (0, 0)))
    hb = _rwkv(rwkv_main.reshape(b, s, RWKV_MAIN), small3, pvec, row(mu_g), row(mu_wa), _bf(rwkv_w_up),
               _bf(rwkv_a_up), _bf(g_up))

    x1 = _mix_out(ha.reshape(t, -1), hb.reshape(t, -1), gates, _bf(w_branch_a), _bf(w_branch_b), _bf(w_mix_out),
                  row(mix_post_norm), x2d, tm=tm_row)

    mem2d = mem.reshape(b * MEM_LEN, d)
    kv, _ = _norm_matmul(mem2d, row(mem_norm), _bf(xattn_wkv), tm=MEM_LEN, tn=TN, out_dtype=F32, name="mem_kv")
    x2 = _xattn(x1, row(xattn_pre_norm), _bf(xattn_wq), kv.reshape(b, MEM_LEN, 2 * XATTN_WIDTH), _bf(xattn_wo),
                row(xattn_post_norm), tm=tm_row, seq=s)

    x3 = _ffn(x2, row(ffn_pre_norm), _bf(ffn_w_up), ffn_conv_w, row(ffn_conv_b), _bf(ffn_w_down),
              row(ffn_post_norm), tm=min(TM_FFN, s), tn=TN, seq=s)
    return x3.reshape(b, s, d)


def kernel(x, mem, mix_pre_norm, w_in, mlstm_b_i, mlstm_b_f, mlstm_head_norm, rwkv_mu, rwkv_w0, rwkv_w_up, rwkv_a0, rwkv_a_up, rwkv_g_up, rwkv_k_k, rwkv_k_a, rwkv_r_k, rwkv_ln_g, rwkv_ln_b, w_branch_a, w_branch_b, w_mix_out, mix_post_norm, xattn_pre_norm, mem_norm, xattn_wq, xattn_wkv, xattn_wo, xattn_post_norm, ffn_pre_norm, ffn_w_up, ffn_conv_w, ffn_conv_b, ffn_w_down, ffn_post_norm):
    for l in range(mix_pre_norm.shape[0]):
        x = _layer(x, mem, mix_pre_norm[l], w_in[l], mlstm_b_i[l], mlstm_b_f[l], mlstm_head_norm[l], rwkv_mu[l],
                   rwkv_w0[l], rwkv_w_up[l], rwkv_a0[l], rwkv_a_up[l], rwkv_g_up[l], rwkv_k_k[l], rwkv_k_a[l],
                   rwkv_r_k[l], rwkv_ln_g[l], rwkv_ln_b[l], w_branch_a[l], w_branch_b[l], w_mix_out[l],
                   mix_post_norm[l], xattn_pre_norm[l], mem_norm[l], xattn_wq[l], xattn_wkv[l], xattn_wo[l],
                   xattn_post_norm[l], ffn_pre_norm[l], ffn_w_up[l], ffn_conv_w[l], ffn_conv_b[l], ffn_w_down[l],
                   ffn_post_norm[l])
    return x
```

```python
import functools

import jax
import jax.numpy as jnp
from jax import lax
from jax.experimental import pallas as pl
from jax.experimental.pallas import tpu as pltpu

D_MODEL = 2048
MEM_LEN = 256
RMS_EPS = 1e-6

MLSTM_HEADS = 4
MLSTM_WIDTH = D_MODEL // 2
MLSTM_V_DIM = MLSTM_WIDTH // MLSTM_HEADS
MLSTM_QK_DIM = MLSTM_V_DIM // 2
MLSTM_QK_WIDTH = MLSTM_HEADS * MLSTM_QK_DIM
GATE_SOFTCAP = 15.0

RWKV_WIDTH = D_MODEL // 2
RWKV_HEAD = 64
RWKV_HEADS = RWKV_WIDTH // RWKV_HEAD
RWKV_DECAY_RANK = 64
RWKV_A_RANK = 64
RWKV_GATE_RANK = 160
RWKV_GN_EPS = 64e-5

MLSTM_MAIN = 2 * MLSTM_QK_WIDTH + 2 * MLSTM_WIDTH
MLSTM_TOTAL = MLSTM_MAIN + 2 * MLSTM_HEADS
RWKV_MAIN = 3 * RWKV_WIDTH
RWKV_TOTAL = RWKV_MAIN + RWKV_DECAY_RANK + RWKV_A_RANK + RWKV_GATE_RANK

XATTN_HEADS = 4
XATTN_HEAD_DIM = 128
XATTN_WIDTH = XATTN_HEADS * XATTN_HEAD_DIM

D_FF = 4 * D_MODEL
CONV_WIDTH = 3

RG_PAD = 256
WA_W = RWKV_DECAY_RANK + RWKV_A_RANK
IF_W = 128
COL_RG = 0
COL_WA = COL_RG + RG_PAD
COL_IF = COL_WA + WA_W
SMALL_COLS = COL_IF + IF_W

MLSTM_CHUNK = 256
RWKV_CHUNK = 64
RWKV_CHUNKS_PER_STEP = 2
RWKV_GROUP_W = 256
RWKV_GROUPS = RWKV_WIDTH // RWKV_GROUP_W
VMEM_LIMIT = 56 * 1024 * 1024

SUBLANES = 8
BF16_SUBLANES = 16
TM_IN_PROJ = 1024
TN_IN_PROJ = 1024
TM_ROW = 512
TM_FFN = 1024
TN = 512

F32 = jnp.float32
BF16 = jnp.bfloat16


def _bf(x):
    return x.astype(BF16)


def _mm(a, b):
    return jnp.dot(_bf(a), _bf(b), preferred_element_type=F32)


def _mm_nt(a, b):
    return lax.dot_general(_bf(a), _bf(b), (((1,), (1,)), ((), ())), preferred_element_type=F32)


def _mm_tn(a, b):
    return lax.dot_general(_bf(a), _bf(b), (((0,), (0,)), ((), ())), preferred_element_type=F32)


def _mm_exact_lhs(tri, x):
    hi = _bf(x)
    r1 = x - hi.astype(F32)
    mid = _bf(r1)
    lo = _bf(r1 - mid.astype(F32))
    t = _bf(tri)
    return (jnp.dot(t, hi, preferred_element_type=F32) + jnp.dot(t, mid, preferred_element_type=F32)
            + jnp.dot(t, lo, preferred_element_type=F32))


def _rms(x, g):
    return x * lax.rsqrt(jnp.mean(x * x, axis=-1, keepdims=True) + RMS_EPS) * g


def _softplus(x):
    return jnp.maximum(x, 0.0) + jnp.log(1.0 + jnp.exp(-jnp.abs(x)))


def _sigmoid(x):
    return 1.0 / (1.0 + jnp.exp(-x))


def _params(sem):
    return pltpu.CompilerParams(dimension_semantics=sem, vmem_limit_bytes=VMEM_LIMIT)


def _call_with_casts(body, *, n_in, out_shape, grid, in_specs, out_specs, cast_arrays=(), **kwargs):
    n_out = len(out_shape)
    k = len(cast_arrays)
    nsteps = 1
    for extent in grid:
        nsteps *= extent

    def step(*ids):
        lin = ids[0]
        for extent, idx in zip(grid[1:], ids[1:]):
            lin = lin * extent + idx
        return lin

    band_specs, band_shapes = [], []
    for arr in cast_arrays:
        rows, cols = arr.shape
        assert rows % nsteps == 0 and (rows // nsteps) % BF16_SUBLANES == 0, (arr.shape, nsteps)
        band_specs.append(pl.BlockSpec((rows // nsteps, cols), lambda *ids: (step(*ids), 0)))
        band_shapes.append(jax.ShapeDtypeStruct(arr.shape, BF16))

    def kernel(*refs):
        ins, src = refs[:n_in], refs[n_in:n_in + k]
        outs, dst = refs[n_in + k:n_in + k + n_out], refs[n_in + k + n_out:n_in + 2 * k + n_out]
        for s_ref, d_ref in zip(src, dst):
            d_ref[...] = s_ref[...].astype(d_ref.dtype)
        body(*ins, *outs, *refs[n_in + 2 * k + n_out:])

    def run(*operands):
        assert len(operands) == n_in
        return pl.pallas_call(kernel, out_shape=tuple(out_shape) + tuple(band_shapes), grid=grid,
                              in_specs=list(in_specs) + band_specs, out_specs=tuple(out_specs) + tuple(band_specs),
                              **kwargs)(*operands, *cast_arrays)

    return run


def _norm_matmul_kernel(x_ref, g_ref, w_ref, o_ref, h_ref, *, w_is_transposed):
    @pl.when(pl.program_id(1) == 0)
    def _():
        h_ref[...] = _bf(_rms(x_ref[...], g_ref[...]))

    w = w_ref[...].T if w_is_transposed else w_ref[...]
    o_ref[...] = jnp.dot(h_ref[...], _bf(w), preferred_element_type=F32).astype(o_ref.dtype)


def _norm_matmul(x, g, w, *, tm, tn, out_dtype, name, w_is_transposed=False):
    t, d = x.shape
    n = w.shape[0] if w_is_transposed else w.shape[1]
    w_spec = pl.BlockSpec((tn, d), lambda i, j: (j, 0)) if w_is_transposed else pl.BlockSpec((d, tn), lambda i, j: (0, j))
    return pl.pallas_call(
        functools.partial(_norm_matmul_kernel, w_is_transposed=w_is_transposed),
        out_shape=(jax.ShapeDtypeStruct((t, n), out_dtype), jax.ShapeDtypeStruct((t, d), BF16)),
        grid=(t // tm, n // tn),
        in_specs=[pl.BlockSpec((tm, d), lambda i, j: (i, 0)),
                  pl.BlockSpec((1, d), lambda i, j: (0, 0)),
                  w_spec],
        out_specs=(pl.BlockSpec((tm, tn), lambda i, j: (i, j)),
                   pl.BlockSpec((tm, d), lambda i, j: (i, 0))),
        compiler_params=_params(("parallel", "arbitrary")),
        name=name,
    )(x, g, w)


def _rowrange_matmul_kernel(a_ref, wt_ref, o_ref, wb_ref):
    @pl.when(pl.program_id(1) == 0)
    def _():
        wb_ref[...] = _bf(wt_ref[...].T)

    o_ref[...] = jnp.dot(a_ref[...], wb_ref[...], preferred_element_type=F32).astype(o_ref.dtype)


def _rowrange_matmul(a, w_t, *, row0, nrows, tm, tn, out_dtype, name, cast_arrays=()):
    t, d = a.shape
    assert row0 % SUBLANES == 0 and nrows % tn == 0
    return _call_with_casts(
        _rowrange_matmul_kernel, n_in=2,
        out_shape=[jax.ShapeDtypeStruct((t, nrows), out_dtype)],
        grid=(nrows // tn, t // tm),
        in_specs=[pl.BlockSpec((tm, d), lambda j, i: (i, 0)),
                  pl.BlockSpec((pl.Element(tn), pl.Element(d)),
                               lambda j, i: (pl.multiple_of(row0 + tn * j, SUBLANES), 0))],
        out_specs=[pl.BlockSpec((tm, tn), lambda j, i: (i, j))],
        cast_arrays=cast_arrays,
        scratch_shapes=[pltpu.VMEM((d, tn), BF16)],
        compiler_params=_params(("parallel", "arbitrary")),
        name=name,
    )(a, w_t)


def _mlstm_kernel(q_ref, k_ref, v_ref, o_ref, gate_ref, bias_ref, hn_ref, out_ref, ct_ref, n_ref, m_ref):
    L = MLSTM_CHUNK
    dk, dv = MLSTM_QK_DIM, MLSTM_V_DIM

    @pl.when(pl.program_id(1) == 0)
    def _():
        ct_ref[...] = jnp.zeros_like(ct_ref)
        n_ref[...] = jnp.zeros_like(n_ref)
        m_ref[...] = jnp.zeros_like(m_ref)

    pre = gate_ref[0] + bias_ref[...]
    capped = GATE_SOFTCAP * jnp.tanh(pre / GATE_SOFTCAP)
    logf = -_softplus(-capped)
    row = lax.broadcasted_iota(jnp.int32, (L, L), 0)
    col = lax.broadcasted_iota(jnp.int32, (L, L), 1)
    causal = row >= col
    bcum = _mm_exact_lhs(causal.astype(F32), logf)
    ig_t = capped.T
    bcum_t = bcum.T
    scale = MLSTM_QK_DIM ** -0.5

    H = range(MLSTM_HEADS)
    q = [q_ref[0, :, h * dk:(h + 1) * dk] for h in H]
    k = [k_ref[0, :, h * dk:(h + 1) * dk] for h in H]
    v = [v_ref[0, :, h * dv:(h + 1) * dv] for h in H]
    b_c = [bcum[:, MLSTM_HEADS + h:MLSTM_HEADS + h + 1] for h in H]
    b_r = [bcum_t[MLSTM_HEADS + h:MLSTM_HEADS + h + 1, :] for h in H]
    i_c = [capped[:, h:h + 1] for h in H]
    i_r = [ig_t[h:h + 1, :] for h in H]
    m_prev = [m_ref[h][0:1, 0:1] for h in H]
    ct = [ct_ref[h] for h in H]
    nrow = [n_ref[h][0:1, :] for h in H]

    dmat = [jnp.where(causal, b_c[h] - b_r[h] + i_r[h], -jnp.inf) for h in H]
    inter = [b_c[h] + m_prev[h] for h in H]
    m_t = [jnp.maximum(inter[h], jnp.max(dmat[h], axis=-1, keepdims=True)) for h in H]
    dexp = [jnp.exp(dmat[h] - m_t[h]) * scale for h in H]
    w_inter = [jnp.exp(inter[h] - m_t[h]) for h in H]
    qk = [_mm_nt(q[h], k[h]) for h in H]
    qc = [_mm(q[h], ct[h]) for h in H]
    s = [qk[h] * dexp[h] for h in H]
    sv = [_mm(s[h], v[h]) for h in H]
    num = [w_inter[h] * qc[h] + sv[h] for h in H]
    den = [w_inter[h] * jnp.sum(q[h].astype(F32) * nrow[h], axis=-1, keepdims=True)
           + jnp.sum(s[h], axis=-1, keepdims=True) for h in H]
    hh = [num[h] / jnp.maximum(jnp.abs(den[h]), jnp.exp(-m_t[h])) for h in H]

    b_last = [b_c[h][L - 1:L, :] for h in H]
    gs = [b_last[h] - b_c[h] + i_c[h] for h in H]
    m_new = [jnp.maximum(b_last[h] + m_prev[h], jnp.max(gs[h], axis=0, keepdims=True)) for h in H]
    carry_w = [jnp.exp(b_last[h] + m_prev[h] - m_new[h]) for h in H]
    ws = [jnp.exp(gs[h] - m_new[h]) * scale for h in H]
    kv = [_mm_tn(k[h], ws[h] * v[h].astype(F32)) for h in H]
    for h in H:
        ct_ref[h] = carry_w[h] * ct[h] + kv[h]
        n_ref[h] = jnp.broadcast_to(
            carry_w[h] * nrow[h] + jnp.sum(ws[h] * k[h].astype(F32), axis=0, keepdims=True), n_ref.shape[1:])
        m_ref[h] = jnp.broadcast_to(m_new[h], m_ref.shape[1:])

    ms = [jnp.mean(hh[h] * hh[h], axis=-1, keepdims=True) for h in H]
    for h in H:
        hm = hh[h] * lax.rsqrt(ms[h] + RMS_EPS) * hn_ref[:, h * dv:(h + 1) * dv]
        og = o_ref[0, :, h * dv:(h + 1) * dv].astype(F32)
        out_ref[0, :, h * dv:(h + 1) * dv] = (_sigmoid(og) * hm).astype(out_ref.dtype)


def _mlstm(main3, small3, gate_bias, head_norm, cast_arrays=()):
    b, s, _ = main3.shape
    L = min(MLSTM_CHUNK, s)
    assert L == MLSTM_CHUNK and s % L == 0
    qw, vw = MLSTM_QK_WIDTH, MLSTM_WIDTH
    return _call_with_casts(
        _mlstm_kernel, n_in=7, cast_arrays=cast_arrays,
        out_shape=[jax.ShapeDtypeStruct((b, s, MLSTM_WIDTH), BF16)],
        grid=(b, s // L),
        in_specs=[pl.BlockSpec((1, L, qw), lambda i, c: (i, c, 0)),
                  pl.BlockSpec((1, L, qw), lambda i, c: (i, c, 1)),
                  pl.BlockSpec((1, L, vw), lambda i, c: (i, c, 1)),
                  pl.BlockSpec((1, L, vw), lambda i, c: (i, c, 2)),
                  pl.BlockSpec((1, L, IF_W), lambda i, c: (i, c, COL_IF // IF_W)),
                  pl.BlockSpec((1, IF_W), lambda i, c: (0, 0)),
                  pl.BlockSpec((1, vw), lambda i, c: (0, 0))],
        out_specs=[pl.BlockSpec((1, L, vw), lambda i, c: (i, c, 0))],
        scratch_shapes=[pltpu.VMEM((MLSTM_HEADS, MLSTM_QK_DIM, MLSTM_V_DIM), F32),
                        pltpu.VMEM((MLSTM_HEADS, 8, MLSTM_QK_DIM), F32),
                        pltpu.VMEM((MLSTM_HEADS, 8, 128), F32)],
        compiler_params=_params(("parallel", "arbitrary")),
        name="mlstm",
    )(main3, main3, main3, main3, small3, gate_bias, head_norm)


PV_MU_R, PV_MU_K, PV_MU_V, PV_W0, PV_A0, PV_KK, PV_KA, PV_RK, PV_LNG, PV_LNB = range(10)
PV_ROWS = 16


def _rwkv_kernel(r_ref, k_ref, v_ref, g_ref, wa_ref, pv_ref, mug_ref, muwa_ref, wup_ref, aup_ref, gup_ref, seg_ref,
                 out_ref, s_ref, pr_ref, pk_ref, pvv_ref, pg_ref, pwa_ref):
    L = RWKV_CHUNK
    CH = RWKV_CHUNKS_PER_STEP
    R = CH * L
    N = RWKV_HEAD
    GW = RWKV_GROUP_W
    GH = GW // N

    @pl.when(pl.program_id(1) == 0)
    def _():
        s_ref[...] = jnp.zeros_like(s_ref)
        pr_ref[...] = jnp.zeros_like(pr_ref)
        pk_ref[...] = jnp.zeros_like(pk_ref)
        pvv_ref[...] = jnp.zeros_like(pvv_ref)
        pg_ref[...] = jnp.zeros_like(pg_ref)
        pwa_ref[...] = jnp.zeros_like(pwa_ref)

    def shift_lerp(x_ref, prev_ref, mu):
        x = x_ref[0].astype(F32)
        rid = lax.broadcasted_iota(jnp.int32, x.shape, 0)
        xs = jnp.where(rid == 0, prev_ref[0:1, :], pltpu.roll(x, 1, 0))
        prev_ref[0:1, :] = x[R - 1:R, :]
        return x + (xs - x) * mu

    pv = pv_ref[...]
    row = lambda i: pv[i:i + 1, :]
    r = shift_lerp(r_ref, pr_ref, row(PV_MU_R))
    kr = shift_lerp(k_ref, pk_ref, row(PV_MU_K))
    v = shift_lerp(v_ref, pvv_ref, row(PV_MU_V))
    gl = shift_lerp(g_ref, pg_ref, mug_ref[...])
    wa = shift_lerp(wa_ref, pwa_ref, muwa_ref[...])

    wl = jnp.tanh(wa[:, 0:RWKV_DECAY_RANK])
    al = wa[:, RWKV_DECAY_RANK:WA_W]
    w_log = -_softplus(-(row(PV_W0) + jnp.dot(_bf(wl), wup_ref[...], preferred_element_type=F32))) - 0.5
    lw = -jnp.exp(w_log)
    a = _sigmoid(row(PV_A0) + jnp.dot(_bf(al), aup_ref[...], preferred_element_type=F32))
    gg = jnp.dot(_bf(_sigmoid(gl)), gup_ref[...], preferred_element_type=F32)

    tr = lax.broadcasted_iota(jnp.int32, (R, R), 0)
    tc = lax.broadcasted_iota(jnp.int32, (R, R), 1)
    tri = (tr >= tc) & ((tr // L) == (tc // L))
    cs = _mm_exact_lhs(tri.astype(F32), lw)
    g_incl = jnp.exp(cs)
    g_excl = jnp.exp(cs - lw)
    g_inv = jnp.exp(-cs)

    kk0 = kr * row(PV_KK)
    kr2 = kr * (1.0 + (a - 1.0) * row(PV_KA))
    rt_all = r * g_incl
    kt_all = kr2 * g_inv
    bonus_all = r * kr2 * row(PV_RK)

    seg = seg_ref[...]
    ti = lax.broadcasted_iota(jnp.int32, (L, GW), 0)
    lane = lax.broadcasted_iota(jnp.int32, (L, GW), 1)
    si = lane & (N - 1)
    lane_head = lane // N
    incl = ti >= si
    strict = ti > si
    eye = (ti == si).astype(F32)
    lvl_masks = []
    bsz = 1
    while bsz < L:
        same = (ti // (2 * bsz)) == (si // (2 * bsz))
        lvl_masks.append(jnp.where(same & ((ti & bsz) != 0) & ((si & bsz) == 0), 1.0, 0.0))
        bsz *= 2

    def bd(x):
        return jnp.concatenate([_bf(x)] * GH, axis=0) * seg

    def fold(z):
        acc = z[0:N]
        for hh in range(1, GH):
            acc = jnp.where(lane_head == hh, z[hh * N:(hh + 1) * N], acc)
        return acc

    def mmb(x, b):
        return jnp.dot(_bf(x), b, preferred_element_type=F32)

    def mmb_nt(x, b):
        return lax.dot_general(_bf(x), b, (((1,), (1,)), ((), ())), preferred_element_type=F32)

    cat = lambda u, w_: jnp.concatenate([u, w_], axis=0)
    groups = range(RWKV_GROUPS)
    E = [(c, g) for c in range(CH) for g in groups]
    rows = lambda c: slice(c * L, (c + 1) * L)
    lanes = lambda g: slice(g * GW, (g + 1) * GW)
    blk = lambda x: [x[rows(c), lanes(g)] for c, g in E]
    n_e = range(len(E))

    kk0_e, bonus_src, gex_e, a_e, ginv_e = blk(kk0), blk(bonus_all), blk(g_excl), blk(a), blk(g_inv)
    kt, rt, vv = blk(kt_all), blk(rt_all), blk(v)
    gl_ = [g_incl[c * L + L - 1:c * L + L, lanes(g)] for c, g in E]
    sums0 = [mmb(cat(kk0_e[e] * kk0_e[e], bonus_src[e]), seg) for e in n_e]
    kk = [kk0_e[e] * lax.rsqrt(jnp.maximum(sums0[e][:L], 1e-24)) for e in n_e]
    bonus = [sums0[e][L:] for e in n_e]
    p = [gex_e[e] * kk[e] for e in n_e]
    qt = [kk[e] * a_e[e] * ginv_e[e] for e in n_e]
    qg = [qt[e] * gl_[e] for e in n_e]
    kg = [kt[e] * gl_[e] for e in n_e]

    lhs_pr = [cat(p[e], rt[e]) for e in n_e]
    gq = [mmb_nt(lhs_pr[e], bd(qt[e])) for e in n_e]
    gk = [mmb_nt(lhs_pr[e], bd(kt[e])) for e in n_e]
    n_pq = [jnp.where(strict, gq[e][:L], 0.0) for e in n_e]
    a_rq = [jnp.where(incl, gq[e][L:], 0.0) for e in n_e]
    a_pk = [jnp.where(strict, gk[e][:L], 0.0) for e in n_e]
    a_rk = [jnp.where(incl, gk[e][L:], 0.0) for e in n_e]

    x = [eye - n_pq[e] * lvl_masks[0] for e in n_e]
    bd_n = [bd(n_pq[e]) for e in n_e]
    for msk in lvl_masks[1:]:
        t1 = [mmb(x[e], bd_n[e]) for e in n_e]
        x = [x[e] - msk * mmb(t1[e], bd(x[e])) for e in n_e]

    av = [mmb(cat(a_pk[e], a_rk[e]), bd(vv[e])) for e in n_e]
    w = [mmb(x[e], bd(p[e])) for e in n_e]
    u0 = [mmb(x[e], bd(av[e][:L])) for e in n_e]
    gmat = [rt[e] - mmb(a_rq[e], bd(w[e])) for e in n_e]
    y0 = [av[e][L:] - mmb(a_rq[e], bd(u0[e])) for e in n_e]
    mt = [eye * gl_[e] - fold(_mm_tn(qg[e], w[e])) for e in n_e]
    bt = [fold(_mm_tn(cat(kg[e], -qg[e]), cat(vv[e], u0[e]))) for e in n_e]

    state = [s_ref[g] for g in groups]
    y = [None] * len(E)
    for c in range(CH):
        es = [c * RWKV_GROUPS + g for g in groups]
        ys = [mmb(cat(gmat[e], mt[e]), bd(state[g])) for g, e in zip(groups, es)]
        for g, e in zip(groups, es):
            y[e] = ys[g][:L] + y0[e]
            state[g] = ys[g][L:] + bt[e]
    for g in groups:
        s_ref[g] = state[g]

    inv_n = 1.0 / N
    sums1 = [mmb(cat(y[e], y[e] * y[e]), seg) for e in n_e]
    for e, (c, g) in enumerate(E):
        mean = sums1[e][:L] * inv_n
        var = sums1[e][L:] * inv_n - mean * mean
        yn = (y[e] - mean) * lax.rsqrt(var + RWKV_GN_EPS) * row(PV_LNG)[:, lanes(g)] + row(PV_LNB)[:, lanes(g)]
        yn = yn + bonus[e] * vv[e]
        out_ref[0, rows(c), lanes(g)] = (yn * gg[rows(c), lanes(g)]).astype(out_ref.dtype)


def _rwkv(main3, small3, pvec, mu_g, mu_wa, w_up, a_up, g_up, cast_arrays=()):
    b, s, _ = main3.shape
    L = RWKV_CHUNK * RWKV_CHUNKS_PER_STEP
    rw = RWKV_WIDTH
    head_of = jnp.arange(RWKV_GROUP_W) // RWKV_HEAD
    seg = (head_of[:, None] == head_of[None, :]).astype(BF16)
    const = lambda shape: pl.BlockSpec(shape, lambda i, c: (0, 0))
    return _call_with_casts(
        _rwkv_kernel, n_in=12, cast_arrays=cast_arrays,
        out_shape=[jax.ShapeDtypeStruct((b, s, rw), BF16)],
        grid=(b, s // L),
        in_specs=[pl.BlockSpec((1, L, rw), lambda i, c: (i, c, 0)),
                  pl.BlockSpec((1, L, rw), lambda i, c: (i, c, 1)),
                  pl.BlockSpec((1, L, rw), lambda i, c: (i, c, 2)),
                  pl.BlockSpec((1, L, RG_PAD), lambda i, c: (i, c, COL_RG // RG_PAD)),
                  pl.BlockSpec((1, L, WA_W), lambda i, c: (i, c, COL_WA // WA_W)),
                  const((PV_ROWS, rw)), const((1, RG_PAD)), const((1, WA_W)),
                  const((RWKV_DECAY_RANK, rw)), const((RWKV_A_RANK, rw)), const((RG_PAD, rw)),
                  const((RWKV_GROUP_W, RWKV_GROUP_W))],
        out_specs=[pl.BlockSpec((1, L, rw), lambda i, c: (i, c, 0))],
        scratch_shapes=[pltpu.VMEM((RWKV_GROUPS, RWKV_HEAD, RWKV_GROUP_W), F32),
                        pltpu.VMEM((8, rw), F32), pltpu.VMEM((8, rw), F32), pltpu.VMEM((8, rw), F32),
                        pltpu.VMEM((8, RG_PAD), F32), pltpu.VMEM((8, WA_W), F32)],
        compiler_params=_params(("parallel", "arbitrary")),
        name="rwkv7",
    )(main3, main3, main3, small3, small3, pvec, mu_g, mu_wa, w_up, a_up, g_up, seg)


def _mix_out_kernel(ha_ref, hb_ref, ga_ref, gb_ref, wa_ref, wb_ref, wo_ref, g_ref, x_ref, o_ref):
    ya = jnp.dot(ha_ref[...], wa_ref[...], preferred_element_type=F32)
    yb = jnp.dot(hb_ref[...], wb_ref[...], preferred_element_type=F32)
    merged = _bf(_sigmoid(ga_ref[...].astype(F32)) * ya + _sigmoid(gb_ref[...].astype(F32)) * yb)
    y = jnp.dot(merged, wo_ref[...], preferred_element_type=F32)
    o_ref[...] = x_ref[...] + _rms(y, g_ref[...])


def _mix_out(ha, hb, gates, wa, wb, wo, g, x, *, tm):
    t, kdim = ha.shape
    d = wo.shape[1]
    resident = lambda shape: pl.BlockSpec(shape, lambda i: (0, 0), pipeline_mode=pl.Buffered(1))
    return pl.pallas_call(
        _mix_out_kernel,
        out_shape=jax.ShapeDtypeStruct((t, d), F32),
        grid=(t // tm,),
        in_specs=[pl.BlockSpec((tm, kdim), lambda i: (i, 0)),
                  pl.BlockSpec((tm, kdim), lambda i: (i, 0)),
                  pl.BlockSpec((tm, d), lambda i: (i, 0)),
                  pl.BlockSpec((tm, d), lambda i: (i, 1)),
                  resident((kdim, d)), resident((kdim, d)), resident((d, d)),
                  pl.BlockSpec((1, d), lambda i: (0, 0)),
                  pl.BlockSpec((tm, d), lambda i: (i, 0))],
        out_specs=pl.BlockSpec((tm, d), lambda i: (i, 0)),
        compiler_params=_params(("parallel",)),
        name="mix_out",
    )(ha, hb, gates, gates, wa, wb, wo, g, x)


def _xattn_kernel(x_ref, gpre_ref, wq_ref, kv_ref, wo_ref, gpost_ref, o_ref):
    x = x_ref[...]
    h = _bf(_rms(x, gpre_ref[...]))
    q = jnp.dot(h, wq_ref[...], preferred_element_type=F32)
    scale = XATTN_HEAD_DIM ** -0.5
    outs = []
    for hd in range(XATTN_HEADS):
        sl = slice(hd * XATTN_HEAD_DIM, (hd + 1) * XATTN_HEAD_DIM)
        k = kv_ref[0, :, sl]
        v = kv_ref[0, :, XATTN_WIDTH + hd * XATTN_HEAD_DIM:XATTN_WIDTH + (hd + 1) * XATTN_HEAD_DIM]
        sc = _mm_nt(q[:, sl], k) * scale
        sc = sc - jnp.max(sc, axis=-1, keepdims=True)
        e = jnp.exp(sc)
        p = e / jnp.sum(e, axis=-1, keepdims=True)
        outs.append(_mm(p, v))
    o = jnp.concatenate(outs, axis=-1)
    y = jnp.dot(_bf(o), wo_ref[...], preferred_element_type=F32)
    o_ref[...] = x + _rms(y, gpost_ref[...])


def _xattn(x, gpre, wq, kv3, wo, gpost, *, tm, seq):
    t, d = x.shape
    per_seq = seq // tm
    return pl.pallas_call(
        _xattn_kernel,
        out_shape=jax.ShapeDtypeStruct((t, d), F32),
        grid=(t // tm,),
        in_specs=[pl.BlockSpec((tm, d), lambda i: (i, 0)),
                  pl.BlockSpec((1, d), lambda i: (0, 0)),
                  pl.BlockSpec((d, XATTN_WIDTH), lambda i: (0, 0)),
                  pl.BlockSpec((1, MEM_LEN, 2 * XATTN_WIDTH), lambda i: (i // per_seq, 0, 0)),
                  pl.BlockSpec((XATTN_WIDTH, d), lambda i: (0, 0)),
                  pl.BlockSpec((1, d), lambda i: (0, 0))],
        out_specs=pl.BlockSpec((tm, d), lambda i: (i, 0)),
        compiler_params=_params(("parallel",)),
        name="xattn",
    )(x, gpre, wq, kv3, wo, gpost)


FFN_HALO = 8


def _gelu_tanh(x):
    return 0.5 * x * (1.0 + jnp.tanh(0.7978845608028654 * (x + 0.044715 * x * x * x)))


def _ffn_kernel(x_ref, halo_ref, gpre_ref, wg_ref, wu_ref, cwg_ref, cwu_ref, cbg_ref, cbu_ref, wd_ref, gpost_ref,
                o_ref, h_ref, *, tiles_per_seq):
    i = pl.program_id(0)
    j = pl.program_id(1)

    @pl.when(j == 0)
    def _():
        keep = jnp.where(i % tiles_per_seq == 0, 0.0, 1.0)
        h_ref[0:FFN_HALO, :] = _bf(_rms(halo_ref[...], gpre_ref[...]) * keep)
        h_ref[FFN_HALO:, :] = _bf(_rms(x_ref[...], gpre_ref[...]))
        o_ref[...] = jnp.zeros_like(o_ref)

    def conv(w_ref, cw_ref, cb_ref):
        u = jnp.dot(h_ref[...], w_ref[...], preferred_element_type=F32)
        u1 = pltpu.roll(u, 1, 0)
        u2 = pltpu.roll(u, 2, 0)
        cw = cw_ref[...]
        full = cb_ref[...] + cw[2:3, :] * u + cw[1:2, :] * u1 + cw[0:1, :] * u2
        return full[FFN_HALO:, :]

    gate = conv(wg_ref, cwg_ref, cbg_ref)
    up = conv(wu_ref, cwu_ref, cbu_ref)
    act = _bf(_gelu_tanh(gate) * up)
    o_ref[...] += jnp.dot(act, wd_ref[...], preferred_element_type=F32)

    @pl.when(j == pl.num_programs(1) - 1)
    def _():
        o_ref[...] = x_ref[...] + _rms(o_ref[...], gpost_ref[...])


def _ffn(x, gpre, w_up, conv_w, conv_b, w_down, gpost, *, tm, tn, seq):
    t, d = x.shape
    nj = D_FF // tn
    hb = tm // FFN_HALO
    return pl.pallas_call(
        functools.partial(_ffn_kernel, tiles_per_seq=seq // tm),
        out_shape=jax.ShapeDtypeStruct((t, d), F32),
        grid=(t // tm, nj),
        in_specs=[pl.BlockSpec((tm, d), lambda i, j: (i, 0)),
                  pl.BlockSpec((FFN_HALO, d), lambda i, j: (jnp.maximum(i * hb - 1, 0), 0)),
                  pl.BlockSpec((1, d), lambda i, j: (0, 0)),
                  pl.BlockSpec((d, tn), lambda i, j: (0, j)),
                  pl.BlockSpec((d, tn), lambda i, j: (0, nj + j)),
                  pl.BlockSpec((CONV_WIDTH, tn), lambda i, j: (0, j)),
                  pl.BlockSpec((CONV_WIDTH, tn), lambda i, j: (0, nj + j)),
                  pl.BlockSpec((1, tn), lambda i, j: (0, j)),
                  pl.BlockSpec((1, tn), lambda i, j: (0, nj + j)),
                  pl.BlockSpec((tn, d), lambda i, j: (j, 0)),
                  pl.BlockSpec((1, d), lambda i, j: (0, 0))],
        out_specs=pl.BlockSpec((tm, d), lambda i, j: (i, 0), pipeline_mode=pl.Buffered(1)),
        scratch_shapes=[pltpu.VMEM((tm + FFN_HALO, d), BF16)],
        compiler_params=_params(("parallel", "arbitrary")),
        name="conv_glu_ffn",
    )(x, x, gpre, w_up, w_up, conv_w, conv_w, conv_b, conv_b, w_down, gpost)


def _small_in_proj(w_in):
    d = w_in.shape[1]
    r0, g0 = MLSTM_TOTAL, MLSTM_TOTAL + RWKV_TOTAL
    mlstm_if = w_in[MLSTM_MAIN:r0]
    rwkv_wa = w_in[r0 + RWKV_MAIN:r0 + RWKV_MAIN + WA_W]
    rwkv_g = w_in[r0 + RWKV_MAIN + WA_W:g0]
    z = lambda n: jnp.zeros((n, d), w_in.dtype)
    return jnp.concatenate([rwkv_g, z(RG_PAD - RWKV_GATE_RANK), rwkv_wa, mlstm_if, z(IF_W - 2 * MLSTM_HEADS)], axis=0)


def _layer(x, mem, mix_pre_norm, w_in, mlstm_b_i, mlstm_b_f, mlstm_head_norm, rwkv_mu, rwkv_w0, rwkv_w_up, rwkv_a0,
           rwkv_a_up, rwkv_g_up, rwkv_k_k, rwkv_k_a, rwkv_r_k, rwkv_ln_g, rwkv_ln_b, w_branch_a, w_branch_b,
           w_mix_out, mix_post_norm, xattn_pre_norm, mem_norm, xattn_wq, xattn_wkv, xattn_wo, xattn_post_norm,
           ffn_pre_norm, ffn_w_up, ffn_conv_w, ffn_conv_b, ffn_w_down, ffn_post_norm):
    b, s, d = x.shape
    t = b * s
    x2d = x.reshape(t, d)
    row = lambda p: p.reshape(1, -1).astype(F32)
    tm_row = min(TM_ROW, s)

    w_in_t = w_in.T
    tm_in = min(TM_IN_PROJ, t)
    small, h = _norm_matmul(x2d, row(mix_pre_norm), _small_in_proj(w_in_t), tm=tm_in, tn=SMALL_COLS, out_dtype=F32,
                            name="in_proj_small", w_is_transposed=True)
    small3 = small.reshape(b, s, SMALL_COLS)
    wide = functools.partial(_rowrange_matmul, h, w_in_t, tm=tm_in, tn=TN_IN_PROJ, out_dtype=BF16)
    (mlstm_main,) = wide(row0=0, nrows=MLSTM_MAIN, name="in_proj_mlstm")
    (rwkv_main,) = wide(row0=MLSTM_TOTAL, nrows=RWKV_MAIN, name="in_proj_rwkv")
    gates, ffn_w_up_b = wide(row0=MLSTM_TOTAL + RWKV_TOTAL, nrows=2 * D_MODEL, name="in_proj_gates",
                             cast_arrays=(ffn_w_up,))

    gate_bias = jnp.concatenate([mlstm_b_i, mlstm_b_f, jnp.zeros((IF_W - 2 * MLSTM_HEADS,), F32)]).reshape(1, IF_W)
    ha, w_a_b, w_b_b, w_out_b = _mlstm(mlstm_main.reshape(b, s, MLSTM_MAIN), small3, gate_bias, row(mlstm_head_norm),
                                       cast_arrays=(w_branch_a, w_branch_b, w_mix_out))

    mu_r, mu_k, mu_v = (rwkv_mu[i * RWKV_WIDTH:(i + 1) * RWKV_WIDTH] for i in range(3))
    mu_wa = rwkv_mu[RWKV_MAIN:RWKV_MAIN + WA_W]
    mu_g = jnp.pad(rwkv_mu[RWKV_MAIN + WA_W:], (0, RG_PAD - RWKV_GATE_RANK))
    pvec = jnp.stack([mu_r, mu_k, mu_v, rwkv_w0, rwkv_a0, rwkv_k_k, rwkv_k_a, rwkv_r_k.reshape(-1), rwkv_ln_g,
                      rwkv_ln_b] + [jnp.zeros((RWKV_WIDTH,), F32)] * (PV_ROWS - 10))
    g_up = jnp.pad(rwkv_g_up, ((0, RG_PAD - RWKV_GATE_RANK), (0, 0)))
    hb, ffn_w_down_b = _rwkv(rwkv_main.reshape(b, s, RWKV_MAIN), small3, pvec, row(mu_g), row(mu_wa), _bf(rwkv_w_up),
                             _bf(rwkv_a_up), _bf(g_up), cast_arrays=(ffn_w_down,))

    x1 = _mix_out(ha.reshape(t, -1), hb.reshape(t, -1), gates, w_a_b, w_b_b, w_out_b, row(mix_post_norm), x2d,
                  tm=tm_row)

    mem2d = mem.reshape(b * MEM_LEN, d)
    kv, _ = _norm_matmul(mem2d, row(mem_norm), _bf(xattn_wkv), tm=MEM_LEN, tn=TN, out_dtype=F32, name="mem_kv")
    x2 = _xattn(x1, row(xattn_pre_norm), _bf(xattn_wq), kv.reshape(b, MEM_LEN, 2 * XATTN_WIDTH), _bf(xattn_wo),
                row(xattn_post_norm), tm=tm_row, seq=s)

    x3 = _ffn(x2, row(ffn_pre_norm), ffn_w_up_b, ffn_conv_w, row(ffn_conv_b), ffn_w_down_b, row(ffn_post_norm),
              tm=min(TM_FFN, s), tn=TN, seq=s)
    return x3.reshape(b, s, d)


def kernel(x, mem, mix_pre_norm, w_in, mlstm_b_i, mlstm_b_f, mlstm_head_norm, rwkv_mu, rwkv_w0, rwkv_w_up, rwkv_a0, rwkv_a_up, rwkv_g_up, rwkv_k_k, rwkv_k_a, rwkv_r_k, rwkv_ln_g, rwkv_ln_b, w_branch_a, w_branch_b, w_mix_out, mix_post_norm, xattn_pre_norm, mem_norm, xattn_wq, xattn_wkv, xattn_wo, xattn_post_norm, ffn_pre_norm, ffn_w_up, ffn_conv_w, ffn_conv_b, ffn_w_down, ffn_post_norm):
    for l in range(mix_pre_norm.shape[0]):
        x = _layer(x, mem, mix_pre_norm[l], w_in[l], mlstm_b_i[l], mlstm_b_f[l], mlstm_head_norm[l], rwkv_mu[l],
                   rwkv_w0[l], rwkv_w_up[l], rwkv_a0[l], rwkv_a_up[l], rwkv_g_up[l], rwkv_k_k[l], rwkv_k_a[l],
                   rwkv_r_k[l], rwkv_ln_g[l], rwkv_ln_b[l], w_branch_a[l], w_branch_b[l], w_mix_out[l],
                   mix_post_norm[l], xattn_pre_norm[l], mem_norm[l], xattn_wq[l], xattn_wkv[l], xattn_wo[l],
                   xattn_post_norm[l], ffn_pre_norm[l], ffn_w_up[l], ffn_conv_w[l], ffn_conv_b[l], ffn_w_down[l],
                   ffn_post_norm[l])
    return x
```

```python
import functools

import jax
import jax.numpy as jnp
from jax import lax
from jax.experimental import pallas as pl
from jax.experimental.pallas import tpu as pltpu

D_MODEL = 2048
MEM_LEN = 256
RMS_EPS = 1e-6

MLSTM_HEADS = 4
MLSTM_WIDTH = D_MODEL // 2
MLSTM_V_DIM = MLSTM_WIDTH // MLSTM_HEADS
MLSTM_QK_DIM = MLSTM_V_DIM // 2
MLSTM_QK_WIDTH = MLSTM_HEADS * MLSTM_QK_DIM
GATE_SOFTCAP = 15.0

RWKV_WIDTH = D_MODEL // 2
RWKV_HEAD = 64
RWKV_HEADS = RWKV_WIDTH // RWKV_HEAD
RWKV_DECAY_RANK = 64
RWKV_A_RANK = 64
RWKV_GATE_RANK = 160
RWKV_GN_EPS = 64e-5

MLSTM_MAIN = 2 * MLSTM_QK_WIDTH + 2 * MLSTM_WIDTH
MLSTM_TOTAL = MLSTM_MAIN + 2 * MLSTM_HEADS
RWKV_MAIN = 3 * RWKV_WIDTH
RWKV_TOTAL = RWKV_MAIN + RWKV_DECAY_RANK + RWKV_A_RANK + RWKV_GATE_RANK

XATTN_HEADS = 4
XATTN_HEAD_DIM = 128
XATTN_WIDTH = XATTN_HEADS * XATTN_HEAD_DIM

D_FF = 4 * D_MODEL
CONV_WIDTH = 3

RG_PAD = 256
WA_W = RWKV_DECAY_RANK + RWKV_A_RANK
IF_W = 128
COL_RG = 0
COL_WA = COL_RG + RG_PAD
COL_IF = COL_WA + WA_W
SMALL_COLS = COL_IF + IF_W

MLSTM_CHUNK = 256
RWKV_CHUNK = 64
RWKV_CHUNKS_PER_STEP = 2
RWKV_GROUP_W = 256
RWKV_GROUPS = RWKV_WIDTH // RWKV_GROUP_W
VMEM_LIMIT = 56 * 1024 * 1024

SUBLANES = 8
BF16_SUBLANES = 16
TM_IN_PROJ = 1024
TN_IN_PROJ = 1024
TM_ROW = 512
TM_FFN = 1024
TN = 512

F32 = jnp.float32
BF16 = jnp.bfloat16


def _bf(x):
    return x.astype(BF16)


def _mm(a, b):
    return jnp.dot(_bf(a), _bf(b), preferred_element_type=F32)


def _mm_nt(a, b):
    return lax.dot_general(_bf(a), _bf(b), (((1,), (1,)), ((), ())), preferred_element_type=F32)


def _mm_tn(a, b):
    return lax.dot_general(_bf(a), _bf(b), (((0,), (0,)), ((), ())), preferred_element_type=F32)


def _mm_exact_lhs(tri, x):
    hi = _bf(x)
    r1 = x - hi.astype(F32)
    mid = _bf(r1)
    lo = _bf(r1 - mid.astype(F32))
    t = _bf(tri)
    return (jnp.dot(t, hi, preferred_element_type=F32) + jnp.dot(t, mid, preferred_element_type=F32)
            + jnp.dot(t, lo, preferred_element_type=F32))


def _rms(x, g):
    return x * lax.rsqrt(jnp.mean(x * x, axis=-1, keepdims=True) + RMS_EPS) * g


def _softplus(x):
    return jnp.maximum(x, 0.0) + jnp.log(1.0 + jnp.exp(-jnp.abs(x)))


def _sigmoid(x):
    return 1.0 / (1.0 + jnp.exp(-x))


def _params(sem, flags=None):
    return pltpu.CompilerParams(dimension_semantics=sem, vmem_limit_bytes=VMEM_LIMIT, flags=flags)


def _call_with_casts(body, *, n_in, out_shape, grid, in_specs, out_specs, cast_arrays=(), **kwargs):
    n_out = len(out_shape)
    k = len(cast_arrays)
    nsteps = 1
    for extent in grid:
        nsteps *= extent

    def step(*ids):
        lin = ids[0]
        for extent, idx in zip(grid[1:], ids[1:]):
            lin = lin * extent + idx
        return lin

    band_specs, band_shapes = [], []
    for arr in cast_arrays:
        rows, cols = arr.shape
        assert rows % nsteps == 0 and (rows // nsteps) % BF16_SUBLANES == 0, (arr.shape, nsteps)
        band_specs.append(pl.BlockSpec((rows // nsteps, cols), lambda *ids: (step(*ids), 0)))
        band_shapes.append(jax.ShapeDtypeStruct(arr.shape, BF16))

    def kernel(*refs):
        ins, src = refs[:n_in], refs[n_in:n_in + k]
        outs, dst = refs[n_in + k:n_in + k + n_out], refs[n_in + k + n_out:n_in + 2 * k + n_out]
        for s_ref, d_ref in zip(src, dst):
            d_ref[...] = s_ref[...].astype(d_ref.dtype)
        body(*ins, *outs, *refs[n_in + 2 * k + n_out:])

    def run(*operands):
        assert len(operands) == n_in
        return pl.pallas_call(kernel, out_shape=tuple(out_shape) + tuple(band_shapes), grid=grid,
                              in_specs=list(in_specs) + band_specs, out_specs=tuple(out_specs) + tuple(band_specs),
                              **kwargs)(*operands, *cast_arrays)

    return run


def _norm_matmul_kernel(x_ref, g_ref, w_ref, o_ref, h_ref, *, w_is_transposed):
    @pl.when(pl.program_id(1) == 0)
    def _():
        h_ref[...] = _bf(_rms(x_ref[...], g_ref[...]))

    w = w_ref[...].T if w_is_transposed else w_ref[...]
    o_ref[...] = jnp.dot(h_ref[...], _bf(w), preferred_element_type=F32).astype(o_ref.dtype)


def _norm_matmul(x, g, w, *, tm, tn, out_dtype, name, w_is_transposed=False):
    t, d = x.shape
    n = w.shape[0] if w_is_transposed else w.shape[1]
    w_spec = pl.BlockSpec((tn, d), lambda i, j: (j, 0)) if w_is_transposed else pl.BlockSpec((d, tn), lambda i, j: (0, j))
    return pl.pallas_call(
        functools.partial(_norm_matmul_kernel, w_is_transposed=w_is_transposed),
        out_shape=(jax.ShapeDtypeStruct((t, n), out_dtype), jax.ShapeDtypeStruct((t, d), BF16)),
        grid=(t // tm, n // tn),
        in_specs=[pl.BlockSpec((tm, d), lambda i, j: (i, 0)),
                  pl.BlockSpec((1, d), lambda i, j: (0, 0)),
                  w_spec],
        out_specs=(pl.BlockSpec((tm, tn), lambda i, j: (i, j)),
                   pl.BlockSpec((tm, d), lambda i, j: (i, 0))),
        compiler_params=_params(("parallel", "arbitrary")),
        name=name,
    )(x, g, w)


def _rowrange_matmul_kernel(a_ref, wt_ref, o_ref, wb_ref):
    @pl.when(pl.program_id(1) == 0)
    def _():
        wb_ref[...] = _bf(wt_ref[...].T)

    o_ref[...] = jnp.dot(a_ref[...], wb_ref[...], preferred_element_type=F32).astype(o_ref.dtype)


def _rowrange_matmul(a, w_t, *, row0, nrows, tm, tn, out_dtype, name, cast_arrays=()):
    t, d = a.shape
    assert row0 % SUBLANES == 0 and nrows % tn == 0
    return _call_with_casts(
        _rowrange_matmul_kernel, n_in=2,
        out_shape=[jax.ShapeDtypeStruct((t, nrows), out_dtype)],
        grid=(nrows // tn, t // tm),
        in_specs=[pl.BlockSpec((tm, d), lambda j, i: (i, 0)),
                  pl.BlockSpec((pl.Element(tn), pl.Element(d)),
                               lambda j, i: (pl.multiple_of(row0 + tn * j, SUBLANES), 0))],
        out_specs=[pl.BlockSpec((tm, tn), lambda j, i: (i, j))],
        cast_arrays=cast_arrays,
        scratch_shapes=[pltpu.VMEM((d, tn), BF16)],
        compiler_params=_params(("parallel", "arbitrary")),
        name=name,
    )(a, w_t)


def _mlstm_kernel(q_ref, k_ref, v_ref, o_ref, gate_ref, bias_ref, hn_ref, out_ref, ct_ref, n_ref, m_ref):
    L = MLSTM_CHUNK
    dk, dv = MLSTM_QK_DIM, MLSTM_V_DIM

    @pl.when(pl.program_id(1) == 0)
    def _():
        ct_ref[...] = jnp.zeros_like(ct_ref)
        n_ref[...] = jnp.zeros_like(n_ref)
        m_ref[...] = jnp.zeros_like(m_ref)

    pre = gate_ref[0] + bias_ref[...]
    capped = GATE_SOFTCAP * jnp.tanh(pre / GATE_SOFTCAP)
    logf = -_softplus(-capped)
    row = lax.broadcasted_iota(jnp.int32, (L, L), 0)
    col = lax.broadcasted_iota(jnp.int32, (L, L), 1)
    causal = row >= col
    bcum = _mm_exact_lhs(causal.astype(F32), logf)
    ig_t = capped.T
    bcum_t = bcum.T
    scale = MLSTM_QK_DIM ** -0.5

    H = range(MLSTM_HEADS)
    q = [q_ref[0, :, h * dk:(h + 1) * dk] for h in H]
    k = [k_ref[0, :, h * dk:(h + 1) * dk] for h in H]
    v = [v_ref[0, :, h * dv:(h + 1) * dv] for h in H]
    b_c = [bcum[:, MLSTM_HEADS + h:MLSTM_HEADS + h + 1] for h in H]
    b_r = [bcum_t[MLSTM_HEADS + h:MLSTM_HEADS + h + 1, :] for h in H]
    i_c = [capped[:, h:h + 1] for h in H]
    i_r = [ig_t[h:h + 1, :] for h in H]
    m_prev = [m_ref[h][0:1, 0:1] for h in H]
    ct = [ct_ref[h] for h in H]
    nrow = [n_ref[h][0:1, :] for h in H]

    dmat = [jnp.where(causal, b_c[h] - b_r[h] + i_r[h], -jnp.inf) for h in H]
    inter = [b_c[h] + m_prev[h] for h in H]
    m_t = [jnp.maximum(inter[h], jnp.max(dmat[h], axis=-1, keepdims=True)) for h in H]
    dexp = [jnp.exp(dmat[h] - m_t[h]) * scale for h in H]
    w_inter = [jnp.exp(inter[h] - m_t[h]) for h in H]
    qk = [_mm_nt(q[h], k[h]) for h in H]
    qc = [_mm(q[h], ct[h]) for h in H]
    s = [qk[h] * dexp[h] for h in H]
    sv = [_mm(s[h], v[h]) for h in H]
    num = [w_inter[h] * qc[h] + sv[h] for h in H]
    den = [w_inter[h] * jnp.sum(q[h].astype(F32) * nrow[h], axis=-1, keepdims=True)
           + jnp.sum(s[h], axis=-1, keepdims=True) for h in H]
    hh = [num[h] / jnp.maximum(jnp.abs(den[h]), jnp.exp(-m_t[h])) for h in H]

    b_last = [b_c[h][L - 1:L, :] for h in H]
    gs = [b_last[h] - b_c[h] + i_c[h] for h in H]
    m_new = [jnp.maximum(b_last[h] + m_prev[h], jnp.max(gs[h], axis=0, keepdims=True)) for h in H]
    carry_w = [jnp.exp(b_last[h] + m_prev[h] - m_new[h]) for h in H]
    ws = [jnp.exp(gs[h] - m_new[h]) * scale for h in H]
    kv = [_mm_tn(k[h], ws[h] * v[h].astype(F32)) for h in H]
    for h in H:
        ct_ref[h] = carry_w[h] * ct[h] + kv[h]
        n_ref[h] = jnp.broadcast_to(
            carry_w[h] * nrow[h] + jnp.sum(ws[h] * k[h].astype(F32), axis=0, keepdims=True), n_ref.shape[1:])
        m_ref[h] = jnp.broadcast_to(m_new[h], m_ref.shape[1:])

    ms = [jnp.mean(hh[h] * hh[h], axis=-1, keepdims=True) for h in H]
    for h in H:
        hm = hh[h] * lax.rsqrt(ms[h] + RMS_EPS) * hn_ref[:, h * dv:(h + 1) * dv]
        og = o_ref[0, :, h * dv:(h + 1) * dv].astype(F32)
        out_ref[0, :, h * dv:(h + 1) * dv] = (_sigmoid(og) * hm).astype(out_ref.dtype)


def _mlstm(main3, small3, gate_bias, head_norm, cast_arrays=()):
    b, s, _ = main3.shape
    L = min(MLSTM_CHUNK, s)
    assert L == MLSTM_CHUNK and s % L == 0
    qw, vw = MLSTM_QK_WIDTH, MLSTM_WIDTH
    return _call_with_casts(
        _mlstm_kernel, n_in=7, cast_arrays=cast_arrays,
        out_shape=[jax.ShapeDtypeStruct((b, s, MLSTM_WIDTH), BF16)],
        grid=(b, s // L),
        in_specs=[pl.BlockSpec((1, L, qw), lambda i, c: (i, c, 0)),
                  pl.BlockSpec((1, L, qw), lambda i, c: (i, c, 1)),
                  pl.BlockSpec((1, L, vw), lambda i, c: (i, c, 1)),
                  pl.BlockSpec((1, L, vw), lambda i, c: (i, c, 2)),
                  pl.BlockSpec((1, L, IF_W), lambda i, c: (i, c, COL_IF // IF_W)),
                  pl.BlockSpec((1, IF_W), lambda i, c: (0, 0)),
                  pl.BlockSpec((1, vw), lambda i, c: (0, 0))],
        out_specs=[pl.BlockSpec((1, L, vw), lambda i, c: (i, c, 0))],
        scratch_shapes=[pltpu.VMEM((MLSTM_HEADS, MLSTM_QK_DIM, MLSTM_V_DIM), F32),
                        pltpu.VMEM((MLSTM_HEADS, 8, MLSTM_QK_DIM), F32),
                        pltpu.VMEM((MLSTM_HEADS, 8, 128), F32)],
        compiler_params=_params(("parallel", "arbitrary")),
        name="mlstm",
    )(main3, main3, main3, main3, small3, gate_bias, head_norm)


PV_MU_R, PV_MU_K, PV_MU_V, PV_W0, PV_A0, PV_KK, PV_KA, PV_RK, PV_LNG, PV_LNB = range(10)
PV_ROWS = 16


def _rwkv_kernel(r_ref, k_ref, v_ref, g_ref, wa_ref, pv_ref, mug_ref, muwa_ref, wup_ref, aup_ref, gup_ref, seg_ref,
                 out_ref, s_ref, pr_ref, pk_ref, pvv_ref, pg_ref, pwa_ref):
    L = RWKV_CHUNK
    CH = RWKV_CHUNKS_PER_STEP
    R = CH * L
    N = RWKV_HEAD
    GW = RWKV_GROUP_W
    GH = GW // N

    @pl.when(pl.program_id(1) == 0)
    def _():
        s_ref[...] = jnp.zeros_like(s_ref)
        pr_ref[...] = jnp.zeros_like(pr_ref)
        pk_ref[...] = jnp.zeros_like(pk_ref)
        pvv_ref[...] = jnp.zeros_like(pvv_ref)
        pg_ref[...] = jnp.zeros_like(pg_ref)
        pwa_ref[...] = jnp.zeros_like(pwa_ref)

    groups = range(RWKV_GROUPS)
    rows = lambda c: slice(c * L, (c + 1) * L)
    lanes = lambda g: slice(g * GW, (g + 1) * GW)
    cat = lambda u, w_: jnp.concatenate([u, w_], axis=0)
    seg = seg_ref[...]
    pv = pv_ref[...]

    def shift_lerp(x_ref, prev_ref, mu, ls=slice(None)):
        x = x_ref[0, :, ls].astype(F32)
        rid = lax.broadcasted_iota(jnp.int32, x.shape, 0)
        xs = jnp.where(rid == 0, prev_ref[0:1, ls], pltpu.roll(x, 1, 0))
        prev_ref[0:1, ls] = x[R - 1:R, :]
        return x + (xs - x) * mu

    gl = shift_lerp(g_ref, pg_ref, mug_ref[...])
    wa = shift_lerp(wa_ref, pwa_ref, muwa_ref[...])
    wl_b = _bf(jnp.tanh(wa[:, 0:RWKV_DECAY_RANK]))
    al_b = _bf(wa[:, RWKV_DECAY_RANK:WA_W])
    sgl_b = _bf(_sigmoid(gl))
    tr = lax.broadcasted_iota(jnp.int32, (R, R), 0)
    tc = lax.broadcasted_iota(jnp.int32, (R, R), 1)
    tri = ((tr >= tc) & ((tr // L) == (tc // L))).astype(F32)

    ti = lax.broadcasted_iota(jnp.int32, (L, GW), 0)
    lane = lax.broadcasted_iota(jnp.int32, (L, GW), 1)
    si = lane & (N - 1)
    lane_head = lane // N
    incl = ti >= si
    strict = ti > si
    eye = (ti == si).astype(F32)
    lvl_masks = []
    bsz = 1
    while bsz < L:
        same = (ti // (2 * bsz)) == (si // (2 * bsz))
        lvl_masks.append(jnp.where(same & ((ti & bsz) != 0) & ((si & bsz) == 0), 1.0, 0.0))
        bsz *= 2

    def bd(x):
        return jnp.concatenate([_bf(x)] * GH, axis=0) * seg

    def fold(z):
        acc = z[0:N]
        for hh in range(1, GH):
            acc = jnp.where(lane_head == hh, z[hh * N:(hh + 1) * N], acc)
        return acc

    def mmb(x, b):
        return jnp.dot(_bf(x), b, preferred_element_type=F32)

    def mmb_nt(x, b):
        return lax.dot_general(_bf(x), b, (((1,), (1,)), ((), ())), preferred_element_type=F32)

    prep = []
    for g in groups:
        ls = lanes(g)
        row = lambda i, ls=ls: pv[i:i + 1, ls]
        r = shift_lerp(r_ref, pr_ref, row(PV_MU_R), ls)
        kr = shift_lerp(k_ref, pk_ref, row(PV_MU_K), ls)
        v = shift_lerp(v_ref, pvv_ref, row(PV_MU_V), ls)
        w_log = -_softplus(-(row(PV_W0) + jnp.dot(wl_b, wup_ref[:, ls], preferred_element_type=F32))) - 0.5
        lw = -jnp.exp(w_log)
        a = _sigmoid(row(PV_A0) + jnp.dot(al_b, aup_ref[:, ls], preferred_element_type=F32))
        gate = jnp.dot(sgl_b, gup_ref[:, ls], preferred_element_type=F32)
        cs = _mm_exact_lhs(tri, lw)
        g_incl = jnp.exp(cs)
        g_inv = jnp.exp(-cs)
        kk0 = kr * row(PV_KK)
        kr2 = kr * (1.0 + (a - 1.0) * row(PV_KA))
        sums = jnp.dot(_bf(cat(kk0 * kk0, r * kr2 * row(PV_RK))), seg, preferred_element_type=F32)
        kk = kk0 * lax.rsqrt(jnp.maximum(sums[:R], 1e-24))
        prep.append(dict(p=jnp.exp(cs - lw) * kk, qt=kk * a * g_inv, kt=kr2 * g_inv, rt=r * g_incl, v=v,
                         bonus_v=sums[R:] * v, gate=gate, g_incl=g_incl,
                         ln_g=row(PV_LNG), ln_b=row(PV_LNB)))

    E = [(c, g) for c in range(CH) for g in groups]
    n_e = range(len(E))
    blk = lambda name: [prep[g][name][rows(c), :] for c, g in E]
    p, qt, kt, rt, vv, bonus_v, gate = (blk(n_) for n_ in ("p", "qt", "kt", "rt", "v", "bonus_v", "gate"))
    gl_ = [prep[g]["g_incl"][c * L + L - 1:c * L + L, :] for c, g in E]
    qg = [qt[e] * gl_[e] for e in n_e]
    kg = [kt[e] * gl_[e] for e in n_e]

    lhs_pr = [cat(p[e], rt[e]) for e in n_e]
    gq = [mmb_nt(lhs_pr[e], bd(qt[e])) for e in n_e]
    gk = [mmb_nt(lhs_pr[e], bd(kt[e])) for e in n_e]
    n_pq = [jnp.where(strict, gq[e][:L], 0.0) for e in n_e]
    a_rq = [jnp.where(incl, gq[e][L:], 0.0) for e in n_e]
    a_pk = [jnp.where(strict, gk[e][:L], 0.0) for e in n_e]
    a_rk = [jnp.where(incl, gk[e][L:], 0.0) for e in n_e]

    x = [eye - n_pq[e] * lvl_masks[0] for e in n_e]
    bd_n = [bd(n_pq[e]) for e in n_e]
    for msk in lvl_masks[1:]:
        t1 = [mmb(x[e], bd_n[e]) for e in n_e]
        x = [x[e] - msk * mmb(t1[e], bd(x[e])) for e in n_e]

    av = [mmb(cat(a_pk[e], a_rk[e]), bd(vv[e])) for e in n_e]
    w = [mmb(x[e], bd(p[e])) for e in n_e]
    u0 = [mmb(x[e], bd(av[e][:L])) for e in n_e]
    gmat = [rt[e] - mmb(a_rq[e], bd(w[e])) for e in n_e]
    y0 = [av[e][L:] - mmb(a_rq[e], bd(u0[e])) for e in n_e]
    mt = [eye * gl_[e] - fold(_mm_tn(qg[e], w[e])) for e in n_e]
    bt = [fold(_mm_tn(cat(kg[e], -qg[e]), cat(vv[e], u0[e]))) for e in n_e]

    state = [s_ref[g] for g in groups]
    y = [None] * len(E)
    for c in range(CH):
        es = [c * RWKV_GROUPS + g for g in groups]
        ys = [mmb(cat(gmat[e], mt[e]), bd(state[g])) for g, e in zip(groups, es)]
        for g, e in zip(groups, es):
            y[e] = ys[g][:L] + y0[e]
            state[g] = ys[g][L:] + bt[e]
    for g in groups:
        s_ref[g] = state[g]

    inv_n = 1.0 / N
    sums1 = [mmb(cat(y[e], y[e] * y[e]), seg) for e in n_e]
    for e, (c, g) in enumerate(E):
        mean = sums1[e][:L] * inv_n
        var = sums1[e][L:] * inv_n - mean * mean
        yn = (y[e] - mean) * lax.rsqrt(var + RWKV_GN_EPS) * prep[g]["ln_g"] + prep[g]["ln_b"]
        out_ref[0, rows(c), lanes(g)] = ((yn + bonus_v[e]) * gate[e]).astype(out_ref.dtype)


def _rwkv(main3, small3, pvec, mu_g, mu_wa, w_up, a_up, g_up, cast_arrays=()):
    b, s, _ = main3.shape
    L = RWKV_CHUNK * RWKV_CHUNKS_PER_STEP
    rw = RWKV_WIDTH
    head_of = jnp.arange(RWKV_GROUP_W) // RWKV_HEAD
    seg = (head_of[:, None] == head_of[None, :]).astype(BF16)
    const = lambda shape: pl.BlockSpec(shape, lambda i, c: (0, 0))
    return _call_with_casts(
        _rwkv_kernel, n_in=12, cast_arrays=cast_arrays,
        out_shape=[jax.ShapeDtypeStruct((b, s, rw), BF16)],
        grid=(b, s // L),
        in_specs=[pl.BlockSpec((1, L, rw), lambda i, c: (i, c, 0)),
                  pl.BlockSpec((1, L, rw), lambda i, c: (i, c, 1)),
                  pl.BlockSpec((1, L, rw), lambda i, c: (i, c, 2)),
                  pl.BlockSpec((1, L, RG_PAD), lambda i, c: (i, c, COL_RG // RG_PAD)),
                  pl.BlockSpec((1, L, WA_W), lambda i, c: (i, c, COL_WA // WA_W)),
                  const((PV_ROWS, rw)), const((1, RG_PAD)), const((1, WA_W)),
                  const((RWKV_DECAY_RANK, rw)), const((RWKV_A_RANK, rw)), const((RG_PAD, rw)),
                  const((RWKV_GROUP_W, RWKV_GROUP_W))],
        out_specs=[pl.BlockSpec((1, L, rw), lambda i, c: (i, c, 0))],
        scratch_shapes=[pltpu.VMEM((RWKV_GROUPS, RWKV_HEAD, RWKV_GROUP_W), F32),
                        pltpu.VMEM((8, rw), F32), pltpu.VMEM((8, rw), F32), pltpu.VMEM((8, rw), F32),
                        pltpu.VMEM((8, RG_PAD), F32), pltpu.VMEM((8, WA_W), F32)],
        compiler_params=_params(("parallel", "arbitrary")),
        name="rwkv7",
    )(main3, main3, main3, small3, small3, pvec, mu_g, mu_wa, w_up, a_up, g_up, seg)


def _mix_out_kernel(ha_ref, hb_ref, ga_ref, gb_ref, wa_ref, wb_ref, wo_ref, g_ref, x_ref, o_ref):
    ya = jnp.dot(ha_ref[...], wa_ref[...], preferred_element_type=F32)
    yb = jnp.dot(hb_ref[...], wb_ref[...], preferred_element_type=F32)
    merged = _bf(_sigmoid(ga_ref[...].astype(F32)) * ya + _sigmoid(gb_ref[...].astype(F32)) * yb)
    y = jnp.dot(merged, wo_ref[...], preferred_element_type=F32)
    o_ref[...] = x_ref[...] + _rms(y, g_ref[...])


def _mix_out(ha, hb, gates, wa, wb, wo, g, x, *, tm):
    t, kdim = ha.shape
    d = wo.shape[1]
    resident = lambda shape: pl.BlockSpec(shape, lambda i: (0, 0), pipeline_mode=pl.Buffered(1))
    return pl.pallas_call(
        _mix_out_kernel,
        out_shape=jax.ShapeDtypeStruct((t, d), F32),
        grid=(t // tm,),
        in_specs=[pl.BlockSpec((tm, kdim), lambda i: (i, 0)),
                  pl.BlockSpec((tm, kdim), lambda i: (i, 0)),
                  pl.BlockSpec((tm, d), lambda i: (i, 0)),
                  pl.BlockSpec((tm, d), lambda i: (i, 1)),
                  resident((kdim, d)), resident((kdim, d)), resident((d, d)),
                  pl.BlockSpec((1, d), lambda i: (0, 0)),
                  pl.BlockSpec((tm, d), lambda i: (i, 0))],
        out_specs=pl.BlockSpec((tm, d), lambda i: (i, 0)),
        compiler_params=_params(("parallel",)),
        name="mix_out",
    )(ha, hb, gates, gates, wa, wb, wo, g, x)


def _xattn_kernel(x_ref, gpre_ref, wq_ref, kv_ref, wo_ref, gpost_ref, o_ref):
    x = x_ref[...]
    h = _bf(_rms(x, gpre_ref[...]))
    q = jnp.dot(h, wq_ref[...], preferred_element_type=F32)
    scale = XATTN_HEAD_DIM ** -0.5
    outs = []
    for hd in range(XATTN_HEADS):
        sl = slice(hd * XATTN_HEAD_DIM, (hd + 1) * XATTN_HEAD_DIM)
        k = kv_ref[0, :, sl]
        v = kv_ref[0, :, XATTN_WIDTH + hd * XATTN_HEAD_DIM:XATTN_WIDTH + (hd + 1) * XATTN_HEAD_DIM]
        sc = _mm_nt(q[:, sl], k) * scale
        sc = sc - jnp.max(sc, axis=-1, keepdims=True)
        e = jnp.exp(sc)
        p = e / jnp.sum(e, axis=-1, keepdims=True)
        outs.append(_mm(p, v))
    o = jnp.concatenate(outs, axis=-1)
    y = jnp.dot(_bf(o), wo_ref[...], preferred_element_type=F32)
    o_ref[...] = x + _rms(y, gpost_ref[...])


def _xattn(x, gpre, wq, kv3, wo, gpost, *, tm, seq):
    t, d = x.shape
    per_seq = seq // tm
    return pl.pallas_call(
        _xattn_kernel,
        out_shape=jax.ShapeDtypeStruct((t, d), F32),
        grid=(t // tm,),
        in_specs=[pl.BlockSpec((tm, d), lambda i: (i, 0)),
                  pl.BlockSpec((1, d), lambda i: (0, 0)),
                  pl.BlockSpec((d, XATTN_WIDTH), lambda i: (0, 0)),
                  pl.BlockSpec((1, MEM_LEN, 2 * XATTN_WIDTH), lambda i: (i // per_seq, 0, 0)),
                  pl.BlockSpec((XATTN_WIDTH, d), lambda i: (0, 0)),
                  pl.BlockSpec((1, d), lambda i: (0, 0))],
        out_specs=pl.BlockSpec((tm, d), lambda i: (i, 0)),
        compiler_params=_params(("parallel",)),
        name="xattn",
    )(x, gpre, wq, kv3, wo, gpost)


FFN_HALO = 8


def _gelu_tanh(x):
    return 0.5 * x * (1.0 + jnp.tanh(0.7978845608028654 * (x + 0.044715 * x * x * x)))


def _ffn_kernel(x_ref, halo_ref, gpre_ref, wg_ref, wu_ref, cwg_ref, cwu_ref, cbg_ref, cbu_ref, wd_ref, gpost_ref,
                o_ref, h_ref, *, tiles_per_seq):
    i = pl.program_id(0)
    j = pl.program_id(1)

    @pl.when(j == 0)
    def _():
        keep = jnp.where(i % tiles_per_seq == 0, 0.0, 1.0)
        h_ref[0:FFN_HALO, :] = _bf(_rms(halo_ref[...], gpre_ref[...]) * keep)
        h_ref[FFN_HALO:, :] = _bf(_rms(x_ref[...], gpre_ref[...]))
        o_ref[...] = jnp.zeros_like(o_ref)

    def conv(w_ref, cw_ref, cb_ref):
        u = jnp.dot(h_ref[...], w_ref[...], preferred_element_type=F32)
        u1 = pltpu.roll(u, 1, 0)
        u2 = pltpu.roll(u, 2, 0)
        cw = cw_ref[...]
        full = cb_ref[...] + cw[2:3, :] * u + cw[1:2, :] * u1 + cw[0:1, :] * u2
        return full[FFN_HALO:, :]

    gate = conv(wg_ref, cwg_ref, cbg_ref)
    up = conv(wu_ref, cwu_ref, cbu_ref)
    act = _bf(_gelu_tanh(gate) * up)
    o_ref[...] += jnp.dot(act, wd_ref[...], preferred_element_type=F32)

    @pl.when(j == pl.num_programs(1) - 1)
    def _():
        o_ref[...] = x_ref[...] + _rms(o_ref[...], gpost_ref[...])


def _ffn(x, gpre, w_up, conv_w, conv_b, w_down, gpost, *, tm, tn, seq):
    t, d = x.shape
    nj = D_FF // tn
    hb = tm // FFN_HALO
    return pl.pallas_call(
        functools.partial(_ffn_kernel, tiles_per_seq=seq // tm),
        out_shape=jax.ShapeDtypeStruct((t, d), F32),
        grid=(t // tm, nj),
        in_specs=[pl.BlockSpec((tm, d), lambda i, j: (i, 0)),
                  pl.BlockSpec((FFN_HALO, d), lambda i, j: (jnp.maximum(i * hb - 1, 0), 0)),
                  pl.BlockSpec((1, d), lambda i, j: (0, 0)),
                  pl.BlockSpec((d, tn), lambda i, j: (0, j)),
                  pl.BlockSpec((d, tn), lambda i, j: (0, nj + j)),
                  pl.BlockSpec((CONV_WIDTH, tn), lambda i, j: (0, j)),
                  pl.BlockSpec((CONV_WIDTH, tn), lambda i, j: (0, nj + j)),
                  pl.BlockSpec((1, tn), lambda i, j: (0, j)),
                  pl.BlockSpec((1, tn), lambda i, j: (0, nj + j)),
                  pl.BlockSpec((tn, d), lambda i, j: (j, 0)),
                  pl.BlockSpec((1, d), lambda i, j: (0, 0))],
        out_specs=pl.BlockSpec((tm, d), lambda i, j: (i, 0), pipeline_mode=pl.Buffered(1)),
        scratch_shapes=[pltpu.VMEM((tm + FFN_HALO, d), BF16)],
        compiler_params=_params(("parallel", "arbitrary")),
        name="conv_glu_ffn",
    )(x, x, gpre, w_up, w_up, conv_w, conv_w, conv_b, conv_b, w_down, gpost)


def _small_in_proj(w_in):
    d = w_in.shape[1]
    r0, g0 = MLSTM_TOTAL, MLSTM_TOTAL + RWKV_TOTAL
    mlstm_if = w_in[MLSTM_MAIN:r0]
    rwkv_wa = w_in[r0 + RWKV_MAIN:r0 + RWKV_MAIN + WA_W]
    rwkv_g = w_in[r0 + RWKV_MAIN + WA_W:g0]
    z = lambda n: jnp.zeros((n, d), w_in.dtype)
    return jnp.concatenate([rwkv_g, z(RG_PAD - RWKV_GATE_RANK), rwkv_wa, mlstm_if, z(IF_W - 2 * MLSTM_HEADS)], axis=0)


def _layer(x, mem, mix_pre_norm, w_in, mlstm_b_i, mlstm_b_f, mlstm_head_norm, rwkv_mu, rwkv_w0, rwkv_w_up, rwkv_a0,
           rwkv_a_up, rwkv_g_up, rwkv_k_k, rwkv_k_a, rwkv_r_k, rwkv_ln_g, rwkv_ln_b, w_branch_a, w_branch_b,
           w_mix_out, mix_post_norm, xattn_pre_norm, mem_norm, xattn_wq, xattn_wkv, xattn_wo, xattn_post_norm,
           ffn_pre_norm, ffn_w_up, ffn_conv_w, ffn_conv_b, ffn_w_down, ffn_post_norm):
    b, s, d = x.shape
    t = b * s
    x2d = x.reshape(t, d)
    row = lambda p: p.reshape(1, -1).astype(F32)
    tm_row = min(TM_ROW, s)

    w_in_t = w_in.T
    tm_in = min(TM_IN_PROJ, t)
    small, h = _norm_matmul(x2d, row(mix_pre_norm), _small_in_proj(w_in_t), tm=tm_in, tn=SMALL_COLS, out_dtype=F32,
                            name="in_proj_small", w_is_transposed=True)
    small3 = small.reshape(b, s, SMALL_COLS)
    wide = functools.partial(_rowrange_matmul, h, w_in_t, tm=tm_in, tn=TN_IN_PROJ, out_dtype=BF16)
    (mlstm_main,) = wide(row0=0, nrows=MLSTM_MAIN, name="in_proj_mlstm")
    (rwkv_main,) = wide(row0=MLSTM_TOTAL, nrows=RWKV_MAIN, name="in_proj_rwkv")
    gates, ffn_w_up_b = wide(row0=MLSTM_TOTAL + RWKV_TOTAL, nrows=2 * D_MODEL, name="in_proj_gates",
                             cast_arrays=(ffn_w_up,))

    gate_bias = jnp.concatenate([mlstm_b_i, mlstm_b_f, jnp.zeros((IF_W - 2 * MLSTM_HEADS,), F32)]).reshape(1, IF_W)
    ha, w_a_b, w_b_b, w_out_b = _mlstm(mlstm_main.reshape(b, s, MLSTM_MAIN), small3, gate_bias, row(mlstm_head_norm),
                                       cast_arrays=(w_branch_a, w_branch_b, w_mix_out))

    mu_r, mu_k, mu_v = (rwkv_mu[i * RWKV_WIDTH:(i + 1) * RWKV_WIDTH] for i in range(3))
    mu_wa = rwkv_mu[RWKV_MAIN:RWKV_MAIN + WA_W]
    mu_g = jnp.pad(rwkv_mu[RWKV_MAIN + WA_W:], (0, RG_PAD - RWKV_GATE_RANK))
    pvec = jnp.stack([mu_r, mu_k, mu_v, rwkv_w0, rwkv_a0, rwkv_k_k, rwkv_k_a, rwkv_r_k.reshape(-1), rwkv_ln_g,
                      rwkv_ln_b] + [jnp.zeros((RWKV_WIDTH,), F32)] * (PV_ROWS - 10))
    g_up = jnp.pad(rwkv_g_up, ((0, RG_PAD - RWKV_GATE_RANK), (0, 0)))
    hb, ffn_w_down_b = _rwkv(rwkv_main.reshape(b, s, RWKV_MAIN), small3, pvec, row(mu_g), row(mu_wa), _bf(rwkv_w_up),
                             _bf(rwkv_a_up), _bf(g_up), cast_arrays=(ffn_w_down,))

    x1 = _mix_out(ha.reshape(t, -1), hb.reshape(t, -1), gates, w_a_b, w_b_b, w_out_b, row(mix_post_norm), x2d,
                  tm=tm_row)

    mem2d = mem.reshape(b * MEM_LEN, d)
    kv, _ = _norm_matmul(mem2d, row(mem_norm), _bf(xattn_wkv), tm=MEM_LEN, tn=TN, out_dtype=F32, name="mem_kv")
    x2 = _xattn(x1, row(xattn_pre_norm), _bf(xattn_wq), kv.reshape(b, MEM_LEN, 2 * XATTN_WIDTH), _bf(xattn_wo),
                row(xattn_post_norm), tm=tm_row, seq=s)

    x3 = _ffn(x2, row(ffn_pre_norm), ffn_w_up_b, ffn_conv_w, row(ffn_conv_b), ffn_w_down_b, row(ffn_post_norm),
              tm=min(TM_FFN, s), tn=TN, seq=s)
    return x3.reshape(b, s, d)


def kernel(x, mem, mix_pre_norm, w_in, mlstm_b_i, mlstm_b_f, mlstm_head_norm, rwkv_mu, rwkv_w0, rwkv_w_up, rwkv_a0, rwkv_a_up, rwkv_g_up, rwkv_k_k, rwkv_k_a, rwkv_r_k, rwkv_ln_g, rwkv_ln_b, w_branch_a, w_branch_b, w_mix_out, mix_post_norm, xattn_pre_norm, mem_norm, xattn_wq, xattn_wkv, xattn_wo, xattn_post_norm, ffn_pre_norm, ffn_w_up, ffn_conv_w, ffn_conv_b, ffn_w_down, ffn_post_norm):
    for l in range(mix_pre_norm.shape[0]):
        x = _layer(x, mem, mix_pre_norm[l], w_in[l], mlstm_b_i[l], mlstm_b_f[l], mlstm_head_norm[l], rwkv_mu[l],
                   rwkv_w0[l], rwkv_w_up[l], rwkv_a0[l], rwkv_a_up[l], rwkv_g_up[l], rwkv_k_k[l], rwkv_k_a[l],
                   rwkv_r_k[l], rwkv_ln_g[l], rwkv_ln_b[l], w_branch_a[l], w_branch_b[l], w_mix_out[l],
                   mix_post_norm[l], xattn_pre_norm[l], mem_norm[l], xattn_wq[l], xattn_wkv[l], xattn_wo[l],
                   xattn_post_norm[l], ffn_pre_norm[l], ffn_w_up[l], ffn_conv_w[l], ffn_conv_b[l], ffn_w_down[l],
                   ffn_post_norm[l])
    return x
```

```python
import functools

import jax
import jax.numpy as jnp
from jax import lax
from jax.experimental import pallas as pl
from jax.experimental.pallas import tpu as pltpu

D_MODEL = 2048
MEM_LEN = 256
RMS_EPS = 1e-6

MLSTM_HEADS = 4
MLSTM_WIDTH = D_MODEL // 2
MLSTM_V_DIM = MLSTM_WIDTH // MLSTM_HEADS
MLSTM_QK_DIM = MLSTM_V_DIM // 2
MLSTM_QK_WIDTH = MLSTM_HEADS * MLSTM_QK_DIM
GATE_SOFTCAP = 15.0

RWKV_WIDTH = D_MODEL // 2
RWKV_HEAD = 64
RWKV_HEADS = RWKV_WIDTH // RWKV_HEAD
RWKV_DECAY_RANK = 64
RWKV_A_RANK = 64
RWKV_GATE_RANK = 160
RWKV_GN_EPS = 64e-5

MLSTM_MAIN = 2 * MLSTM_QK_WIDTH + 2 * MLSTM_WIDTH
MLSTM_TOTAL = MLSTM_MAIN + 2 * MLSTM_HEADS
RWKV_MAIN = 3 * RWKV_WIDTH
RWKV_TOTAL = RWKV_MAIN + RWKV_DECAY_RANK + RWKV_A_RANK + RWKV_GATE_RANK

XATTN_HEADS = 4
XATTN_HEAD_DIM = 128
XATTN_WIDTH = XATTN_HEADS * XATTN_HEAD_DIM

D_FF = 4 * D_MODEL
CONV_WIDTH = 3

RG_PAD = 256
WA_W = RWKV_DECAY_RANK + RWKV_A_RANK
IF_W = 128
COL_RG = 0
COL_WA = COL_RG + RG_PAD
COL_IF = COL_WA + WA_W
SMALL_COLS = COL_IF + IF_W

MLSTM_CHUNK = 256
RWKV_CHUNK = 64
RWKV_CHUNKS_PER_STEP = 2
RWKV_GROUP_W = 256
RWKV_GROUPS = RWKV_WIDTH // RWKV_GROUP_W
VMEM_LIMIT = 56 * 1024 * 1024

LANES = 128
SUBLANES = 8
BF16_SUBLANES = 16
TM_IN_PROJ = 1024
TN_IN_PROJ = 1024
TM_ROW = 512
TM_XATTN = 1024
TM_FFN = 1024
TN = 512

F32 = jnp.float32
BF16 = jnp.bfloat16


def _bf(x):
    return x.astype(BF16)


def _mm(a, b):
    return jnp.dot(_bf(a), _bf(b), preferred_element_type=F32)


def _mm_nt(a, b):
    return lax.dot_general(_bf(a), _bf(b), (((1,), (1,)), ((), ())), preferred_element_type=F32)


def _mm_tn(a, b):
    return lax.dot_general(_bf(a), _bf(b), (((0,), (0,)), ((), ())), preferred_element_type=F32)


def _mm_exact_lhs(tri, x):
    hi = _bf(x)
    r1 = x - hi.astype(F32)
    mid = _bf(r1)
    lo = _bf(r1 - mid.astype(F32))
    t = _bf(tri)
    return (jnp.dot(t, hi, preferred_element_type=F32) + jnp.dot(t, mid, preferred_element_type=F32)
            + jnp.dot(t, lo, preferred_element_type=F32))


def _rms(x, g):
    return x * lax.rsqrt(jnp.mean(x * x, axis=-1, keepdims=True) + RMS_EPS) * g


def _softplus(x):
    return jnp.maximum(x, 0.0) + jnp.log(1.0 + jnp.exp(-jnp.abs(x)))


def _sigmoid(x):
    return 1.0 / (1.0 + jnp.exp(-x))


def _params(sem):
    return pltpu.CompilerParams(dimension_semantics=sem, vmem_limit_bytes=VMEM_LIMIT)


def _call_with_casts(body, *, n_in, out_shape, grid, in_specs, out_specs, cast_arrays=(), **kwargs):
    n_out = len(out_shape)
    k = len(cast_arrays)
    nsteps = 1
    for extent in grid:
        nsteps *= extent

    def step(*ids):
        lin = ids[0]
        for extent, idx in zip(grid[1:], ids[1:]):
            lin = lin * extent + idx
        return lin

    band_specs, band_shapes = [], []
    for arr in cast_arrays:
        rows, cols = arr.shape
        assert rows % nsteps == 0 and (rows // nsteps) % BF16_SUBLANES == 0, (arr.shape, nsteps)
        band_specs.append(pl.BlockSpec((rows // nsteps, cols), lambda *ids: (step(*ids), 0)))
        band_shapes.append(jax.ShapeDtypeStruct(arr.shape, BF16))

    def kernel(*refs):
        ins, src = refs[:n_in], refs[n_in:n_in + k]
        outs, dst = refs[n_in + k:n_in + k + n_out], refs[n_in + k + n_out:n_in + 2 * k + n_out]
        for s_ref, d_ref in zip(src, dst):
            d_ref[...] = s_ref[...].astype(d_ref.dtype)
        body(*ins, *outs, *refs[n_in + 2 * k + n_out:])

    def run(*operands):
        assert len(operands) == n_in
        return pl.pallas_call(kernel, out_shape=tuple(out_shape) + tuple(band_shapes), grid=grid,
                              in_specs=list(in_specs) + band_specs, out_specs=tuple(out_specs) + tuple(band_specs),
                              **kwargs)(*operands, *cast_arrays)

    return run


def _norm_matmul_kernel(x_ref, g_ref, w_ref, o_ref, h_ref, *, w_is_transposed):
    @pl.when(pl.program_id(1) == 0)
    def _():
        h_ref[...] = _bf(_rms(x_ref[...], g_ref[...]))

    w = w_ref[...].T if w_is_transposed else w_ref[...]
    o_ref[...] = jnp.dot(h_ref[...], _bf(w), preferred_element_type=F32).astype(o_ref.dtype)


def _norm_matmul(x, g, w, *, tm, tn, out_dtype, name, w_is_transposed=False):
    t, d = x.shape
    n = w.shape[0] if w_is_transposed else w.shape[1]
    w_spec = pl.BlockSpec((tn, d), lambda i, j: (j, 0)) if w_is_transposed else pl.BlockSpec((d, tn), lambda i, j: (0, j))
    return pl.pallas_call(
        functools.partial(_norm_matmul_kernel, w_is_transposed=w_is_transposed),
        out_shape=(jax.ShapeDtypeStruct((t, n), out_dtype), jax.ShapeDtypeStruct((t, d), BF16)),
        grid=(t // tm, n // tn),
        in_specs=[pl.BlockSpec((tm, d), lambda i, j: (i, 0)),
                  pl.BlockSpec((1, d), lambda i, j: (0, 0)),
                  w_spec],
        out_specs=(pl.BlockSpec((tm, tn), lambda i, j: (i, j)),
                   pl.BlockSpec((tm, d), lambda i, j: (i, 0))),
        compiler_params=_params(("parallel", "arbitrary")),
        name=name,
    )(x, g, w)


def _rowrange_matmul_kernel(a_ref, wt_ref, o_ref, wb_ref):
    @pl.when(pl.program_id(1) == 0)
    def _():
        wb_ref[...] = _bf(wt_ref[...].T)

    o_ref[...] = jnp.dot(a_ref[...], wb_ref[...], preferred_element_type=F32).astype(o_ref.dtype)


def _rowrange_matmul(a, w_t, *, row0, nrows, tm, tn, out_dtype, name, cast_arrays=()):
    t, d = a.shape
    assert row0 % SUBLANES == 0 and nrows % tn == 0
    return _call_with_casts(
        _rowrange_matmul_kernel, n_in=2,
        out_shape=[jax.ShapeDtypeStruct((t, nrows), out_dtype)],
        grid=(nrows // tn, t // tm),
        in_specs=[pl.BlockSpec((tm, d), lambda j, i: (i, 0)),
                  pl.BlockSpec((pl.Element(tn), pl.Element(d)),
                               lambda j, i: (pl.multiple_of(row0 + tn * j, SUBLANES), 0))],
        out_specs=[pl.BlockSpec((tm, tn), lambda j, i: (i, j))],
        cast_arrays=cast_arrays,
        scratch_shapes=[pltpu.VMEM((d, tn), BF16)],
        compiler_params=_params(("parallel", "arbitrary")),
        name=name,
    )(a, w_t)


def _mlstm_kernel(q_ref, k_ref, v_ref, o_ref, gate_ref, bias_ref, hn_ref, out_ref, ct_ref, n_ref, m_ref):
    L = MLSTM_CHUNK
    dk, dv = MLSTM_QK_DIM, MLSTM_V_DIM

    @pl.when(pl.program_id(1) == 0)
    def _():
        ct_ref[...] = jnp.zeros_like(ct_ref)
        n_ref[...] = jnp.zeros_like(n_ref)
        m_ref[...] = jnp.zeros_like(m_ref)

    pre = gate_ref[0] + bias_ref[...]
    capped = GATE_SOFTCAP * jnp.tanh(pre / GATE_SOFTCAP)
    logf = -_softplus(-capped)
    row = lax.broadcasted_iota(jnp.int32, (L, L), 0)
    col = lax.broadcasted_iota(jnp.int32, (L, L), 1)
    causal = row >= col
    bcum = _mm_exact_lhs(causal.astype(F32), logf)
    ig_t = capped.T
    bcum_t = bcum.T
    scale = MLSTM_QK_DIM ** -0.5

    H = range(MLSTM_HEADS)
    q = [q_ref[0, :, h * dk:(h + 1) * dk] for h in H]
    k = [k_ref[0, :, h * dk:(h + 1) * dk] for h in H]
    v = [v_ref[0, :, h * dv:(h + 1) * dv] for h in H]
    b_c = [bcum[:, MLSTM_HEADS + h:MLSTM_HEADS + h + 1] for h in H]
    b_r = [bcum_t[MLSTM_HEADS + h:MLSTM_HEADS + h + 1, :] for h in H]
    i_c = [capped[:, h:h + 1] for h in H]
    i_r = [ig_t[h:h + 1, :] for h in H]
    m_prev = [m_ref[h][0:1, 0:1] for h in H]
    ct = [ct_ref[h] for h in H]
    nrow = [n_ref[h][0:1, :] for h in H]

    dmat = [jnp.where(causal, b_c[h] - b_r[h] + i_r[h], -jnp.inf) for h in H]
    inter = [b_c[h] + m_prev[h] for h in H]
    m_t = [jnp.maximum(inter[h], jnp.max(dmat[h], axis=-1, keepdims=True)) for h in H]
    dexp = [jnp.exp(dmat[h] - m_t[h]) * scale for h in H]
    w_inter = [jnp.exp(inter[h] - m_t[h]) for h in H]
    qk = [_mm_nt(q[h], k[h]) for h in H]
    qc = [_mm(q[h], ct[h]) for h in H]
    s = [qk[h] * dexp[h] for h in H]
    sv = [_mm(s[h], v[h]) for h in H]
    num = [w_inter[h] * qc[h] + sv[h] for h in H]
    den = [w_inter[h] * jnp.sum(q[h].astype(F32) * nrow[h], axis=-1, keepdims=True)
           + jnp.sum(s[h], axis=-1, keepdims=True) for h in H]
    hh = [num[h] / jnp.maximum(jnp.abs(den[h]), jnp.exp(-m_t[h])) for h in H]

    b_last = [b_c[h][L - 1:L, :] for h in H]
    gs = [b_last[h] - b_c[h] + i_c[h] for h in H]
    m_new = [jnp.maximum(b_last[h] + m_prev[h], jnp.max(gs[h], axis=0, keepdims=True)) for h in H]
    carry_w = [jnp.exp(b_last[h] + m_prev[h] - m_new[h]) for h in H]
    ws = [jnp.exp(gs[h] - m_new[h]) * scale for h in H]
    kv = [_mm_tn(k[h], ws[h] * v[h].astype(F32)) for h in H]
    for h in H:
        ct_ref[h] = carry_w[h] * ct[h] + kv[h]
        n_ref[h] = jnp.broadcast_to(
            carry_w[h] * nrow[h] + jnp.sum(ws[h] * k[h].astype(F32), axis=0, keepdims=True), n_ref.shape[1:])
        m_ref[h] = jnp.broadcast_to(m_new[h], m_ref.shape[1:])

    ms = [jnp.mean(hh[h] * hh[h], axis=-1, keepdims=True) for h in H]
    for h in H:
        hm = hh[h] * lax.rsqrt(ms[h] + RMS_EPS) * hn_ref[:, h * dv:(h + 1) * dv]
        og = o_ref[0, :, h * dv:(h + 1) * dv].astype(F32)
        out_ref[0, :, h * dv:(h + 1) * dv] = (_sigmoid(og) * hm).astype(out_ref.dtype)


def _mlstm(main3, small3, gate_bias, head_norm, cast_arrays=()):
    b, s, _ = main3.shape
    L = min(MLSTM_CHUNK, s)
    assert L == MLSTM_CHUNK and s % L == 0
    qw, vw = MLSTM_QK_WIDTH, MLSTM_WIDTH
    return _call_with_casts(
        _mlstm_kernel, n_in=7, cast_arrays=cast_arrays,
        out_shape=[jax.ShapeDtypeStruct((b, s, MLSTM_WIDTH), BF16)],
        grid=(b, s // L),
        in_specs=[pl.BlockSpec((1, L, qw), lambda i, c: (i, c, 0)),
                  pl.BlockSpec((1, L, qw), lambda i, c: (i, c, 1)),
                  pl.BlockSpec((1, L, vw), lambda i, c: (i, c, 1)),
                  pl.BlockSpec((1, L, vw), lambda i, c: (i, c, 2)),
                  pl.BlockSpec((1, L, IF_W), lambda i, c: (i, c, COL_IF // IF_W)),
                  pl.BlockSpec((1, IF_W), lambda i, c: (0, 0)),
                  pl.BlockSpec((1, vw), lambda i, c: (0, 0))],
        out_specs=[pl.BlockSpec((1, L, vw), lambda i, c: (i, c, 0))],
        scratch_shapes=[pltpu.VMEM((MLSTM_HEADS, MLSTM_QK_DIM, MLSTM_V_DIM), F32),
                        pltpu.VMEM((MLSTM_HEADS, SUBLANES, MLSTM_QK_DIM), F32),
                        pltpu.VMEM((MLSTM_HEADS, SUBLANES, LANES), F32)],
        compiler_params=_params(("parallel", "arbitrary")),
        name="mlstm",
    )(main3, main3, main3, main3, small3, gate_bias, head_norm)


PV_MU_R, PV_MU_K, PV_MU_V, PV_W0, PV_A0, PV_KK, PV_KA, PV_RK, PV_LNG, PV_LNB = range(10)
PV_ROWS = 16


def _rwkv_kernel(r_ref, k_ref, v_ref, g_ref, wa_ref, pv_ref, mug_ref, muwa_ref, wup_ref, aup_ref, gup_ref, seg_ref,
                 out_ref, s_ref, pr_ref, pk_ref, pvv_ref, pg_ref, pwa_ref):
    L = RWKV_CHUNK
    CH = RWKV_CHUNKS_PER_STEP
    R = CH * L
    N = RWKV_HEAD
    GW = RWKV_GROUP_W
    GH = GW // N

    @pl.when(pl.program_id(1) == 0)
    def _():
        s_ref[...] = jnp.zeros_like(s_ref)
        pr_ref[...] = jnp.zeros_like(pr_ref)
        pk_ref[...] = jnp.zeros_like(pk_ref)
        pvv_ref[...] = jnp.zeros_like(pvv_ref)
        pg_ref[...] = jnp.zeros_like(pg_ref)
        pwa_ref[...] = jnp.zeros_like(pwa_ref)

    groups = range(RWKV_GROUPS)
    rows = lambda c: slice(c * L, (c + 1) * L)
    lanes = lambda g: slice(g * GW, (g + 1) * GW)
    cat = lambda u, w_: jnp.concatenate([u, w_], axis=0)
    seg = seg_ref[...]
    pv = pv_ref[...]

    def shift_lerp(x_ref, prev_ref, mu, ls=slice(None)):
        x = x_ref[0, :, ls].astype(F32)
        rid = lax.broadcasted_iota(jnp.int32, x.shape, 0)
        xs = jnp.where(rid == 0, prev_ref[0:1, ls], pltpu.roll(x, 1, 0))
        prev_ref[0:1, ls] = x[R - 1:R, :]
        return x + (xs - x) * mu

    gl = shift_lerp(g_ref, pg_ref, mug_ref[...])
    wa = shift_lerp(wa_ref, pwa_ref, muwa_ref[...])
    wl_b = _bf(jnp.tanh(wa[:, 0:RWKV_DECAY_RANK]))
    al_b = _bf(wa[:, RWKV_DECAY_RANK:WA_W])
    sgl_b = _bf(_sigmoid(gl))
    tr = lax.broadcasted_iota(jnp.int32, (R, R), 0)
    tc = lax.broadcasted_iota(jnp.int32, (R, R), 1)
    tri = ((tr >= tc) & ((tr // L) == (tc // L))).astype(F32)

    ti = lax.broadcasted_iota(jnp.int32, (L, GW), 0)
    lane = lax.broadcasted_iota(jnp.int32, (L, GW), 1)
    si = lane & (N - 1)
    lane_head = lane // N
    incl = ti >= si
    strict = ti > si
    eye = (ti == si).astype(F32)
    lvl_masks = []
    bsz = 1
    while bsz < L:
        same = (ti // (2 * bsz)) == (si // (2 * bsz))
        lvl_masks.append(jnp.where(same & ((ti & bsz) != 0) & ((si & bsz) == 0), 1.0, 0.0))
        bsz *= 2

    def bd(x):
        return jnp.concatenate([_bf(x)] * GH, axis=0) * seg

    def fold(z):
        acc = z[0:N]
        for hh in range(1, GH):
            acc = jnp.where(lane_head == hh, z[hh * N:(hh + 1) * N], acc)
        return acc

    def mmb(x, b):
        return jnp.dot(_bf(x), b, preferred_element_type=F32)

    def mmb_nt(x, b):
        return lax.dot_general(_bf(x), b, (((1,), (1,)), ((), ())), preferred_element_type=F32)

    prep = []
    for g in groups:
        ls = lanes(g)
        row = lambda i, ls=ls: pv[i:i + 1, ls]
        r = shift_lerp(r_ref, pr_ref, row(PV_MU_R), ls)
        kr = shift_lerp(k_ref, pk_ref, row(PV_MU_K), ls)
        v = shift_lerp(v_ref, pvv_ref, row(PV_MU_V), ls)
        w_log = -_softplus(-(row(PV_W0) + jnp.dot(wl_b, wup_ref[:, ls], preferred_element_type=F32))) - 0.5
        lw = -jnp.exp(w_log)
        a = _sigmoid(row(PV_A0) + jnp.dot(al_b, aup_ref[:, ls], preferred_element_type=F32))
        gate = jnp.dot(sgl_b, gup_ref[:, ls], preferred_element_type=F32)
        cs = _mm_exact_lhs(tri, lw)
        g_incl = jnp.exp(cs)
        g_inv = jnp.exp(-cs)
        kk0 = kr * row(PV_KK)
        kr2 = kr * (1.0 + (a - 1.0) * row(PV_KA))
        sums = jnp.dot(_bf(cat(kk0 * kk0, r * kr2 * row(PV_RK))), seg, preferred_element_type=F32)
        kk = kk0 * lax.rsqrt(jnp.maximum(sums[:R], 1e-24))
        prep.append(dict(p=jnp.exp(cs - lw) * kk, qt=kk * a * g_inv, kt=kr2 * g_inv, rt=r * g_incl, v=v,
                         bonus_v=sums[R:] * v, gate=gate, g_incl=g_incl,
                         ln_g=row(PV_LNG), ln_b=row(PV_LNB)))

    E = [(c, g) for c in range(CH) for g in groups]
    n_e = range(len(E))
    blk = lambda name: [prep[g][name][rows(c), :] for c, g in E]
    p, qt, kt, rt, vv, bonus_v, gate = (blk(n_) for n_ in ("p", "qt", "kt", "rt", "v", "bonus_v", "gate"))
    gl_ = [prep[g]["g_incl"][c * L + L - 1:c * L + L, :] for c, g in E]
    qg = [qt[e] * gl_[e] for e in n_e]
    kg = [kt[e] * gl_[e] for e in n_e]

    lhs_pr = [cat(p[e], rt[e]) for e in n_e]
    gq = [mmb_nt(lhs_pr[e], bd(qt[e])) for e in n_e]
    gk = [mmb_nt(lhs_pr[e], bd(kt[e])) for e in n_e]
    n_pq = [jnp.where(strict, gq[e][:L], 0.0) for e in n_e]
    a_rq = [jnp.where(incl, gq[e][L:], 0.0) for e in n_e]
    a_pk = [jnp.where(strict, gk[e][:L], 0.0) for e in n_e]
    a_rk = [jnp.where(incl, gk[e][L:], 0.0) for e in n_e]

    x = [eye - n_pq[e] * lvl_masks[0] for e in n_e]
    bd_n = [bd(n_pq[e]) for e in n_e]
    for msk in lvl_masks[1:]:
        t1 = [mmb(x[e], bd_n[e]) for e in n_e]
        x = [x[e] - msk * mmb(t1[e], bd(x[e])) for e in n_e]

    av = [mmb(cat(a_pk[e], a_rk[e]), bd(vv[e])) for e in n_e]
    w = [mmb(x[e], bd(p[e])) for e in n_e]
    u0 = [mmb(x[e], bd(av[e][:L])) for e in n_e]
    gmat = [rt[e] - mmb(a_rq[e], bd(w[e])) for e in n_e]
    y0 = [av[e][L:] - mmb(a_rq[e], bd(u0[e])) for e in n_e]
    mt = [eye * gl_[e] - fold(_mm_tn(qg[e], w[e])) for e in n_e]
    bt = [fold(_mm_tn(cat(kg[e], -qg[e]), cat(vv[e], u0[e]))) for e in n_e]

    state = [s_ref[g] for g in groups]
    y = [None] * len(E)
    for c in range(CH):
        es = [c * RWKV_GROUPS + g for g in groups]
        ys = [mmb(cat(gmat[e], mt[e]), bd(state[g])) for g, e in zip(groups, es)]
        for g, e in zip(groups, es):
            y[e] = ys[g][:L] + y0[e]
            state[g] = ys[g][L:] + bt[e]
    for g in groups:
        s_ref[g] = state[g]

    inv_n = 1.0 / N
    sums1 = [mmb(cat(y[e], y[e] * y[e]), seg) for e in n_e]
    for e, (c, g) in enumerate(E):
        mean = sums1[e][:L] * inv_n
        var = sums1[e][L:] * inv_n - mean * mean
        yn = (y[e] - mean) * lax.rsqrt(var + RWKV_GN_EPS) * prep[g]["ln_g"] + prep[g]["ln_b"]
        out_ref[0, rows(c), lanes(g)] = ((yn + bonus_v[e]) * gate[e]).astype(out_ref.dtype)


def _rwkv(main3, small3, pvec, mu_g, mu_wa, w_up, a_up, g_up, cast_arrays=()):
    b, s, _ = main3.shape
    L = RWKV_CHUNK * RWKV_CHUNKS_PER_STEP
    rw = RWKV_WIDTH
    head_of = jnp.arange(RWKV_GROUP_W) // RWKV_HEAD
    seg = (head_of[:, None] == head_of[None, :]).astype(BF16)
    const = lambda shape: pl.BlockSpec(shape, lambda i, c: (0, 0))
    return _call_with_casts(
        _rwkv_kernel, n_in=12, cast_arrays=cast_arrays,
        out_shape=[jax.ShapeDtypeStruct((b, s, rw), BF16)],
        grid=(b, s // L),
        in_specs=[pl.BlockSpec((1, L, rw), lambda i, c: (i, c, 0)),
                  pl.BlockSpec((1, L, rw), lambda i, c: (i, c, 1)),
                  pl.BlockSpec((1, L, rw), lambda i, c: (i, c, 2)),
                  pl.BlockSpec((1, L, RG_PAD), lambda i, c: (i, c, COL_RG // RG_PAD)),
                  pl.BlockSpec((1, L, WA_W), lambda i, c: (i, c, COL_WA // WA_W)),
                  const((PV_ROWS, rw)), const((1, RG_PAD)), const((1, WA_W)),
                  const((RWKV_DECAY_RANK, rw)), const((RWKV_A_RANK, rw)), const((RG_PAD, rw)),
                  const((RWKV_GROUP_W, RWKV_GROUP_W))],
        out_specs=[pl.BlockSpec((1, L, rw), lambda i, c: (i, c, 0))],
        scratch_shapes=[pltpu.VMEM((RWKV_GROUPS, RWKV_HEAD, RWKV_GROUP_W), F32),
                        pltpu.VMEM((SUBLANES, rw), F32), pltpu.VMEM((SUBLANES, rw), F32),
                        pltpu.VMEM((SUBLANES, rw), F32), pltpu.VMEM((SUBLANES, RG_PAD), F32),
                        pltpu.VMEM((SUBLANES, WA_W), F32)],
        compiler_params=_params(("parallel", "arbitrary")),
        name="rwkv7",
    )(main3, main3, main3, small3, small3, pvec, mu_g, mu_wa, w_up, a_up, g_up, seg)


def _mix_out_kernel(ha_ref, hb_ref, ga_ref, gb_ref, wa_ref, wb_ref, wo_ref, g_ref, x_ref, o_ref):
    ya = jnp.dot(ha_ref[...], wa_ref[...], preferred_element_type=F32)
    yb = jnp.dot(hb_ref[...], wb_ref[...], preferred_element_type=F32)
    merged = _bf(_sigmoid(ga_ref[...].astype(F32)) * ya + _sigmoid(gb_ref[...].astype(F32)) * yb)
    y = jnp.dot(merged, wo_ref[...], preferred_element_type=F32)
    o_ref[...] = x_ref[...] + _rms(y, g_ref[...])


def _mix_out(ha, hb, gates, wa, wb, wo, g, x, *, tm):
    t, kdim = ha.shape
    d = wo.shape[1]
    resident = lambda shape: pl.BlockSpec(shape, lambda i: (0, 0), pipeline_mode=pl.Buffered(1))
    return pl.pallas_call(
        _mix_out_kernel,
        out_shape=jax.ShapeDtypeStruct((t, d), F32),
        grid=(t // tm,),
        in_specs=[pl.BlockSpec((tm, kdim), lambda i: (i, 0)),
                  pl.BlockSpec((tm, kdim), lambda i: (i, 0)),
                  pl.BlockSpec((tm, d), lambda i: (i, 0)),
                  pl.BlockSpec((tm, d), lambda i: (i, 1)),
                  resident((kdim, d)), resident((kdim, d)), resident((d, d)),
                  pl.BlockSpec((1, d), lambda i: (0, 0)),
                  pl.BlockSpec((tm, d), lambda i: (i, 0))],
        out_specs=pl.BlockSpec((tm, d), lambda i: (i, 0)),
        compiler_params=_params(("parallel",)),
        name="mix_out",
    )(ha, hb, gates, gates, wa, wb, wo, g, x)


def _xattn_kernel(x_ref, gpre_ref, wq_ref, kv_ref, wo_ref, gpost_ref, o_ref):
    x = x_ref[...]
    h = _bf(_rms(x, gpre_ref[...]))
    q = jnp.dot(h, wq_ref[...], preferred_element_type=F32)
    scale = XATTN_HEAD_DIM ** -0.5
    outs = []
    for hd in range(XATTN_HEADS):
        sl = slice(hd * XATTN_HEAD_DIM, (hd + 1) * XATTN_HEAD_DIM)
        k = kv_ref[0, :, sl]
        v = kv_ref[0, :, XATTN_WIDTH + hd * XATTN_HEAD_DIM:XATTN_WIDTH + (hd + 1) * XATTN_HEAD_DIM]
        sc = _mm_nt(q[:, sl], k) * scale
        sc = sc - jnp.max(sc, axis=-1, keepdims=True)
        e = jnp.exp(sc)
        p = e / jnp.sum(e, axis=-1, keepdims=True)
        outs.append(_mm(p, v))
    o = jnp.concatenate(outs, axis=-1)
    y = jnp.dot(_bf(o), wo_ref[...], preferred_element_type=F32)
    o_ref[...] = x + _rms(y, gpost_ref[...])


def _xattn(x, gpre, wq, kv3, wo, gpost, *, tm, seq):
    t, d = x.shape
    per_seq = seq // tm
    return pl.pallas_call(
        _xattn_kernel,
        out_shape=jax.ShapeDtypeStruct((t, d), F32),
        grid=(t // tm,),
        in_specs=[pl.BlockSpec((tm, d), lambda i: (i, 0)),
                  pl.BlockSpec((1, d), lambda i: (0, 0)),
                  pl.BlockSpec((d, XATTN_WIDTH), lambda i: (0, 0)),
                  pl.BlockSpec((1, MEM_LEN, 2 * XATTN_WIDTH), lambda i: (i // per_seq, 0, 0)),
                  pl.BlockSpec((XATTN_WIDTH, d), lambda i: (0, 0)),
                  pl.BlockSpec((1, d), lambda i: (0, 0))],
        out_specs=pl.BlockSpec((tm, d), lambda i: (i, 0)),
        compiler_params=_params(("parallel",)),
        name="xattn",
    )(x, gpre, wq, kv3, wo, gpost)


FFN_TAIL = SUBLANES


def _gelu_tanh(x):
    return 0.5 * x * (1.0 + jnp.tanh(0.7978845608028654 * (x + 0.044715 * x * x * x)))


def _ffn_kernel(x_ref, gpre_ref, wg_ref, wu_ref, cwg_ref, cwu_ref, cbg_ref, cbu_ref, wd_ref, gpost_ref,
                o_ref, h_ref, tail_ref, *, tiles_per_seq):
    i = pl.program_id(0)
    j = pl.program_id(1)
    tm, tn = x_ref.shape[0], wg_ref.shape[1]

    @pl.when((i == 0) & (j == 0))
    def _():
        tail_ref[...] = jnp.zeros_like(tail_ref)

    @pl.when(j == 0)
    def _():
        h_ref[...] = _bf(_rms(x_ref[...], gpre_ref[...]))
        o_ref[...] = jnp.zeros_like(o_ref)

    prev = jnp.where(i % tiles_per_seq == 0, 0.0, tail_ref[j])
    ug = jnp.dot(h_ref[...], wg_ref[...], preferred_element_type=F32)
    uu = jnp.dot(h_ref[...], wu_ref[...], preferred_element_type=F32)
    tail_ref[j] = jnp.concatenate([ug[tm - FFN_TAIL:, :], uu[tm - FFN_TAIL:, :]], axis=1)

    def conv(u, u_prev, cw_ref, cb_ref):
        u = jnp.concatenate([u_prev, u], axis=0)
        u1 = pltpu.roll(u, 1, 0)
        u2 = pltpu.roll(u, 2, 0)
        cw = cw_ref[...]
        full = cb_ref[...] + cw[2:3, :] * u + cw[1:2, :] * u1 + cw[0:1, :] * u2
        return full[FFN_TAIL:, :]

    gate = conv(ug, prev[:, :tn], cwg_ref, cbg_ref)
    up = conv(uu, prev[:, tn:], cwu_ref, cbu_ref)
    act = _bf(_gelu_tanh(gate) * up)
    o_ref[...] += jnp.dot(act, wd_ref[...], preferred_element_type=F32)

    @pl.when(j == pl.num_programs(1) - 1)
    def _():
        o_ref[...] = x_ref[...] + _rms(o_ref[...], gpost_ref[...])


def _ffn(x, gpre, w_up, conv_w, conv_b, w_down, gpost, *, tm, tn, seq):
    t, d = x.shape
    nj = D_FF // tn
    return pl.pallas_call(
        functools.partial(_ffn_kernel, tiles_per_seq=seq // tm),
        out_shape=jax.ShapeDtypeStruct((t, d), F32),
        grid=(t // tm, nj),
        in_specs=[pl.BlockSpec((tm, d), lambda i, j: (i, 0)),
                  pl.BlockSpec((1, d), lambda i, j: (0, 0)),
                  pl.BlockSpec((d, tn), lambda i, j: (0, j)),
                  pl.BlockSpec((d, tn), lambda i, j: (0, nj + j)),
                  pl.BlockSpec((CONV_WIDTH, tn), lambda i, j: (0, j)),
                  pl.BlockSpec((CONV_WIDTH, tn), lambda i, j: (0, nj + j)),
                  pl.BlockSpec((1, tn), lambda i, j: (0, j)),
                  pl.BlockSpec((1, tn), lambda i, j: (0, nj + j)),
                  pl.BlockSpec((tn, d), lambda i, j: (j, 0)),
                  pl.BlockSpec((1, d), lambda i, j: (0, 0))],
        out_specs=pl.BlockSpec((tm, d), lambda i, j: (i, 0), pipeline_mode=pl.Buffered(1)),
        scratch_shapes=[pltpu.VMEM((tm, d), BF16), pltpu.VMEM((nj, FFN_TAIL, 2 * tn), F32)],
        compiler_params=_params(("arbitrary", "arbitrary")),
        name="conv_glu_ffn",
    )(x, gpre, w_up, w_up, conv_w, conv_w, conv_b, conv_b, w_down, gpost)


def _small_in_proj(w_in):
    d = w_in.shape[1]
    r0, g0 = MLSTM_TOTAL, MLSTM_TOTAL + RWKV_TOTAL
    mlstm_if = w_in[MLSTM_MAIN:r0]
    rwkv_wa = w_in[r0 + RWKV_MAIN:r0 + RWKV_MAIN + WA_W]
    rwkv_g = w_in[r0 + RWKV_MAIN + WA_W:g0]
    z = lambda n: jnp.zeros((n, d), w_in.dtype)
    return jnp.concatenate([rwkv_g, z(RG_PAD - RWKV_GATE_RANK), rwkv_wa, mlstm_if, z(IF_W - 2 * MLSTM_HEADS)], axis=0)


def _layer(x, mem, mix_pre_norm, w_in, mlstm_b_i, mlstm_b_f, mlstm_head_norm, rwkv_mu, rwkv_w0, rwkv_w_up, rwkv_a0,
           rwkv_a_up, rwkv_g_up, rwkv_k_k, rwkv_k_a, rwkv_r_k, rwkv_ln_g, rwkv_ln_b, w_branch_a, w_branch_b,
           w_mix_out, mix_post_norm, xattn_pre_norm, mem_norm, xattn_wq, xattn_wkv, xattn_wo, xattn_post_norm,
           ffn_pre_norm, ffn_w_up, ffn_conv_w, ffn_conv_b, ffn_w_down, ffn_post_norm):
    b, s, d = x.shape
    t = b * s
    x2d = x.reshape(t, d)
    row = lambda p: p.reshape(1, -1).astype(F32)
    tm_row = min(TM_ROW, s)

    w_in_t = w_in.T
    tm_in = min(TM_IN_PROJ, t)
    small, h = _norm_matmul(x2d, row(mix_pre_norm), _small_in_proj(w_in_t), tm=tm_in, tn=SMALL_COLS, out_dtype=F32,
                            name="in_proj_small", w_is_transposed=True)
    small3 = small.reshape(b, s, SMALL_COLS)
    wide = functools.partial(_rowrange_matmul, h, w_in_t, tm=tm_in, tn=TN_IN_PROJ, out_dtype=BF16)
    (mlstm_main,) = wide(row0=0, nrows=MLSTM_MAIN, name="in_proj_mlstm")
    (rwkv_main,) = wide(row0=MLSTM_TOTAL, nrows=RWKV_MAIN, name="in_proj_rwkv")
    gates, w_a_b, w_b_b, w_out_b = wide(row0=MLSTM_TOTAL + RWKV_TOTAL, nrows=2 * D_MODEL, name="in_proj_gates",
                                        cast_arrays=(w_branch_a, w_branch_b, w_mix_out))

    gate_bias = jnp.concatenate([mlstm_b_i, mlstm_b_f, jnp.zeros((IF_W - 2 * MLSTM_HEADS,), F32)]).reshape(1, IF_W)
    ha, ffn_w_down_b = _mlstm(mlstm_main.reshape(b, s, MLSTM_MAIN), small3, gate_bias, row(mlstm_head_norm),
                              cast_arrays=(ffn_w_down,))

    mu_r, mu_k, mu_v = (rwkv_mu[i * RWKV_WIDTH:(i + 1) * RWKV_WIDTH] for i in range(3))
    mu_wa = rwkv_mu[RWKV_MAIN:RWKV_MAIN + WA_W]
    mu_g = jnp.pad(rwkv_mu[RWKV_MAIN + WA_W:], (0, RG_PAD - RWKV_GATE_RANK))
    pvec = jnp.stack([mu_r, mu_k, mu_v, rwkv_w0, rwkv_a0, rwkv_k_k, rwkv_k_a, rwkv_r_k.reshape(-1), rwkv_ln_g,
                      rwkv_ln_b] + [jnp.zeros((RWKV_WIDTH,), F32)] * (PV_ROWS - 10))
    g_up = jnp.pad(rwkv_g_up, ((0, RG_PAD - RWKV_GATE_RANK), (0, 0)))
    hb, ffn_w_up_b = _rwkv(rwkv_main.reshape(b, s, RWKV_MAIN), small3, pvec, row(mu_g), row(mu_wa), _bf(rwkv_w_up),
                           _bf(rwkv_a_up), _bf(g_up), cast_arrays=(ffn_w_up,))

    x1 = _mix_out(ha.reshape(t, -1), hb.reshape(t, -1), gates, w_a_b, w_b_b, w_out_b, row(mix_post_norm), x2d,
                  tm=tm_row)

    mem2d = mem.reshape(b * MEM_LEN, d)
    kv, _ = _norm_matmul(mem2d, row(mem_norm), _bf(xattn_wkv), tm=MEM_LEN, tn=TN, out_dtype=F32, name="mem_kv")
    x2 = _xattn(x1, row(xattn_pre_norm), _bf(xattn_wq), kv.reshape(b, MEM_LEN, 2 * XATTN_WIDTH), _bf(xattn_wo),
                row(xattn_post_norm), tm=min(TM_XATTN, s), seq=s)

    x3 = _ffn(x2, row(ffn_pre_norm), ffn_w_up_b, ffn_conv_w, row(ffn_conv_b), ffn_w_down_b, row(ffn_post_norm),
              tm=min(TM_FFN, s), tn=TN, seq=s)
    return x3.reshape(b, s, d)


def kernel(x, mem, mix_pre_norm, w_in, mlstm_b_i, mlstm_b_f, mlstm_head_norm, rwkv_mu, rwkv_w0, rwkv_w_up, rwkv_a0, rwkv_a_up, rwkv_g_up, rwkv_k_k, rwkv_k_a, rwkv_r_k, rwkv_ln_g, rwkv_ln_b, w_branch_a, w_branch_b, w_mix_out, mix_post_norm, xattn_pre_norm, mem_norm, xattn_wq, xattn_wkv, xattn_wo, xattn_post_norm, ffn_pre_norm, ffn_w_up, ffn_conv_w, ffn_conv_b, ffn_w_down, ffn_post_norm):
    for l in range(mix_pre_norm.shape[0]):
        x = _layer(x, mem, mix_pre_norm[l], w_in[l], mlstm_b_i[l], mlstm_b_f[l], mlstm_head_norm[l], rwkv_mu[l],
                   rwkv_w0[l], rwkv_w_up[l], rwkv_a0[l], rwkv_a_up[l], rwkv_g_up[l], rwkv_k_k[l], rwkv_k_a[l],
                   rwkv_r_k[l], rwkv_ln_g[l], rwkv_ln_b[l], w_branch_a[l], w_branch_b[l], w_mix_out[l],
                   mix_post_norm[l], xattn_pre_norm[l], mem_norm[l], xattn_wq[l], xattn_wkv[l], xattn_wo[l],
                   xattn_post_norm[l], ffn_pre_norm[l], ffn_w_up[l], ffn_conv_w[l], ffn_conv_b[l], ffn_w_down[l],
                   ffn_post_norm[l])
    return x
```

```python
import functools

import jax
import jax.numpy as jnp
from jax import lax
from jax.experimental import pallas as pl
from jax.experimental.pallas import tpu as pltpu

D_MODEL = 2048
MEM_LEN = 256
RMS_EPS = 1e-6

MLSTM_HEADS = 4
MLSTM_WIDTH = D_MODEL // 2
MLSTM_V_DIM = MLSTM_WIDTH // MLSTM_HEADS
MLSTM_QK_DIM = MLSTM_V_DIM // 2
MLSTM_QK_WIDTH = MLSTM_HEADS * MLSTM_QK_DIM
GATE_SOFTCAP = 15.0

RWKV_WIDTH = D_MODEL // 2
RWKV_HEAD = 64
RWKV_HEADS = RWKV_WIDTH // RWKV_HEAD
RWKV_DECAY_RANK = 64
RWKV_A_RANK = 64
RWKV_GATE_RANK = 160
RWKV_GN_EPS = 64e-5

MLSTM_MAIN = 2 * MLSTM_QK_WIDTH + 2 * MLSTM_WIDTH
MLSTM_TOTAL = MLSTM_MAIN + 2 * MLSTM_HEADS
RWKV_MAIN = 3 * RWKV_WIDTH
RWKV_TOTAL = RWKV_MAIN + RWKV_DECAY_RANK + RWKV_A_RANK + RWKV_GATE_RANK

XATTN_HEADS = 4
XATTN_HEAD_DIM = 128
XATTN_WIDTH = XATTN_HEADS * XATTN_HEAD_DIM

D_FF = 4 * D_MODEL
CONV_WIDTH = 3

RG_PAD = 256
WA_W = RWKV_DECAY_RANK + RWKV_A_RANK
IF_W = 128
COL_RG = 0
COL_WA = COL_RG + RG_PAD
COL_IF = COL_WA + WA_W
SMALL_COLS = COL_IF + IF_W

MLSTM_CHUNK = 256
RWKV_CHUNK = 64
RWKV_CHUNKS_PER_STEP = 2
RWKV_GROUP_W = 256
RWKV_GROUPS = RWKV_WIDTH // RWKV_GROUP_W
VMEM_LIMIT = 56 * 1024 * 1024

LANES = 128
SUBLANES = 8
BF16_SUBLANES = 16
TM_IN_PROJ = 1024
TN_IN_PROJ = 1024
TM_ROW = 512
TM_XATTN = 1024
TM_FFN = 1024
TN = 512

F32 = jnp.float32
BF16 = jnp.bfloat16


def _bf(x):
    return x.astype(BF16)


def _mm(a, b):
    return jnp.dot(_bf(a), _bf(b), preferred_element_type=F32)


def _mm_nt(a, b):
    return lax.dot_general(_bf(a), _bf(b), (((1,), (1,)), ((), ())), preferred_element_type=F32)


def _mm_tn(a, b):
    return lax.dot_general(_bf(a), _bf(b), (((0,), (0,)), ((), ())), preferred_element_type=F32)


def _mm_exact_lhs(tri, x):
    hi = _bf(x)
    r1 = x - hi.astype(F32)
    mid = _bf(r1)
    lo = _bf(r1 - mid.astype(F32))
    t = _bf(tri)
    return (jnp.dot(t, hi, preferred_element_type=F32) + jnp.dot(t, mid, preferred_element_type=F32)
            + jnp.dot(t, lo, preferred_element_type=F32))


def _rms(x, g):
    return x * lax.rsqrt(jnp.mean(x * x, axis=-1, keepdims=True) + RMS_EPS) * g


def _softplus(x):
    return jnp.maximum(x, 0.0) + jnp.log(1.0 + jnp.exp(-jnp.abs(x)))


def _sigmoid(x):
    return 1.0 / (1.0 + jnp.exp(-x))


def _params(sem):
    return pltpu.CompilerParams(dimension_semantics=sem, vmem_limit_bytes=VMEM_LIMIT)


def _call_with_casts(body, *, n_in, out_shape, grid, in_specs, out_specs, cast_arrays=(), **kwargs):
    n_out = len(out_shape)
    k = len(cast_arrays)
    nsteps = 1
    for extent in grid:
        nsteps *= extent

    def step(*ids):
        lin = ids[0]
        for extent, idx in zip(grid[1:], ids[1:]):
            lin = lin * extent + idx
        return lin

    band_specs, band_shapes = [], []
    for arr in cast_arrays:
        rows, cols = arr.shape
        assert rows % nsteps == 0 and (rows // nsteps) % BF16_SUBLANES == 0, (arr.shape, nsteps)
        band_specs.append(pl.BlockSpec((rows // nsteps, cols), lambda *ids: (step(*ids), 0)))
        band_shapes.append(jax.ShapeDtypeStruct(arr.shape, BF16))

    def kernel(*refs):
        ins, src = refs[:n_in], refs[n_in:n_in + k]
        outs, dst = refs[n_in + k:n_in + k + n_out], refs[n_in + k + n_out:n_in + 2 * k + n_out]
        for s_ref, d_ref in zip(src, dst):
            d_ref[...] = s_ref[...].astype(d_ref.dtype)
        body(*ins, *outs, *refs[n_in + 2 * k + n_out:])

    def run(*operands):
        assert len(operands) == n_in
        return pl.pallas_call(kernel, out_shape=tuple(out_shape) + tuple(band_shapes), grid=grid,
                              in_specs=list(in_specs) + band_specs, out_specs=tuple(out_specs) + tuple(band_specs),
                              **kwargs)(*operands, *cast_arrays)

    return run


def _norm_matmul_kernel(x_ref, g_ref, w_ref, o_ref, h_ref, *, w_is_transposed):
    @pl.when(pl.program_id(1) == 0)
    def _():
        h_ref[...] = _bf(_rms(x_ref[...], g_ref[...]))

    w = w_ref[...].T if w_is_transposed else w_ref[...]
    o_ref[...] = jnp.dot(h_ref[...], _bf(w), preferred_element_type=F32).astype(o_ref.dtype)


def _norm_matmul(x, g, w, *, tm, tn, out_dtype, name, w_is_transposed=False):
    t, d = x.shape
    n = w.shape[0] if w_is_transposed else w.shape[1]
    w_spec = pl.BlockSpec((tn, d), lambda i, j: (j, 0)) if w_is_transposed else pl.BlockSpec((d, tn), lambda i, j: (0, j))
    return pl.pallas_call(
        functools.partial(_norm_matmul_kernel, w_is_transposed=w_is_transposed),
        out_shape=(jax.ShapeDtypeStruct((t, n), out_dtype), jax.ShapeDtypeStruct((t, d), BF16)),
        grid=(t // tm, n // tn),
        in_specs=[pl.BlockSpec((tm, d), lambda i, j: (i, 0)),
                  pl.BlockSpec((1, d), lambda i, j: (0, 0)),
                  w_spec],
        out_specs=(pl.BlockSpec((tm, tn), lambda i, j: (i, j)),
                   pl.BlockSpec((tm, d), lambda i, j: (i, 0))),
        compiler_params=_params(("parallel", "arbitrary")),
        name=name,
    )(x, g, w)


def _rowranges_matmul_kernel(a_ref, wt_ref, *refs, tile_ranges):
    o_refs, wb_ref = refs[:-1], refs[-1]
    j = pl.program_id(0)

    @pl.when(pl.program_id(1) == 0)
    def _():
        wb_ref[...] = _bf(wt_ref[...].T)

    for o_ref, (first, stop) in zip(o_refs, tile_ranges):
        @pl.when((j >= first) & (j < stop))
        def _(o_ref=o_ref):
            o_ref[...] = jnp.dot(a_ref[...], wb_ref[...], preferred_element_type=F32).astype(o_ref.dtype)


def _rowranges_matmul(a, w_t, ranges, *, tm, tn, out_dtype, name):
    t, d = a.shape
    n_i = t // tm
    tile_ranges, first = [], 0
    for row0, nrows in ranges:
        assert row0 % SUBLANES == 0 and nrows % tn == 0
        tile_ranges.append((first, first + nrows // tn))
        first += nrows // tn

    def weight_row(j, i):
        row = 0
        for (row0, _), (lo, hi) in zip(ranges, tile_ranges):
            row = row + jnp.where((j >= lo) & (j < hi), row0 + tn * (j - lo), 0)
        return pl.multiple_of(row, SUBLANES), 0

    def out_index(lo, hi):
        return lambda j, i: (jnp.where(j < lo, 0, jnp.where(j >= hi, n_i - 1, i)), jnp.clip(j - lo, 0, hi - lo - 1))

    return pl.pallas_call(
        functools.partial(_rowranges_matmul_kernel, tile_ranges=tuple(tile_ranges)),
        out_shape=[jax.ShapeDtypeStruct((t, nrows), out_dtype) for _, nrows in ranges],
        grid=(first, n_i),
        in_specs=[pl.BlockSpec((tm, d), lambda j, i: (i, 0)),
                  pl.BlockSpec((pl.Element(tn), pl.Element(d)), weight_row)],
        out_specs=[pl.BlockSpec((tm, tn), out_index(lo, hi)) for lo, hi in tile_ranges],
        scratch_shapes=[pltpu.VMEM((d, tn), BF16)],
        compiler_params=_params(("arbitrary", "arbitrary")),
        name=name,
    )(a, w_t)


def _mlstm_kernel(q_ref, k_ref, v_ref, o_ref, gate_ref, bias_ref, hn_ref, out_ref, ct_ref, n_ref, m_ref):
    L = MLSTM_CHUNK
    dk, dv = MLSTM_QK_DIM, MLSTM_V_DIM

    @pl.when(pl.program_id(1) == 0)
    def _():
        ct_ref[...] = jnp.zeros_like(ct_ref)
        n_ref[...] = jnp.zeros_like(n_ref)
        m_ref[...] = jnp.zeros_like(m_ref)

    pre = gate_ref[0] + bias_ref[...]
    capped = GATE_SOFTCAP * jnp.tanh(pre / GATE_SOFTCAP)
    logf = -_softplus(-capped)
    row = lax.broadcasted_iota(jnp.int32, (L, L), 0)
    col = lax.broadcasted_iota(jnp.int32, (L, L), 1)
    causal = row >= col
    bcum = _mm_exact_lhs(causal.astype(F32), logf)
    ig_t = capped.T
    bcum_t = bcum.T
    scale = MLSTM_QK_DIM ** -0.5

    H = range(MLSTM_HEADS)
    q = [q_ref[0, :, h * dk:(h + 1) * dk] for h in H]
    k = [k_ref[0, :, h * dk:(h + 1) * dk] for h in H]
    v = [v_ref[0, :, h * dv:(h + 1) * dv] for h in H]
    b_c = [bcum[:, MLSTM_HEADS + h:MLSTM_HEADS + h + 1] for h in H]
    b_r = [bcum_t[MLSTM_HEADS + h:MLSTM_HEADS + h + 1, :] for h in H]
    i_c = [capped[:, h:h + 1] for h in H]
    i_r = [ig_t[h:h + 1, :] for h in H]
    m_prev = [m_ref[h][0:1, 0:1] for h in H]
    ct = [ct_ref[h] for h in H]
    nrow = [n_ref[h][0:1, :] for h in H]

    dmat = [jnp.where(causal, b_c[h] - b_r[h] + i_r[h], -jnp.inf) for h in H]
    inter = [b_c[h] + m_prev[h] for h in H]
    m_t = [jnp.maximum(inter[h], jnp.max(dmat[h], axis=-1, keepdims=True)) for h in H]
    dexp = [jnp.exp(dmat[h] - m_t[h]) * scale for h in H]
    w_inter = [jnp.exp(inter[h] - m_t[h]) for h in H]
    qk = [_mm_nt(q[h], k[h]) for h in H]
    qc = [_mm(q[h], ct[h]) for h in H]
    s = [qk[h] * dexp[h] for h in H]
    sv = [_mm(s[h], v[h]) for h in H]
    num = [w_inter[h] * qc[h] + sv[h] for h in H]
    den = [w_inter[h] * jnp.sum(q[h].astype(F32) * nrow[h], axis=-1, keepdims=True)
           + jnp.sum(s[h], axis=-1, keepdims=True) for h in H]
    hh = [num[h] / jnp.maximum(jnp.abs(den[h]), jnp.exp(-m_t[h])) for h in H]

    b_last = [b_c[h][L - 1:L, :] for h in H]
    gs = [b_last[h] - b_c[h] + i_c[h] for h in H]
    m_new = [jnp.maximum(b_last[h] + m_prev[h], jnp.max(gs[h], axis=0, keepdims=True)) for h in H]
    carry_w = [jnp.exp(b_last[h] + m_prev[h] - m_new[h]) for h in H]
    ws = [jnp.exp(gs[h] - m_new[h]) * scale for h in H]
    kv = [_mm_tn(k[h], ws[h] * v[h].astype(F32)) for h in H]
    for h in H:
        ct_ref[h] = carry_w[h] * ct[h] + kv[h]
        n_ref[h] = jnp.broadcast_to(
            carry_w[h] * nrow[h] + jnp.sum(ws[h] * k[h].astype(F32), axis=0, keepdims=True), n_ref.shape[1:])
        m_ref[h] = jnp.broadcast_to(m_new[h], m_ref.shape[1:])

    ms = [jnp.mean(hh[h] * hh[h], axis=-1, keepdims=True) for h in H]
    for h in H:
        hm = hh[h] * lax.rsqrt(ms[h] + RMS_EPS) * hn_ref[:, h * dv:(h + 1) * dv]
        og = o_ref[0, :, h * dv:(h + 1) * dv].astype(F32)
        out_ref[0, :, h * dv:(h + 1) * dv] = (_sigmoid(og) * hm).astype(out_ref.dtype)


def _mlstm(main3, small3, gate_bias, head_norm, cast_arrays=()):
    b, s, _ = main3.shape
    L = min(MLSTM_CHUNK, s)
    assert L == MLSTM_CHUNK and s % L == 0
    qw, vw = MLSTM_QK_WIDTH, MLSTM_WIDTH
    return _call_with_casts(
        _mlstm_kernel, n_in=7, cast_arrays=cast_arrays,
        out_shape=[jax.ShapeDtypeStruct((b, s, MLSTM_WIDTH), BF16)],
        grid=(b, s // L),
        in_specs=[pl.BlockSpec((1, L, qw), lambda i, c: (i, c, 0)),
                  pl.BlockSpec((1, L, qw), lambda i, c: (i, c, 1)),
                  pl.BlockSpec((1, L, vw), lambda i, c: (i, c, 1)),
                  pl.BlockSpec((1, L, vw), lambda i, c: (i, c, 2)),
                  pl.BlockSpec((1, L, IF_W), lambda i, c: (i, c, COL_IF // IF_W)),
                  pl.BlockSpec((1, IF_W), lambda i, c: (0, 0)),
                  pl.BlockSpec((1, vw), lambda i, c: (0, 0))],
        out_specs=[pl.BlockSpec((1, L, vw), lambda i, c: (i, c, 0))],
        scratch_shapes=[pltpu.VMEM((MLSTM_HEADS, MLSTM_QK_DIM, MLSTM_V_DIM), F32),
                        pltpu.VMEM((MLSTM_HEADS, SUBLANES, MLSTM_QK_DIM), F32),
                        pltpu.VMEM((MLSTM_HEADS, SUBLANES, LANES), F32)],
        compiler_params=_params(("parallel", "arbitrary")),
        name="mlstm",
    )(main3, main3, main3, main3, small3, gate_bias, head_norm)


PV_MU_R, PV_MU_K, PV_MU_V, PV_W0, PV_A0, PV_KK, PV_KA, PV_RK, PV_LNG, PV_LNB = range(10)
PV_ROWS = 16


def _rwkv_kernel(r_ref, k_ref, v_ref, g_ref, wa_ref, pv_ref, mug_ref, muwa_ref, wup_ref, aup_ref, gup_ref, seg_ref,
                 out_ref, s_ref, pr_ref, pk_ref, pvv_ref, pg_ref, pwa_ref):
    L = RWKV_CHUNK
    CH = RWKV_CHUNKS_PER_STEP
    R = CH * L
    N = RWKV_HEAD
    GW = RWKV_GROUP_W
    GH = GW // N

    @pl.when(pl.program_id(1) == 0)
    def _():
        s_ref[...] = jnp.zeros_like(s_ref)
        pr_ref[...] = jnp.zeros_like(pr_ref)
        pk_ref[...] = jnp.zeros_like(pk_ref)
        pvv_ref[...] = jnp.zeros_like(pvv_ref)
        pg_ref[...] = jnp.zeros_like(pg_ref)
        pwa_ref[...] = jnp.zeros_like(pwa_ref)

    groups = range(RWKV_GROUPS)
    rows = lambda c: slice(c * L, (c + 1) * L)
    lanes = lambda g: slice(g * GW, (g + 1) * GW)
    cat = lambda u, w_: jnp.concatenate([u, w_], axis=0)
    seg = seg_ref[...]
    pv = pv_ref[...]

    def shift_lerp(x_ref, prev_ref, mu, ls=slice(None)):
        x = x_ref[0, :, ls].astype(F32)
        rid = lax.broadcasted_iota(jnp.int32, x.shape, 0)
        xs = jnp.where(rid == 0, prev_ref[0:1, ls], pltpu.roll(x, 1, 0))
        prev_ref[0:1, ls] = x[R - 1:R, :]
        return x + (xs - x) * mu

    gl = shift_lerp(g_ref, pg_ref, mug_ref[...])
    wa = shift_lerp(wa_ref, pwa_ref, muwa_ref[...])
    wl_b = _bf(jnp.tanh(wa[:, 0:RWKV_DECAY_RANK]))
    al_b = _bf(wa[:, RWKV_DECAY_RANK:WA_W])
    sgl_b = _bf(_sigmoid(gl))
    tr = lax.broadcasted_iota(jnp.int32, (R, R), 0)
    tc = lax.broadcasted_iota(jnp.int32, (R, R), 1)
    tri = ((tr >= tc) & ((tr // L) == (tc // L))).astype(F32)

    ti = lax.broadcasted_iota(jnp.int32, (L, GW), 0)
    lane = lax.broadcasted_iota(jnp.int32, (L, GW), 1)
    si = lane & (N - 1)
    lane_head = lane // N
    incl = ti >= si
    strict = ti > si
    eye = (ti == si).astype(F32)
    lvl_masks = []
    bsz = 1
    while bsz < L:
        same = (ti // (2 * bsz)) == (si // (2 * bsz))
        lvl_masks.append(jnp.where(same & ((ti & bsz) != 0) & ((si & bsz) == 0), 1.0, 0.0))
        bsz *= 2

    def bd(x):
        return jnp.concatenate([_bf(x)] * GH, axis=0) * seg

    def fold(z):
        acc = z[0:N]
        for hh in range(1, GH):
            acc = jnp.where(lane_head == hh, z[hh * N:(hh + 1) * N], acc)
        return acc

    def mmb(x, b):
        return jnp.dot(_bf(x), b, preferred_element_type=F32)

    def mmb_nt(x, b):
        return lax.dot_general(_bf(x), b, (((1,), (1,)), ((), ())), preferred_element_type=F32)

    prep = []
    for g in groups:
        ls = lanes(g)
        row = lambda i, ls=ls: pv[i:i + 1, ls]
        r = shift_lerp(r_ref, pr_ref, row(PV_MU_R), ls)
        kr = shift_lerp(k_ref, pk_ref, row(PV_MU_K), ls)
        v = shift_lerp(v_ref, pvv_ref, row(PV_MU_V), ls)
        w_log = -_softplus(-(row(PV_W0) + jnp.dot(wl_b, wup_ref[:, ls], preferred_element_type=F32))) - 0.5
        lw = -jnp.exp(w_log)
        a = _sigmoid(row(PV_A0) + jnp.dot(al_b, aup_ref[:, ls], preferred_element_type=F32))
        gate = jnp.dot(sgl_b, gup_ref[:, ls], preferred_element_type=F32)
        cs = _mm_exact_lhs(tri, lw)
        g_incl = jnp.exp(cs)
        g_inv = jnp.exp(-cs)
        kk0 = kr * row(PV_KK)
        kr2 = kr * (1.0 + (a - 1.0) * row(PV_KA))
        sums = jnp.dot(_bf(cat(kk0 * kk0, r * kr2 * row(PV_RK))), seg, preferred_element_type=F32)
        kk = kk0 * lax.rsqrt(jnp.maximum(sums[:R], 1e-24))
        prep.append(dict(p=jnp.exp(cs - lw) * kk, qt=kk * a * g_inv, kt=kr2 * g_inv, rt=r * g_incl, v=v,
                         bonus_v=sums[R:] * v, gate=gate, g_incl=g_incl,
                         ln_g=row(PV_LNG), ln_b=row(PV_LNB)))

    E = [(c, g) for c in range(CH) for g in groups]
    n_e = range(len(E))
    blk = lambda name: [prep[g][name][rows(c), :] for c, g in E]
    p, qt, kt, rt, vv, bonus_v, gate = (blk(n_) for n_ in ("p", "qt", "kt", "rt", "v", "bonus_v", "gate"))
    gl_ = [prep[g]["g_incl"][c * L + L - 1:c * L + L, :] for c, g in E]
    qg = [qt[e] * gl_[e] for e in n_e]
    kg = [kt[e] * gl_[e] for e in n_e]

    lhs_pr = [cat(p[e], rt[e]) for e in n_e]
    gq = [mmb_nt(lhs_pr[e], bd(qt[e])) for e in n_e]
    gk = [mmb_nt(lhs_pr[e], bd(kt[e])) for e in n_e]
    n_pq = [jnp.where(strict, gq[e][:L], 0.0) for e in n_e]
    a_rq = [jnp.where(incl, gq[e][L:], 0.0) for e in n_e]
    a_pk = [jnp.where(strict, gk[e][:L], 0.0) for e in n_e]
    a_rk = [jnp.where(incl, gk[e][L:], 0.0) for e in n_e]

    x = [eye - n_pq[e] * lvl_masks[0] for e in n_e]
    bd_n = [bd(n_pq[e]) for e in n_e]
    for msk in lvl_masks[1:]:
        t1 = [mmb(x[e], bd_n[e]) for e in n_e]
        x = [x[e] - msk * mmb(t1[e], bd(x[e])) for e in n_e]

    av = [mmb(cat(a_pk[e], a_rk[e]), bd(vv[e])) for e in n_e]
    w = [mmb(x[e], bd(p[e])) for e in n_e]
    u0 = [mmb(x[e], bd(av[e][:L])) for e in n_e]
    gmat = [rt[e] - mmb(a_rq[e], bd(w[e])) for e in n_e]
    y0 = [av[e][L:] - mmb(a_rq[e], bd(u0[e])) for e in n_e]
    mt = [eye * gl_[e] - fold(_mm_tn(qg[e], w[e])) for e in n_e]
    bt = [fold(_mm_tn(cat(kg[e], -qg[e]), cat(vv[e], u0[e]))) for e in n_e]

    state = [s_ref[g] for g in groups]
    y = [None] * len(E)
    for c in range(CH):
        es = [c * RWKV_GROUPS + g for g in groups]
        ys = [mmb(cat(gmat[e], mt[e]), bd(state[g])) for g, e in zip(groups, es)]
        for g, e in zip(groups, es):
            y[e] = ys[g][:L] + y0[e]
            state[g] = ys[g][L:] + bt[e]
    for g in groups:
        s_ref[g] = state[g]

    inv_n = 1.0 / N
    sums1 = [mmb(cat(y[e], y[e] * y[e]), seg) for e in n_e]
    for e, (c, g) in enumerate(E):
        mean = sums1[e][:L] * inv_n
        var = sums1[e][L:] * inv_n - mean * mean
        yn = (y[e] - mean) * lax.rsqrt(var + RWKV_GN_EPS) * prep[g]["ln_g"] + prep[g]["ln_b"]
        out_ref[0, rows(c), lanes(g)] = ((yn + bonus_v[e]) * gate[e]).astype(out_ref.dtype)


def _rwkv(main3, small3, pvec, mu_g, mu_wa, w_up, a_up, g_up, cast_arrays=()):
    b, s, _ = main3.shape
    L = RWKV_CHUNK * RWKV_CHUNKS_PER_STEP
    rw = RWKV_WIDTH
    head_of = jnp.arange(RWKV_GROUP_W) // RWKV_HEAD
    seg = (head_of[:, None] == head_of[None, :]).astype(BF16)
    const = lambda shape: pl.BlockSpec(shape, lambda i, c: (0, 0))
    return _call_with_casts(
        _rwkv_kernel, n_in=12, cast_arrays=cast_arrays,
        out_shape=[jax.ShapeDtypeStruct((b, s, rw), BF16)],
        grid=(b, s // L),
        in_specs=[pl.BlockSpec((1, L, rw), lambda i, c: (i, c, 0)),
                  pl.BlockSpec((1, L, rw), lambda i, c: (i, c, 1)),
                  pl.BlockSpec((1, L, rw), lambda i, c: (i, c, 2)),
                  pl.BlockSpec((1, L, RG_PAD), lambda i, c: (i, c, COL_RG // RG_PAD)),
                  pl.BlockSpec((1, L, WA_W), lambda i, c: (i, c, COL_WA // WA_W)),
                  const((PV_ROWS, rw)), const((1, RG_PAD)), const((1, WA_W)),
                  const((RWKV_DECAY_RANK, rw)), const((RWKV_A_RANK, rw)), const((RG_PAD, rw)),
                  const((RWKV_GROUP_W, RWKV_GROUP_W))],
        out_specs=[pl.BlockSpec((1, L, rw), lambda i, c: (i, c, 0))],
        scratch_shapes=[pltpu.VMEM((RWKV_GROUPS, RWKV_HEAD, RWKV_GROUP_W), F32),
                        pltpu.VMEM((SUBLANES, rw), F32), pltpu.VMEM((SUBLANES, rw), F32),
                        pltpu.VMEM((SUBLANES, rw), F32), pltpu.VMEM((SUBLANES, RG_PAD), F32),
                        pltpu.VMEM((SUBLANES, WA_W), F32)],
        compiler_params=_params(("parallel", "arbitrary")),
        name="rwkv7",
    )(main3, main3, main3, small3, small3, pvec, mu_g, mu_wa, w_up, a_up, g_up, seg)


def _mix_out_kernel(ha_ref, hb_ref, ga_ref, gb_ref, wa_ref, wb_ref, wo_ref, g_ref, x_ref, o_ref):
    ya = jnp.dot(ha_ref[...], wa_ref[...], preferred_element_type=F32)
    yb = jnp.dot(hb_ref[...], wb_ref[...], preferred_element_type=F32)
    merged = _bf(_sigmoid(ga_ref[...].astype(F32)) * ya + _sigmoid(gb_ref[...].astype(F32)) * yb)
    y = jnp.dot(merged, wo_ref[...], preferred_element_type=F32)
    o_ref[...] = x_ref[...] + _rms(y, g_ref[...])


def _mix_out(ha, hb, gates, wa, wb, wo, g, x, *, tm):
    t, kdim = ha.shape
    d = wo.shape[1]
    resident = lambda shape: pl.BlockSpec(shape, lambda i: (0, 0), pipeline_mode=pl.Buffered(1))
    return pl.pallas_call(
        _mix_out_kernel,
        out_shape=jax.ShapeDtypeStruct((t, d), F32),
        grid=(t // tm,),
        in_specs=[pl.BlockSpec((tm, kdim), lambda i: (i, 0)),
                  pl.BlockSpec((tm, kdim), lambda i: (i, 0)),
                  pl.BlockSpec((tm, d), lambda i: (i, 0)),
                  pl.BlockSpec((tm, d), lambda i: (i, 1)),
                  resident((kdim, d)), resident((kdim, d)), resident((d, d)),
                  pl.BlockSpec((1, d), lambda i: (0, 0)),
                  pl.BlockSpec((tm, d), lambda i: (i, 0))],
        out_specs=pl.BlockSpec((tm, d), lambda i: (i, 0)),
        compiler_params=_params(("parallel",)),
        name="mix_out",
    )(ha, hb, gates, gates, wa, wb, wo, g, x)


def _xattn_kernel(x_ref, gpre_ref, wq_ref, kv_ref, wo_ref, gpost_ref, o_ref):
    x = x_ref[...]
    h = _bf(_rms(x, gpre_ref[...]))
    q = jnp.dot(h, wq_ref[...], preferred_element_type=F32)
    scale = XATTN_HEAD_DIM ** -0.5
    outs = []
    for hd in range(XATTN_HEADS):
        sl = slice(hd * XATTN_HEAD_DIM, (hd + 1) * XATTN_HEAD_DIM)
        k = kv_ref[0, :, sl]
        v = kv_ref[0, :, XATTN_WIDTH + hd * XATTN_HEAD_DIM:XATTN_WIDTH + (hd + 1) * XATTN_HEAD_DIM]
        sc = _mm_nt(q[:, sl], k) * scale
        sc = sc - jnp.max(sc, axis=-1, keepdims=True)
        e = jnp.exp(sc)
        p = e / jnp.sum(e, axis=-1, keepdims=True)
        outs.append(_mm(p, v))
    o = jnp.concatenate(outs, axis=-1)
    y = jnp.dot(_bf(o), wo_ref[...], preferred_element_type=F32)
    o_ref[...] = x + _rms(y, gpost_ref[...])


def _xattn(x, gpre, wq, kv3, wo, gpost, *, tm, seq):
    t, d = x.shape
    per_seq = seq // tm
    return pl.pallas_call(
        _xattn_kernel,
        out_shape=jax.ShapeDtypeStruct((t, d), F32),
        grid=(t // tm,),
        in_specs=[pl.BlockSpec((tm, d), lambda i: (i, 0)),
                  pl.BlockSpec((1, d), lambda i: (0, 0)),
                  pl.BlockSpec((d, XATTN_WIDTH), lambda i: (0, 0)),
                  pl.BlockSpec((1, MEM_LEN, 2 * XATTN_WIDTH), lambda i: (i // per_seq, 0, 0)),
                  pl.BlockSpec((XATTN_WIDTH, d), lambda i: (0, 0)),
                  pl.BlockSpec((1, d), lambda i: (0, 0))],
        out_specs=pl.BlockSpec((tm, d), lambda i: (i, 0)),
        compiler_params=_params(("parallel",)),
        name="xattn",
    )(x, gpre, wq, kv3, wo, gpost)


FFN_TAIL = SUBLANES


def _gelu_tanh(x):
    return 0.5 * x * (1.0 + jnp.tanh(0.7978845608028654 * (x + 0.044715 * x * x * x)))


def _ffn_kernel(x_ref, gpre_ref, wg_ref, wu_ref, cwg_ref, cwu_ref, cbg_ref, cbu_ref, wd_ref, gpost_ref,
                o_ref, h_ref, tail_ref, *, tiles_per_seq):
    i = pl.program_id(0)
    j = pl.program_id(1)
    tm, tn = x_ref.shape[0], wg_ref.shape[1]

    @pl.when((i == 0) & (j == 0))
    def _():
        tail_ref[...] = jnp.zeros_like(tail_ref)

    @pl.when(j == 0)
    def _():
        h_ref[...] = _bf(_rms(x_ref[...], gpre_ref[...]))
        o_ref[...] = jnp.zeros_like(o_ref)

    prev = jnp.where(i % tiles_per_seq == 0, 0.0, tail_ref[j])
    ug = jnp.dot(h_ref[...], wg_ref[...], preferred_element_type=F32)
    uu = jnp.dot(h_ref[...], wu_ref[...], preferred_element_type=F32)
    tail_ref[j] = jnp.concatenate([ug[tm - FFN_TAIL:, :], uu[tm - FFN_TAIL:, :]], axis=1)

    def conv(u, u_prev, cw_ref, cb_ref):
        u = jnp.concatenate([u_prev, u], axis=0)
        u1 = pltpu.roll(u, 1, 0)
        u2 = pltpu.roll(u, 2, 0)
        cw = cw_ref[...]
        full = cb_ref[...] + cw[2:3, :] * u + cw[1:2, :] * u1 + cw[0:1, :] * u2
        return full[FFN_TAIL:, :]

    gate = conv(ug, prev[:, :tn], cwg_ref, cbg_ref)
    up = conv(uu, prev[:, tn:], cwu_ref, cbu_ref)
    act = _bf(_gelu_tanh(gate) * up)
    o_ref[...] += jnp.dot(act, wd_ref[...], preferred_element_type=F32)

    @pl.when(j == pl.num_programs(1) - 1)
    def _():
        o_ref[...] = x_ref[...] + _rms(o_ref[...], gpost_ref[...])


def _ffn(x, gpre, w_up, conv_w, conv_b, w_down, gpost, *, tm, tn, seq):
    t, d = x.shape
    nj = D_FF // tn
    return pl.pallas_call(
        functools.partial(_ffn_kernel, tiles_per_seq=seq // tm),
        out_shape=jax.ShapeDtypeStruct((t, d), F32),
        grid=(t // tm, nj),
        in_specs=[pl.BlockSpec((tm, d), lambda i, j: (i, 0)),
                  pl.BlockSpec((1, d), lambda i, j: (0, 0)),
                  pl.BlockSpec((d, tn), lambda i, j: (0, j)),
                  pl.BlockSpec((d, tn), lambda i, j: (0, nj + j)),
                  pl.BlockSpec((CONV_WIDTH, tn), lambda i, j: (0, j)),
                  pl.BlockSpec((CONV_WIDTH, tn), lambda i, j: (0, nj + j)),
                  pl.BlockSpec((1, tn), lambda i, j: (0, j)),
                  pl.BlockSpec((1, tn), lambda i, j: (0, nj + j)),
                  pl.BlockSpec((tn, d), lambda i, j: (j, 0)),
                  pl.BlockSpec((1, d), lambda i, j: (0, 0))],
        out_specs=pl.BlockSpec((tm, d), lambda i, j: (i, 0), pipeline_mode=pl.Buffered(1)),
        scratch_shapes=[pltpu.VMEM((tm, d), BF16), pltpu.VMEM((nj, FFN_TAIL, 2 * tn), F32)],
        compiler_params=_params(("arbitrary", "arbitrary")),
        name="conv_glu_ffn",
    )(x, gpre, w_up, w_up, conv_w, conv_w, conv_b, conv_b, w_down, gpost)


def _small_in_proj(w_in):
    d = w_in.shape[1]
    r0, g0 = MLSTM_TOTAL, MLSTM_TOTAL + RWKV_TOTAL
    mlstm_if = w_in[MLSTM_MAIN:r0]
    rwkv_wa = w_in[r0 + RWKV_MAIN:r0 + RWKV_MAIN + WA_W]
    rwkv_g = w_in[r0 + RWKV_MAIN + WA_W:g0]
    z = lambda n: jnp.zeros((n, d), w_in.dtype)
    return jnp.concatenate([rwkv_g, z(RG_PAD - RWKV_GATE_RANK), rwkv_wa, mlstm_if, z(IF_W - 2 * MLSTM_HEADS)], axis=0)


def _layer(x, mem, mix_pre_norm, w_in, mlstm_b_i, mlstm_b_f, mlstm_head_norm, rwkv_mu, rwkv_w0, rwkv_w_up, rwkv_a0,
           rwkv_a_up, rwkv_g_up, rwkv_k_k, rwkv_k_a, rwkv_r_k, rwkv_ln_g, rwkv_ln_b, w_branch_a, w_branch_b,
           w_mix_out, mix_post_norm, xattn_pre_norm, mem_norm, xattn_wq, xattn_wkv, xattn_wo, xattn_post_norm,
           ffn_pre_norm, ffn_w_up, ffn_conv_w, ffn_conv_b, ffn_w_down, ffn_post_norm):
    b, s, d = x.shape
    t = b * s
    x2d = x.reshape(t, d)
    row = lambda p: p.reshape(1, -1).astype(F32)
    tm_row = min(TM_ROW, s)

    w_in_t = w_in.T
    tm_in = min(TM_IN_PROJ, t)
    small, h = _norm_matmul(x2d, row(mix_pre_norm), _small_in_proj(w_in_t), tm=tm_in, tn=SMALL_COLS, out_dtype=F32,
                            name="in_proj_small", w_is_transposed=True)
    small3 = small.reshape(b, s, SMALL_COLS)
    mlstm_main, rwkv_main, gates = _rowranges_matmul(
        h, w_in_t, [(0, MLSTM_MAIN), (MLSTM_TOTAL, RWKV_MAIN), (MLSTM_TOTAL + RWKV_TOTAL, 2 * D_MODEL)],
        tm=tm_in, tn=TN_IN_PROJ, out_dtype=BF16, name="in_proj_wide")

    gate_bias = jnp.concatenate([mlstm_b_i, mlstm_b_f, jnp.zeros((IF_W - 2 * MLSTM_HEADS,), F32)]).reshape(1, IF_W)
    ha, ffn_w_down_b, w_a_b, w_b_b, w_out_b = _mlstm(
        mlstm_main.reshape(b, s, MLSTM_MAIN), small3, gate_bias, row(mlstm_head_norm),
        cast_arrays=(ffn_w_down, w_branch_a, w_branch_b, w_mix_out))

    mu_r, mu_k, mu_v = (rwkv_mu[i * RWKV_WIDTH:(i + 1) * RWKV_WIDTH] for i in range(3))
    mu_wa = rwkv_mu[RWKV_MAIN:RWKV_MAIN + WA_W]
    mu_g = jnp.pad(rwkv_mu[RWKV_MAIN + WA_W:], (0, RG_PAD - RWKV_GATE_RANK))
    pvec = jnp.stack([mu_r, mu_k, mu_v, rwkv_w0, rwkv_a0, rwkv_k_k, rwkv_k_a, rwkv_r_k.reshape(-1), rwkv_ln_g,
                      rwkv_ln_b] + [jnp.zeros((RWKV_WIDTH,), F32)] * (PV_ROWS - 10))
    g_up = jnp.pad(rwkv_g_up, ((0, RG_PAD - RWKV_GATE_RANK), (0, 0)))
    hb, ffn_w_up_b = _rwkv(rwkv_main.reshape(b, s, RWKV_MAIN), small3, pvec, row(mu_g), row(mu_wa), _bf(rwkv_w_up),
                           _bf(rwkv_a_up), _bf(g_up), cast_arrays=(ffn_w_up,))

    x1 = _mix_out(ha.reshape(t, -1), hb.reshape(t, -1), gates, w_a_b, w_b_b, w_out_b, row(mix_post_norm), x2d,
                  tm=tm_row)

    mem2d = mem.reshape(b * MEM_LEN, d)
    kv, _ = _norm_matmul(mem2d, row(mem_norm), _bf(xattn_wkv), tm=MEM_LEN, tn=TN, out_dtype=F32, name="mem_kv")
    x2 = _xattn(x1, row(xattn_pre_norm), _bf(xattn_wq), kv.reshape(b, MEM_LEN, 2 * XATTN_WIDTH), _bf(xattn_wo),
                row(xattn_post_norm), tm=min(TM_XATTN, s), seq=s)

    x3 = _ffn(x2, row(ffn_pre_norm), ffn_w_up_b, ffn_conv_w, row(ffn_conv_b), ffn_w_down_b, row(ffn_post_norm),
              tm=min(TM_FFN, s), tn=TN, seq=s)
    return x3.reshape(b, s, d)


def kernel(x, mem, mix_pre_norm, w_in, mlstm_b_i, mlstm_b_f, mlstm_head_norm, rwkv_mu, rwkv_w0, rwkv_w_up, rwkv_a0, rwkv_a_up, rwkv_g_up, rwkv_k_k, rwkv_k_a, rwkv_r_k, rwkv_ln_g, rwkv_ln_b, w_branch_a, w_branch_b, w_mix_out, mix_post_norm, xattn_pre_norm, mem_norm, xattn_wq, xattn_wkv, xattn_wo, xattn_post_norm, ffn_pre_norm, ffn_w_up, ffn_conv_w, ffn_conv_b, ffn_w_down, ffn_post_norm):
    for l in range(mix_pre_norm.shape[0]):
        x = _layer(x, mem, mix_pre_norm[l], w_in[l], mlstm_b_i[l], mlstm_b_f[l], mlstm_head_norm[l], rwkv_mu[l],
                   rwkv_w0[l], rwkv_w_up[l], rwkv_a0[l], rwkv_a_up[l], rwkv_g_up[l], rwkv_k_k[l], rwkv_k_a[l],
                   rwkv_r_k[l], rwkv_ln_g[l], rwkv_ln_b[l], w_branch_a[l], w_branch_b[l], w_mix_out[l],
                   mix_post_norm[l], xattn_pre_norm[l], mem_norm[l], xattn_wq[l], xattn_wkv[l], xattn_wo[l],
                   xattn_post_norm[l], ffn_pre_norm[l], ffn_w_up[l], ffn_conv_w[l], ffn_conv_b[l], ffn_w_down[l],
                   ffn_post_norm[l])
    return x
```

```python
import functools

import jax
import jax.numpy as jnp
from jax import lax
from jax.experimental import pallas as pl
from jax.experimental.pallas import tpu as pltpu

D_MODEL = 2048
MEM_LEN = 256
RMS_EPS = 1e-6

MLSTM_HEADS = 4
MLSTM_WIDTH = D_MODEL // 2
MLSTM_V_DIM = MLSTM_WIDTH // MLSTM_HEADS
MLSTM_QK_DIM = MLSTM_V_DIM // 2
MLSTM_QK_WIDTH = MLSTM_HEADS * MLSTM_QK_DIM
GATE_SOFTCAP = 15.0

RWKV_WIDTH = D_MODEL // 2
RWKV_HEAD = 64
RWKV_HEADS = RWKV_WIDTH // RWKV_HEAD
RWKV_DECAY_RANK = 64
RWKV_A_RANK = 64
RWKV_GATE_RANK = 160
RWKV_GN_EPS = 64e-5

MLSTM_MAIN = 2 * MLSTM_QK_WIDTH + 2 * MLSTM_WIDTH
MLSTM_TOTAL = MLSTM_MAIN + 2 * MLSTM_HEADS
RWKV_MAIN = 3 * RWKV_WIDTH
RWKV_TOTAL = RWKV_MAIN + RWKV_DECAY_RANK + RWKV_A_RANK + RWKV_GATE_RANK

XATTN_HEADS = 4
XATTN_HEAD_DIM = 128
XATTN_WIDTH = XATTN_HEADS * XATTN_HEAD_DIM

D_FF = 4 * D_MODEL
CONV_WIDTH = 3

RG_PAD = 256
WA_W = RWKV_DECAY_RANK + RWKV_A_RANK
IF_W = 128
COL_RG = 0
COL_WA = COL_RG + RG_PAD
COL_IF = COL_WA + WA_W
SMALL_COLS = COL_IF + IF_W

MLSTM_CHUNK = 256
RWKV_CHUNK = 64
RWKV_CHUNKS_PER_STEP = 4
RWKV_PREP_CHUNKS = 2
RWKV_GROUP_W = 256
RWKV_GROUPS = RWKV_WIDTH // RWKV_GROUP_W
VMEM_LIMIT = 56 * 1024 * 1024

LANES = 128
SUBLANES = 8
BF16_SUBLANES = 16
TM_IN_PROJ = 1024
TN_IN_PROJ = 1024
TM_ROW = 512
TM_XATTN = 1024
TM_FFN = 1024
TN = 512

F32 = jnp.float32
BF16 = jnp.bfloat16


def _bf(x):
    return x.astype(BF16)


def _mm(a, b):
    return jnp.dot(_bf(a), _bf(b), preferred_element_type=F32)


def _mm_nt(a, b):
    return lax.dot_general(_bf(a), _bf(b), (((1,), (1,)), ((), ())), preferred_element_type=F32)


def _mm_tn(a, b):
    return lax.dot_general(_bf(a), _bf(b), (((0,), (0,)), ((), ())), preferred_element_type=F32)


def _mm_exact_lhs(tri, x):
    hi = _bf(x)
    r1 = x - hi.astype(F32)
    mid = _bf(r1)
    lo = _bf(r1 - mid.astype(F32))
    t = _bf(tri)
    return (jnp.dot(t, hi, preferred_element_type=F32) + jnp.dot(t, mid, preferred_element_type=F32)
            + jnp.dot(t, lo, preferred_element_type=F32))


def _rms(x, g):
    return x * lax.rsqrt(jnp.mean(x * x, axis=-1, keepdims=True) + RMS_EPS) * g


def _softplus(x):
    return jnp.maximum(x, 0.0) + jnp.log(1.0 + jnp.exp(-jnp.abs(x)))


def _sigmoid(x):
    return 1.0 / (1.0 + jnp.exp(-x))


def _params(sem):
    return pltpu.CompilerParams(dimension_semantics=sem, vmem_limit_bytes=VMEM_LIMIT)


def _call_with_casts(body, *, n_in, out_shape, grid, in_specs, out_specs, cast_arrays=(), **kwargs):
    n_out = len(out_shape)
    k = len(cast_arrays)
    nsteps = 1
    for extent in grid:
        nsteps *= extent

    def step(*ids):
        lin = ids[0]
        for extent, idx in zip(grid[1:], ids[1:]):
            lin = lin * extent + idx
        return lin

    band_specs, band_shapes = [], []
    for arr in cast_arrays:
        rows, cols = arr.shape
        assert rows % nsteps == 0 and (rows // nsteps) % BF16_SUBLANES == 0, (arr.shape, nsteps)
        band_specs.append(pl.BlockSpec((rows // nsteps, cols), lambda *ids: (step(*ids), 0)))
        band_shapes.append(jax.ShapeDtypeStruct(arr.shape, BF16))

    def kernel(*refs):
        ins, src = refs[:n_in], refs[n_in:n_in + k]
        outs, dst = refs[n_in + k:n_in + k + n_out], refs[n_in + k + n_out:n_in + 2 * k + n_out]
        for s_ref, d_ref in zip(src, dst):
            d_ref[...] = s_ref[...].astype(d_ref.dtype)
        body(*ins, *outs, *refs[n_in + 2 * k + n_out:])

    def run(*operands):
        assert len(operands) == n_in
        return pl.pallas_call(kernel, out_shape=tuple(out_shape) + tuple(band_shapes), grid=grid,
                              in_specs=list(in_specs) + band_specs, out_specs=tuple(out_specs) + tuple(band_specs),
                              **kwargs)(*operands, *cast_arrays)

    return run


def _norm_matmul_kernel(x_ref, g_ref, w_ref, o_ref, h_ref, *, w_is_transposed):
    @pl.when(pl.program_id(1) == 0)
    def _():
        h_ref[...] = _bf(_rms(x_ref[...], g_ref[...]))

    w = w_ref[...].T if w_is_transposed else w_ref[...]
    o_ref[...] = jnp.dot(h_ref[...], _bf(w), preferred_element_type=F32).astype(o_ref.dtype)


def _norm_matmul(x, g, w, *, tm, tn, out_dtype, name, w_is_transposed=False):
    t, d = x.shape
    n = w.shape[0] if w_is_transposed else w.shape[1]
    w_spec = pl.BlockSpec((tn, d), lambda i, j: (j, 0)) if w_is_transposed else pl.BlockSpec((d, tn), lambda i, j: (0, j))
    return pl.pallas_call(
        functools.partial(_norm_matmul_kernel, w_is_transposed=w_is_transposed),
        out_shape=(jax.ShapeDtypeStruct((t, n), out_dtype), jax.ShapeDtypeStruct((t, d), BF16)),
        grid=(t // tm, n // tn),
        in_specs=[pl.BlockSpec((tm, d), lambda i, j: (i, 0)),
                  pl.BlockSpec((1, d), lambda i, j: (0, 0)),
                  w_spec],
        out_specs=(pl.BlockSpec((tm, tn), lambda i, j: (i, j)),
                   pl.BlockSpec((tm, d), lambda i, j: (i, 0))),
        compiler_params=_params(("parallel", "arbitrary")),
        name=name,
    )(x, g, w)


def _rowranges_matmul_kernel(a_ref, wt_ref, *refs, tile_ranges):
    o_refs, wb_ref = refs[:-1], refs[-1]
    j = pl.program_id(0)

    @pl.when(pl.program_id(1) == 0)
    def _():
        wb_ref[...] = _bf(wt_ref[...].T)

    for o_ref, (first, stop) in zip(o_refs, tile_ranges):
        @pl.when((j >= first) & (j < stop))
        def _(o_ref=o_ref):
            o_ref[...] = jnp.dot(a_ref[...], wb_ref[...], preferred_element_type=F32).astype(o_ref.dtype)


def _rowranges_matmul(a, w_t, ranges, *, tm, tn, out_dtype, name):
    t, d = a.shape
    n_i = t // tm
    tile_ranges, first = [], 0
    for row0, nrows in ranges:
        assert row0 % SUBLANES == 0 and nrows % tn == 0
        tile_ranges.append((first, first + nrows // tn))
        first += nrows // tn

    def weight_row(j, i):
        row = 0
        for (row0, _), (lo, hi) in zip(ranges, tile_ranges):
            row = row + jnp.where((j >= lo) & (j < hi), row0 + tn * (j - lo), 0)
        return pl.multiple_of(row, SUBLANES), 0

    def out_index(lo, hi):
        return lambda j, i: (jnp.where(j < lo, 0, jnp.where(j >= hi, n_i - 1, i)), jnp.clip(j - lo, 0, hi - lo - 1))

    return pl.pallas_call(
        functools.partial(_rowranges_matmul_kernel, tile_ranges=tuple(tile_ranges)),
        out_shape=[jax.ShapeDtypeStruct((t, nrows), out_dtype) for _, nrows in ranges],
        grid=(first, n_i),
        in_specs=[pl.BlockSpec((tm, d), lambda j, i: (i, 0)),
                  pl.BlockSpec((pl.Element(tn), pl.Element(d)), weight_row)],
        out_specs=[pl.BlockSpec((tm, tn), out_index(lo, hi)) for lo, hi in tile_ranges],
        scratch_shapes=[pltpu.VMEM((d, tn), BF16)],
        compiler_params=_params(("arbitrary", "arbitrary")),
        name=name,
    )(a, w_t)


def _mlstm_kernel(q_ref, k_ref, v_ref, o_ref, gate_ref, bias_ref, hn_ref, out_ref, ct_ref, n_ref, m_ref):
    L = MLSTM_CHUNK
    dk, dv = MLSTM_QK_DIM, MLSTM_V_DIM

    @pl.when(pl.program_id(1) == 0)
    def _():
        ct_ref[...] = jnp.zeros_like(ct_ref)
        n_ref[...] = jnp.zeros_like(n_ref)
        m_ref[...] = jnp.zeros_like(m_ref)

    pre = gate_ref[0] + bias_ref[...]
    capped = GATE_SOFTCAP * jnp.tanh(pre / GATE_SOFTCAP)
    logf = -_softplus(-capped)
    row = lax.broadcasted_iota(jnp.int32, (L, L), 0)
    col = lax.broadcasted_iota(jnp.int32, (L, L), 1)
    causal = row >= col
    bcum = _mm_exact_lhs(causal.astype(F32), logf)
    ig_t = capped.T
    bcum_t = bcum.T
    scale = MLSTM_QK_DIM ** -0.5

    H = range(MLSTM_HEADS)
    q = [q_ref[0, :, h * dk:(h + 1) * dk] for h in H]
    k = [k_ref[0, :, h * dk:(h + 1) * dk] for h in H]
    v = [v_ref[0, :, h * dv:(h + 1) * dv] for h in H]
    b_c = [bcum[:, MLSTM_HEADS + h:MLSTM_HEADS + h + 1] for h in H]
    b_r = [bcum_t[MLSTM_HEADS + h:MLSTM_HEADS + h + 1, :] for h in H]
    i_c = [capped[:, h:h + 1] for h in H]
    i_r = [ig_t[h:h + 1, :] for h in H]
    m_prev = [m_ref[h][0:1, 0:1] for h in H]
    ct = [ct_ref[h] for h in H]
    nrow = [n_ref[h][0:1, :] for h in H]

    dmat = [jnp.where(causal, b_c[h] - b_r[h] + i_r[h], -jnp.inf) for h in H]
    inter = [b_c[h] + m_prev[h] for h in H]
    m_t = [jnp.maximum(inter[h], jnp.max(dmat[h], axis=-1, keepdims=True)) for h in H]
    dexp = [jnp.exp(dmat[h] - m_t[h]) * scale for h in H]
    w_inter = [jnp.exp(inter[h] - m_t[h]) for h in H]
    qk = [_mm_nt(q[h], k[h]) for h in H]
    qc = [_mm(q[h], ct[h]) for h in H]
    s = [qk[h] * dexp[h] for h in H]
    sv = [_mm(s[h], v[h]) for h in H]
    num = [w_inter[h] * qc[h] + sv[h] for h in H]
    den = [w_inter[h] * jnp.sum(q[h].astype(F32) * nrow[h], axis=-1, keepdims=True)
           + jnp.sum(s[h], axis=-1, keepdims=True) for h in H]
    hh = [num[h] / jnp.maximum(jnp.abs(den[h]), jnp.exp(-m_t[h])) for h in H]

    b_last = [b_c[h][L - 1:L, :] for h in H]
    gs = [b_last[h] - b_c[h] + i_c[h] for h in H]
    m_new = [jnp.maximum(b_last[h] + m_prev[h], jnp.max(gs[h], axis=0, keepdims=True)) for h in H]
    carry_w = [jnp.exp(b_last[h] + m_prev[h] - m_new[h]) for h in H]
    ws = [jnp.exp(gs[h] - m_new[h]) * scale for h in H]
    kv = [_mm_tn(k[h], ws[h] * v[h].astype(F32)) for h in H]
    for h in H:
        ct_ref[h] = carry_w[h] * ct[h] + kv[h]
        n_ref[h] = jnp.broadcast_to(
            carry_w[h] * nrow[h] + jnp.sum(ws[h] * k[h].astype(F32), axis=0, keepdims=True), n_ref.shape[1:])
        m_ref[h] = jnp.broadcast_to(m_new[h], m_ref.shape[1:])

    ms = [jnp.mean(hh[h] * hh[h], axis=-1, keepdims=True) for h in H]
    for h in H:
        hm = hh[h] * lax.rsqrt(ms[h] + RMS_EPS) * hn_ref[:, h * dv:(h + 1) * dv]
        og = o_ref[0, :, h * dv:(h + 1) * dv].astype(F32)
        out_ref[0, :, h * dv:(h + 1) * dv] = (_sigmoid(og) * hm).astype(out_ref.dtype)


def _mlstm(main3, small3, gate_bias, head_norm, cast_arrays=()):
    b, s, _ = main3.shape
    L = min(MLSTM_CHUNK, s)
    assert L == MLSTM_CHUNK and s % L == 0
    qw, vw = MLSTM_QK_WIDTH, MLSTM_WIDTH
    return _call_with_casts(
        _mlstm_kernel, n_in=7, cast_arrays=cast_arrays,
        out_shape=[jax.ShapeDtypeStruct((b, s, MLSTM_WIDTH), BF16)],
        grid=(b, s // L),
        in_specs=[pl.BlockSpec((1, L, qw), lambda i, c: (i, c, 0)),
                  pl.BlockSpec((1, L, qw), lambda i, c: (i, c, 1)),
                  pl.BlockSpec((1, L, vw), lambda i, c: (i, c, 1)),
                  pl.BlockSpec((1, L, vw), lambda i, c: (i, c, 2)),
                  pl.BlockSpec((1, L, IF_W), lambda i, c: (i, c, COL_IF // IF_W)),
                  pl.BlockSpec((1, IF_W), lambda i, c: (0, 0)),
                  pl.BlockSpec((1, vw), lambda i, c: (0, 0))],
        out_specs=[pl.BlockSpec((1, L, vw), lambda i, c: (i, c, 0))],
        scratch_shapes=[pltpu.VMEM((MLSTM_HEADS, MLSTM_QK_DIM, MLSTM_V_DIM), F32),
                        pltpu.VMEM((MLSTM_HEADS, SUBLANES, MLSTM_QK_DIM), F32),
                        pltpu.VMEM((MLSTM_HEADS, SUBLANES, LANES), F32)],
        compiler_params=_params(("parallel", "arbitrary")),
        name="mlstm",
    )(main3, main3, main3, main3, small3, gate_bias, head_norm)


PV_MU_R, PV_MU_K, PV_MU_V, PV_W0, PV_A0, PV_KK, PV_KA, PV_RK, PV_LNG, PV_LNB = range(10)
PV_ROWS = 16


def _rwkv_kernel(r_ref, k_ref, v_ref, g_ref, wa_ref, pv_ref, mug_ref, muwa_ref, wup_ref, aup_ref, gup_ref, seg_ref,
                 out_ref, s_ref, pr_ref, pk_ref, pvv_ref, pg_ref, pwa_ref):
    L = RWKV_CHUNK
    CH = RWKV_CHUNKS_PER_STEP
    R = CH * L
    N = RWKV_HEAD
    GW = RWKV_GROUP_W
    GH = GW // N

    @pl.when(pl.program_id(1) == 0)
    def _():
        s_ref[...] = jnp.zeros_like(s_ref)
        pr_ref[...] = jnp.zeros_like(pr_ref)
        pk_ref[...] = jnp.zeros_like(pk_ref)
        pvv_ref[...] = jnp.zeros_like(pvv_ref)
        pg_ref[...] = jnp.zeros_like(pg_ref)
        pwa_ref[...] = jnp.zeros_like(pwa_ref)

    groups = range(RWKV_GROUPS)
    rows = lambda c: slice(c * L, (c + 1) * L)
    lanes = lambda g: slice(g * GW, (g + 1) * GW)
    cat = lambda u, w_: jnp.concatenate([u, w_], axis=0)
    seg = seg_ref[...]
    pv = pv_ref[...]

    PR = RWKV_PREP_CHUNKS * L

    def shift_lerp(x_ref, prev_ref, mu, rs, ls=slice(None)):
        x = x_ref[0, rs, ls].astype(F32)
        rid = lax.broadcasted_iota(jnp.int32, x.shape, 0)
        xs = jnp.where(rid == 0, prev_ref[0:1, ls], pltpu.roll(x, 1, 0))
        prev_ref[0:1, ls] = x[PR - 1:PR, :]
        return x + (xs - x) * mu

    tr = lax.broadcasted_iota(jnp.int32, (PR, PR), 0)
    tc = lax.broadcasted_iota(jnp.int32, (PR, PR), 1)
    tri = ((tr >= tc) & ((tr // L) == (tc // L))).astype(F32)

    ti = lax.broadcasted_iota(jnp.int32, (L, GW), 0)
    lane = lax.broadcasted_iota(jnp.int32, (L, GW), 1)
    si = lane & (N - 1)
    lane_head = lane // N
    incl = ti >= si
    strict = ti > si
    eye = (ti == si).astype(F32)
    lvl_masks = []
    bsz = 1
    while bsz < L:
        same = (ti // (2 * bsz)) == (si // (2 * bsz))
        lvl_masks.append(jnp.where(same & ((ti & bsz) != 0) & ((si & bsz) == 0), 1.0, 0.0))
        bsz *= 2

    def bd(x):
        return jnp.concatenate([_bf(x)] * GH, axis=0) * seg

    def fold(z):
        acc = z[0:N]
        for hh in range(1, GH):
            acc = jnp.where(lane_head == hh, z[hh * N:(hh + 1) * N], acc)
        return acc

    def mmb(x, b):
        return jnp.dot(_bf(x), b, preferred_element_type=F32)

    def mmb_nt(x, b):
        return lax.dot_general(_bf(x), b, (((1,), (1,)), ((), ())), preferred_element_type=F32)

    prep = []
    for pb in range(CH // RWKV_PREP_CHUNKS):
        rs = slice(pb * PR, (pb + 1) * PR)
        gl = shift_lerp(g_ref, pg_ref, mug_ref[...], rs)
        wa = shift_lerp(wa_ref, pwa_ref, muwa_ref[...], rs)
        wl_b = _bf(jnp.tanh(wa[:, 0:RWKV_DECAY_RANK]))
        al_b = _bf(wa[:, RWKV_DECAY_RANK:WA_W])
        sgl_b = _bf(_sigmoid(gl))
        pieces = []
        for g in groups:
            ls = lanes(g)
            row = lambda i, ls=ls: pv[i:i + 1, ls]
            r = shift_lerp(r_ref, pr_ref, row(PV_MU_R), rs, ls)
            kr = shift_lerp(k_ref, pk_ref, row(PV_MU_K), rs, ls)
            v = shift_lerp(v_ref, pvv_ref, row(PV_MU_V), rs, ls)
            w_log = -_softplus(-(row(PV_W0) + jnp.dot(wl_b, wup_ref[:, ls], preferred_element_type=F32))) - 0.5
            lw = -jnp.exp(w_log)
            a = _sigmoid(row(PV_A0) + jnp.dot(al_b, aup_ref[:, ls], preferred_element_type=F32))
            gate = jnp.dot(sgl_b, gup_ref[:, ls], preferred_element_type=F32)
            cs = _mm_exact_lhs(tri, lw)
            g_incl = jnp.exp(cs)
            g_inv = jnp.exp(-cs)
            kk0 = kr * row(PV_KK)
            kr2 = kr * (1.0 + (a - 1.0) * row(PV_KA))
            sums = jnp.dot(_bf(cat(kk0 * kk0, r * kr2 * row(PV_RK))), seg, preferred_element_type=F32)
            kk = kk0 * lax.rsqrt(jnp.maximum(sums[:PR], 1e-24))
            pieces.append(dict(p=jnp.exp(cs - lw) * kk, qt=kk * a * g_inv, kt=kr2 * g_inv, rt=r * g_incl, v=v,
                               bonus_v=sums[PR:] * v, gate=gate, g_incl=g_incl,
                               ln_g=row(PV_LNG), ln_b=row(PV_LNB)))
        prep.append(pieces)

    E = [(c, g) for c in range(CH) for g in groups]
    n_e = range(len(E))
    piece = lambda c, g: prep[c // RWKV_PREP_CHUNKS][g]
    local = lambda c: rows(c % RWKV_PREP_CHUNKS)
    blk = lambda name: [piece(c, g)[name][local(c), :] for c, g in E]
    p, qt, kt, rt, vv, bonus_v, gate = (blk(n_) for n_ in ("p", "qt", "kt", "rt", "v", "bonus_v", "gate"))
    gl_ = [piece(c, g)["g_incl"][local(c).stop - 1:local(c).stop, :] for c, g in E]
    qg = [qt[e] * gl_[e] for e in n_e]
    kg = [kt[e] * gl_[e] for e in n_e]

    lhs_pr = [cat(p[e], rt[e]) for e in n_e]
    gq = [mmb_nt(lhs_pr[e], bd(qt[e])) for e in n_e]
    gk = [mmb_nt(lhs_pr[e], bd(kt[e])) for e in n_e]
    n_pq = [jnp.where(strict, gq[e][:L], 0.0) for e in n_e]
    a_rq = [jnp.where(incl, gq[e][L:], 0.0) for e in n_e]
    a_pk = [jnp.where(strict, gk[e][:L], 0.0) for e in n_e]
    a_rk = [jnp.where(incl, gk[e][L:], 0.0) for e in n_e]

    x = [eye - n_pq[e] * lvl_masks[0] for e in n_e]
    bd_n = [bd(n_pq[e]) for e in n_e]
    for msk in lvl_masks[1:]:
        t1 = [mmb(x[e], bd_n[e]) for e in n_e]
        x = [x[e] - msk * mmb(t1[e], bd(x[e])) for e in n_e]

    av = [mmb(cat(a_pk[e], a_rk[e]), bd(vv[e])) for e in n_e]
    w = [mmb(x[e], bd(p[e])) for e in n_e]
    u0 = [mmb(x[e], bd(av[e][:L])) for e in n_e]
    gmat = [rt[e] - mmb(a_rq[e], bd(w[e])) for e in n_e]
    y0 = [av[e][L:] - mmb(a_rq[e], bd(u0[e])) for e in n_e]
    mt = [eye * gl_[e] - fold(_mm_tn(qg[e], w[e])) for e in n_e]
    bt = [fold(_mm_tn(cat(kg[e], -qg[e]), cat(vv[e], u0[e]))) for e in n_e]

    state = [s_ref[g] for g in groups]
    y = [None] * len(E)
    for c in range(CH):
        es = [c * RWKV_GROUPS + g for g in groups]
        ys = [mmb(cat(gmat[e], mt[e]), bd(state[g])) for g, e in zip(groups, es)]
        for g, e in zip(groups, es):
            y[e] = ys[g][:L] + y0[e]
            state[g] = ys[g][L:] + bt[e]
    for g in groups:
        s_ref[g] = state[g]

    inv_n = 1.0 / N
    sums1 = [mmb(cat(y[e], y[e] * y[e]), seg) for e in n_e]
    for e, (c, g) in enumerate(E):
        mean = sums1[e][:L] * inv_n
        var = sums1[e][L:] * inv_n - mean * mean
        yn = (y[e] - mean) * lax.rsqrt(var + RWKV_GN_EPS) * piece(c, g)["ln_g"] + piece(c, g)["ln_b"]
        out_ref[0, rows(c), lanes(g)] = ((yn + bonus_v[e]) * gate[e]).astype(out_ref.dtype)


def _rwkv(main3, small3, pvec, mu_g, mu_wa, w_up, a_up, g_up, cast_arrays=()):
    b, s, _ = main3.shape
    L = RWKV_CHUNK * RWKV_CHUNKS_PER_STEP
    rw = RWKV_WIDTH
    head_of = jnp.arange(RWKV_GROUP_W) // RWKV_HEAD
    seg = (head_of[:, None] == head_of[None, :]).astype(BF16)
    const = lambda shape: pl.BlockSpec(shape, lambda i, c: (0, 0))
    return _call_with_casts(
        _rwkv_kernel, n_in=12, cast_arrays=cast_arrays,
        out_shape=[jax.ShapeDtypeStruct((b, s, rw), BF16)],
        grid=(b, s // L),
        in_specs=[pl.BlockSpec((1, L, rw), lambda i, c: (i, c, 0)),
                  pl.BlockSpec((1, L, rw), lambda i, c: (i, c, 1)),
                  pl.BlockSpec((1, L, rw), lambda i, c: (i, c, 2)),
                  pl.BlockSpec((1, L, RG_PAD), lambda i, c: (i, c, COL_RG // RG_PAD)),
                  pl.BlockSpec((1, L, WA_W), lambda i, c: (i, c, COL_WA // WA_W)),
                  const((PV_ROWS, rw)), const((1, RG_PAD)), const((1, WA_W)),
                  const((RWKV_DECAY_RANK, rw)), const((RWKV_A_RANK, rw)), const((RG_PAD, rw)),
                  const((RWKV_GROUP_W, RWKV_GROUP_W))],
        out_specs=[pl.BlockSpec((1, L, rw), lambda i, c: (i, c, 0))],
        scratch_shapes=[pltpu.VMEM((RWKV_GROUPS, RWKV_HEAD, RWKV_GROUP_W), F32),
                        pltpu.VMEM((SUBLANES, rw), F32), pltpu.VMEM((SUBLANES, rw), F32),
                        pltpu.VMEM((SUBLANES, rw), F32), pltpu.VMEM((SUBLANES, RG_PAD), F32),
                        pltpu.VMEM((SUBLANES, WA_W), F32)],
        compiler_params=_params(("parallel", "arbitrary")),
        name="rwkv7",
    )(main3, main3, main3, small3, small3, pvec, mu_g, mu_wa, w_up, a_up, g_up, seg)


def _mix_out_kernel(ha_ref, hb_ref, ga_ref, gb_ref, wa_ref, wb_ref, wo_ref, g_ref, x_ref, o_ref):
    ya = jnp.dot(ha_ref[...], wa_ref[...], preferred_element_type=F32)
    yb = jnp.dot(hb_ref[...], wb_ref[...], preferred_element_type=F32)
    merged = _bf(_sigmoid(ga_ref[...].astype(F32)) * ya + _sigmoid(gb_ref[...].astype(F32)) * yb)
    y = jnp.dot(merged, wo_ref[...], preferred_element_type=F32)
    o_ref[...] = x_ref[...] + _rms(y, g_ref[...])


def _mix_out(ha, hb, gates, wa, wb, wo, g, x, *, tm):
    t, kdim = ha.shape
    d = wo.shape[1]
    resident = lambda shape: pl.BlockSpec(shape, lambda i: (0, 0), pipeline_mode=pl.Buffered(1))
    return pl.pallas_call(
        _mix_out_kernel,
        out_shape=jax.ShapeDtypeStruct((t, d), F32),
        grid=(t // tm,),
        in_specs=[pl.BlockSpec((tm, kdim), lambda i: (i, 0)),
                  pl.BlockSpec((tm, kdim), lambda i: (i, 0)),
                  pl.BlockSpec((tm, d), lambda i: (i, 0)),
                  pl.BlockSpec((tm, d), lambda i: (i, 1)),
                  resident((kdim, d)), resident((kdim, d)), resident((d, d)),
                  pl.BlockSpec((1, d), lambda i: (0, 0)),
                  pl.BlockSpec((tm, d), lambda i: (i, 0))],
        out_specs=pl.BlockSpec((tm, d), lambda i: (i, 0)),
        compiler_params=_params(("parallel",)),
        name="mix_out",
    )(ha, hb, gates, gates, wa, wb, wo, g, x)


def _xattn_kernel(x_ref, gpre_ref, wq_ref, kv_ref, wo_ref, gpost_ref, o_ref):
    x = x_ref[...]
    h = _bf(_rms(x, gpre_ref[...]))
    q = jnp.dot(h, wq_ref[...], preferred_element_type=F32)
    scale = XATTN_HEAD_DIM ** -0.5
    outs = []
    for hd in range(XATTN_HEADS):
        sl = slice(hd * XATTN_HEAD_DIM, (hd + 1) * XATTN_HEAD_DIM)
        k = kv_ref[0, :, sl]
        v = kv_ref[0, :, XATTN_WIDTH + hd * XATTN_HEAD_DIM:XATTN_WIDTH + (hd + 1) * XATTN_HEAD_DIM]
        sc = _mm_nt(q[:, sl], k) * scale
        sc = sc - jnp.max(sc, axis=-1, keepdims=True)
        e = jnp.exp(sc)
        p = e / jnp.sum(e, axis=-1, keepdims=True)
        outs.append(_mm(p, v))
    o = jnp.concatenate(outs, axis=-1)
    y = jnp.dot(_bf(o), wo_ref[...], preferred_element_type=F32)
    o_ref[...] = x + _rms(y, gpost_ref[...])


def _xattn(x, gpre, wq, kv3, wo, gpost, *, tm, seq):
    t, d = x.shape
    per_seq = seq // tm
    return pl.pallas_call(
        _xattn_kernel,
        out_shape=jax.ShapeDtypeStruct((t, d), F32),
        grid=(t // tm,),
        in_specs=[pl.BlockSpec((tm, d), lambda i: (i, 0)),
                  pl.BlockSpec((1, d), lambda i: (0, 0)),
                  pl.BlockSpec((d, XATTN_WIDTH), lambda i: (0, 0)),
                  pl.BlockSpec((1, MEM_LEN, 2 * XATTN_WIDTH), lambda i: (i // per_seq, 0, 0)),
                  pl.BlockSpec((XATTN_WIDTH, d), lambda i: (0, 0)),
                  pl.BlockSpec((1, d), lambda i: (0, 0))],
        out_specs=pl.BlockSpec((tm, d), lambda i: (i, 0)),
        compiler_params=_params(("parallel",)),
        name="xattn",
    )(x, gpre, wq, kv3, wo, gpost)


FFN_TAIL = SUBLANES


def _gelu_tanh(x):
    return 0.5 * x * (1.0 + jnp.tanh(0.7978845608028654 * (x + 0.044715 * x * x * x)))


def _ffn_kernel(x_ref, gpre_ref, wg_ref, wu_ref, cwg_ref, cwu_ref, cbg_ref, cbu_ref, wd_ref, gpost_ref,
                o_ref, h_ref, tail_ref, *, tiles_per_seq):
    i = pl.program_id(0)
    j = pl.program_id(1)
    tm, tn = x_ref.shape[0], wg_ref.shape[1]

    @pl.when((i == 0) & (j == 0))
    def _():
        tail_ref[...] = jnp.zeros_like(tail_ref)

    @pl.when(j == 0)
    def _():
        h_ref[...] = _bf(_rms(x_ref[...], gpre_ref[...]))
        o_ref[...] = jnp.zeros_like(o_ref)

    prev = jnp.where(i % tiles_per_seq == 0, 0.0, tail_ref[j])
    ug = jnp.dot(h_ref[...], wg_ref[...], preferred_element_type=F32)
    uu = jnp.dot(h_ref[...], wu_ref[...], preferred_element_type=F32)
    tail_ref[j] = jnp.concatenate([ug[tm - FFN_TAIL:, :], uu[tm - FFN_TAIL:, :]], axis=1)

    def conv(u, u_prev, cw_ref, cb_ref):
        u = jnp.concatenate([u_prev, u], axis=0)
        u1 = pltpu.roll(u, 1, 0)
        u2 = pltpu.roll(u, 2, 0)
        cw = cw_ref[...]
        full = cb_ref[...] + cw[2:3, :] * u + cw[1:2, :] * u1 + cw[0:1, :] * u2
        return full[FFN_TAIL:, :]

    gate = conv(ug, prev[:, :tn], cwg_ref, cbg_ref)
    up = conv(uu, prev[:, tn:], cwu_ref, cbu_ref)
    act = _bf(_gelu_tanh(gate) * up)
    o_ref[...] += jnp.dot(act, wd_ref[...], preferred_element_type=F32)

    @pl.when(j == pl.num_programs(1) - 1)
    def _():
        o_ref[...] = x_ref[...] + _rms(o_ref[...], gpost_ref[...])


def _ffn(x, gpre, w_up, conv_w, conv_b, w_down, gpost, *, tm, tn, seq):
    t, d = x.shape
    nj = D_FF // tn
    return pl.pallas_call(
        functools.partial(_ffn_kernel, tiles_per_seq=seq // tm),
        out_shape=jax.ShapeDtypeStruct((t, d), F32),
        grid=(t // tm, nj),
        in_specs=[pl.BlockSpec((tm, d), lambda i, j: (i, 0)),
                  pl.BlockSpec((1, d), lambda i, j: (0, 0)),
                  pl.BlockSpec((d, tn), lambda i, j: (0, j)),
                  pl.BlockSpec((d, tn), lambda i, j: (0, nj + j)),
                  pl.BlockSpec((CONV_WIDTH, tn), lambda i, j: (0, j)),
                  pl.BlockSpec((CONV_WIDTH, tn), lambda i, j: (0, nj + j)),
                  pl.BlockSpec((1, tn), lambda i, j: (0, j)),
                  pl.BlockSpec((1, tn), lambda i, j: (0, nj + j)),
                  pl.BlockSpec((tn, d), lambda i, j: (j, 0)),
                  pl.BlockSpec((1, d), lambda i, j: (0, 0))],
        out_specs=pl.BlockSpec((tm, d), lambda i, j: (i, 0), pipeline_mode=pl.Buffered(1)),
        scratch_shapes=[pltpu.VMEM((tm, d), BF16), pltpu.VMEM((nj, FFN_TAIL, 2 * tn), F32)],
        compiler_params=_params(("arbitrary", "arbitrary")),
        name="conv_glu_ffn",
    )(x, gpre, w_up, w_up, conv_w, conv_w, conv_b, conv_b, w_down, gpost)


def _small_in_proj(w_in):
    d = w_in.shape[1]
    r0, g0 = MLSTM_TOTAL, MLSTM_TOTAL + RWKV_TOTAL
    mlstm_if = w_in[MLSTM_MAIN:r0]
    rwkv_wa = w_in[r0 + RWKV_MAIN:r0 + RWKV_MAIN + WA_W]
    rwkv_g = w_in[r0 + RWKV_MAIN + WA_W:g0]
    z = lambda n: jnp.zeros((n, d), w_in.dtype)
    return jnp.concatenate([rwkv_g, z(RG_PAD - RWKV_GATE_RANK), rwkv_wa, mlstm_if, z(IF_W - 2 * MLSTM_HEADS)], axis=0)


def _layer(x, mem, mix_pre_norm, w_in, mlstm_b_i, mlstm_b_f, mlstm_head_norm, rwkv_mu, rwkv_w0, rwkv_w_up, rwkv_a0,
           rwkv_a_up, rwkv_g_up, rwkv_k_k, rwkv_k_a, rwkv_r_k, rwkv_ln_g, rwkv_ln_b, w_branch_a, w_branch_b,
           w_mix_out, mix_post_norm, xattn_pre_norm, mem_norm, xattn_wq, xattn_wkv, xattn_wo, xattn_post_norm,
           ffn_pre_norm, ffn_w_up, ffn_conv_w, ffn_conv_b, ffn_w_down, ffn_post_norm):
    b, s, d = x.shape
    t = b * s
    x2d = x.reshape(t, d)
    row = lambda p: p.reshape(1, -1).astype(F32)
    tm_row = min(TM_ROW, s)

    w_in_t = w_in.T
    tm_in = min(TM_IN_PROJ, t)
    small, h = _norm_matmul(x2d, row(mix_pre_norm), _small_in_proj(w_in_t), tm=tm_in, tn=SMALL_COLS, out_dtype=F32,
                            name="in_proj_small", w_is_transposed=True)
    small3 = small.reshape(b, s, SMALL_COLS)
    mlstm_main, rwkv_main, gates = _rowranges_matmul(
        h, w_in_t, [(0, MLSTM_MAIN), (MLSTM_TOTAL, RWKV_MAIN), (MLSTM_TOTAL + RWKV_TOTAL, 2 * D_MODEL)],
        tm=tm_in, tn=TN_IN_PROJ, out_dtype=BF16, name="in_proj_wide")

    gate_bias = jnp.concatenate([mlstm_b_i, mlstm_b_f, jnp.zeros((IF_W - 2 * MLSTM_HEADS,), F32)]).reshape(1, IF_W)
    ha, ffn_w_down_b, w_a_b, w_b_b, w_out_b = _mlstm(
        mlstm_main.reshape(b, s, MLSTM_MAIN), small3, gate_bias, row(mlstm_head_norm),
        cast_arrays=(ffn_w_down, w_branch_a, w_branch_b, w_mix_out))

    mu_r, mu_k, mu_v = (rwkv_mu[i * RWKV_WIDTH:(i + 1) * RWKV_WIDTH] for i in range(3))
    mu_wa = rwkv_mu[RWKV_MAIN:RWKV_MAIN + WA_W]
    mu_g = jnp.pad(rwkv_mu[RWKV_MAIN + WA_W:], (0, RG_PAD - RWKV_GATE_RANK))
    pvec = jnp.stack([mu_r, mu_k, mu_v, rwkv_w0, rwkv_a0, rwkv_k_k, rwkv_k_a, rwkv_r_k.reshape(-1), rwkv_ln_g,
                      rwkv_ln_b] + [jnp.zeros((RWKV_WIDTH,), F32)] * (PV_ROWS - 10))
    g_up = jnp.pad(rwkv_g_up, ((0, RG_PAD - RWKV_GATE_RANK), (0, 0)))
    hb, ffn_w_up_b = _rwkv(rwkv_main.reshape(b, s, RWKV_MAIN), small3, pvec, row(mu_g), row(mu_wa), _bf(rwkv_w_up),
                           _bf(rwkv_a_up), _bf(g_up), cast_arrays=(ffn_w_up,))

    x1 = _mix_out(ha.reshape(t, -1), hb.reshape(t, -1), gates, w_a_b, w_b_b, w_out_b, row(mix_post_norm), x2d,
                  tm=tm_row)

    mem2d = mem.reshape(b * MEM_LEN, d)
    kv, _ = _norm_matmul(mem2d, row(mem_norm), _bf(xattn_wkv), tm=MEM_LEN, tn=TN, out_dtype=F32, name="mem_kv")
    x2 = _xattn(x1, row(xattn_pre_norm), _bf(xattn_wq), kv.reshape(b, MEM_LEN, 2 * XATTN_WIDTH), _bf(xattn_wo),
                row(xattn_post_norm), tm=min(TM_XATTN, s), seq=s)

    x3 = _ffn(x2, row(ffn_pre_norm), ffn_w_up_b, ffn_conv_w, row(ffn_conv_b), ffn_w_down_b, row(ffn_post_norm),
              tm=min(TM_FFN, s), tn=TN, seq=s)
    return x3.reshape(b, s, d)


def kernel(x, mem, mix_pre_norm, w_in, mlstm_b_i, mlstm_b_f, mlstm_head_norm, rwkv_mu, rwkv_w0, rwkv_w_up, rwkv_a0, rwkv_a_up, rwkv_g_up, rwkv_k_k, rwkv_k_a, rwkv_r_k, rwkv_ln_g, rwkv_ln_b, w_branch_a, w_branch_b, w_mix_out, mix_post_norm, xattn_pre_norm, mem_norm, xattn_wq, xattn_wkv, xattn_wo, xattn_post_norm, ffn_pre_norm, ffn_w_up, ffn_conv_w, ffn_conv_b, ffn_w_down, ffn_post_norm):
    for l in range(mix_pre_norm.shape[0]):
        x = _layer(x, mem, mix_pre_norm[l], w_in[l], mlstm_b_i[l], mlstm_b_f[l], mlstm_head_norm[l], rwkv_mu[l],
                   rwkv_w0[l], rwkv_w_up[l], rwkv_a0[l], rwkv_a_up[l], rwkv_g_up[l], rwkv_k_k[l], rwkv_k_a[l],
                   rwkv_r_k[l], rwkv_ln_g[l], rwkv_ln_b[l], w_branch_a[l], w_branch_b[l], w_mix_out[l],
                   mix_post_norm[l], xattn_pre_norm[l], mem_norm[l], xattn_wq[l], xattn_wkv[l], xattn_wo[l],
                   xattn_post_norm[l], ffn_pre_norm[l], ffn_w_up[l], ffn_conv_w[l], ffn_conv_b[l], ffn_w_down[l],
                   ffn_post_norm[l])
    return x
```

```python
import functools

import jax
import jax.numpy as jnp
from jax import lax
from jax.experimental import pallas as pl
from jax.experimental.pallas import tpu as pltpu

D_MODEL = 2048
MEM_LEN = 256
RMS_EPS = 1e-6

MLSTM_HEADS = 4
MLSTM_WIDTH = D_MODEL // 2
MLSTM_V_DIM = MLSTM_WIDTH // MLSTM_HEADS
MLSTM_QK_DIM = MLSTM_V_DIM // 2
MLSTM_QK_WIDTH = MLSTM_HEADS * MLSTM_QK_DIM
GATE_SOFTCAP = 15.0

RWKV_WIDTH = D_MODEL // 2
RWKV_HEAD = 64
RWKV_HEADS = RWKV_WIDTH // RWKV_HEAD
RWKV_DECAY_RANK = 64
RWKV_A_RANK = 64
RWKV_GATE_RANK = 160
RWKV_GN_EPS = 64e-5

MLSTM_MAIN = 2 * MLSTM_QK_WIDTH + 2 * MLSTM_WIDTH
MLSTM_TOTAL = MLSTM_MAIN + 2 * MLSTM_HEADS
RWKV_MAIN = 3 * RWKV_WIDTH
RWKV_TOTAL = RWKV_MAIN + RWKV_DECAY_RANK + RWKV_A_RANK + RWKV_GATE_RANK

XATTN_HEADS = 4
XATTN_HEAD_DIM = 128
XATTN_WIDTH = XATTN_HEADS * XATTN_HEAD_DIM

D_FF = 4 * D_MODEL
CONV_WIDTH = 3

RG_PAD = 256
WA_W = RWKV_DECAY_RANK + RWKV_A_RANK
IF_W = 128
COL_RG = 0
COL_WA = COL_RG + RG_PAD
COL_IF = COL_WA + WA_W
SMALL_COLS = COL_IF + IF_W

MLSTM_CHUNK = 256
RWKV_CHUNK = 64
RWKV_CHUNKS_PER_STEP = 2
RWKV_GROUP_W = 256
RWKV_GROUPS = RWKV_WIDTH // RWKV_GROUP_W
VMEM_LIMIT = 56 * 1024 * 1024

LANES = 128
SUBLANES = 8
BF16_SUBLANES = 16
TM_IN_PROJ = 1024
TN_IN_PROJ = 1024
TM_ROW = 512
TM_XATTN = 1024
TM_FFN = 1024
TN = 512

F32 = jnp.float32
BF16 = jnp.bfloat16


def _bf(x):
    return x.astype(BF16)


def _mm(a, b):
    return jnp.dot(_bf(a), _bf(b), preferred_element_type=F32)


def _mm_nt(a, b):
    return lax.dot_general(_bf(a), _bf(b), (((1,), (1,)), ((), ())), preferred_element_type=F32)


def _mm_tn(a, b):
    return lax.dot_general(_bf(a), _bf(b), (((0,), (0,)), ((), ())), preferred_element_type=F32)


def _mm_exact_lhs(tri, x):
    hi = _bf(x)
    r1 = x - hi.astype(F32)
    mid = _bf(r1)
    lo = _bf(r1 - mid.astype(F32))
    t = _bf(tri)
    return (jnp.dot(t, hi, preferred_element_type=F32) + jnp.dot(t, mid, preferred_element_type=F32)
            + jnp.dot(t, lo, preferred_element_type=F32))


def _rms(x, g):
    return x * lax.rsqrt(jnp.mean(x * x, axis=-1, keepdims=True) + RMS_EPS) * g


NORM_ROWS = 16


def _store_residual_norm(o_ref, x_ref, y, g):
    for r0 in range(0, o_ref.shape[0], NORM_ROWS):
        rs = slice(r0, r0 + NORM_ROWS)
        o_ref[rs, :] = x_ref[rs, :] + _rms(y[rs, :], g)


def _softplus(x):
    return jnp.maximum(x, 0.0) + jnp.log(1.0 + jnp.exp(-jnp.abs(x)))


def _sigmoid(x):
    return 1.0 / (1.0 + jnp.exp(-x))


def _params(sem):
    return pltpu.CompilerParams(dimension_semantics=sem, vmem_limit_bytes=VMEM_LIMIT)


def _call_with_casts(body, *, n_in, out_shape, grid, in_specs, out_specs, cast_arrays=(), **kwargs):
    n_out = len(out_shape)
    k = len(cast_arrays)
    nsteps = 1
    for extent in grid:
        nsteps *= extent

    def step(*ids):
        lin = ids[0]
        for extent, idx in zip(grid[1:], ids[1:]):
            lin = lin * extent + idx
        return lin

    band_specs, band_shapes = [], []
    for arr in cast_arrays:
        rows, cols = arr.shape
        assert rows % nsteps == 0 and (rows // nsteps) % BF16_SUBLANES == 0, (arr.shape, nsteps)
        band_specs.append(pl.BlockSpec((rows // nsteps, cols), lambda *ids: (step(*ids), 0)))
        band_shapes.append(jax.ShapeDtypeStruct(arr.shape, BF16))

    def kernel(*refs):
        ins, src = refs[:n_in], refs[n_in:n_in + k]
        outs, dst = refs[n_in + k:n_in + k + n_out], refs[n_in + k + n_out:n_in + 2 * k + n_out]
        for s_ref, d_ref in zip(src, dst):
            d_ref[...] = s_ref[...].astype(d_ref.dtype)
        body(*ins, *outs, *refs[n_in + 2 * k + n_out:])

    def run(*operands):
        assert len(operands) == n_in
        return pl.pallas_call(kernel, out_shape=tuple(out_shape) + tuple(band_shapes), grid=grid,
                              in_specs=list(in_specs) + band_specs, out_specs=tuple(out_specs) + tuple(band_specs),
                              **kwargs)(*operands, *cast_arrays)

    return run


def _norm_matmul_kernel(x_ref, g_ref, w_ref, o_ref, h_ref, *, w_is_transposed):
    @pl.when(pl.program_id(1) == 0)
    def _():
        h_ref[...] = _bf(_rms(x_ref[...], g_ref[...]))

    w = w_ref[...].T if w_is_transposed else w_ref[...]
    o_ref[...] = jnp.dot(h_ref[...], _bf(w), preferred_element_type=F32).astype(o_ref.dtype)


def _norm_matmul(x, g, w, *, tm, tn, out_dtype, name, w_is_transposed=False):
    t, d = x.shape
    n = w.shape[0] if w_is_transposed else w.shape[1]
    w_spec = pl.BlockSpec((tn, d), lambda i, j: (j, 0)) if w_is_transposed else pl.BlockSpec((d, tn), lambda i, j: (0, j))
    return pl.pallas_call(
        functools.partial(_norm_matmul_kernel, w_is_transposed=w_is_transposed),
        out_shape=(jax.ShapeDtypeStruct((t, n), out_dtype), jax.ShapeDtypeStruct((t, d), BF16)),
        grid=(t // tm, n // tn),
        in_specs=[pl.BlockSpec((tm, d), lambda i, j: (i, 0)),
                  pl.BlockSpec((1, d), lambda i, j: (0, 0)),
                  w_spec],
        out_specs=(pl.BlockSpec((tm, tn), lambda i, j: (i, j)),
                   pl.BlockSpec((tm, d), lambda i, j: (i, 0))),
        compiler_params=_params(("parallel", "arbitrary")),
        name=name,
    )(x, g, w)


def _rowranges_matmul_kernel(a_ref, wt_ref, *refs, tile_ranges):
    o_refs, wb_ref = refs[:-1], refs[-1]
    j = pl.program_id(0)

    @pl.when(pl.program_id(1) == 0)
    def _():
        wb_ref[...] = _bf(wt_ref[...].T)

    for o_ref, (first, stop) in zip(o_refs, tile_ranges):
        @pl.when((j >= first) & (j < stop))
        def _(o_ref=o_ref):
            o_ref[...] = jnp.dot(a_ref[...], wb_ref[...], preferred_element_type=F32).astype(o_ref.dtype)


def _rowranges_matmul(a, w_t, ranges, *, tm, tn, out_dtype, name):
    t, d = a.shape
    n_i = t // tm
    tile_ranges, first = [], 0
    for row0, nrows in ranges:
        assert row0 % SUBLANES == 0 and nrows % tn == 0
        tile_ranges.append((first, first + nrows // tn))
        first += nrows // tn

    def weight_row(j, i):
        row = 0
        for (row0, _), (lo, hi) in zip(ranges, tile_ranges):
            row = row + jnp.where((j >= lo) & (j < hi), row0 + tn * (j - lo), 0)
        return pl.multiple_of(row, SUBLANES), 0

    def out_index(lo, hi):
        return lambda j, i: (jnp.where(j < lo, 0, jnp.where(j >= hi, n_i - 1, i)), jnp.clip(j - lo, 0, hi - lo - 1))

    return pl.pallas_call(
        functools.partial(_rowranges_matmul_kernel, tile_ranges=tuple(tile_ranges)),
        out_shape=[jax.ShapeDtypeStruct((t, nrows), out_dtype) for _, nrows in ranges],
        grid=(first, n_i),
        in_specs=[pl.BlockSpec((tm, d), lambda j, i: (i, 0)),
                  pl.BlockSpec((pl.Element(tn), pl.Element(d)), weight_row)],
        out_specs=[pl.BlockSpec((tm, tn), out_index(lo, hi)) for lo, hi in tile_ranges],
        scratch_shapes=[pltpu.VMEM((d, tn), BF16)],
        compiler_params=_params(("arbitrary", "arbitrary")),
        name=name,
    )(a, w_t)


def _mlstm_kernel(q_ref, k_ref, v_ref, o_ref, gate_ref, bias_ref, hn_ref, out_ref, ct_ref, n_ref, m_ref):
    L = MLSTM_CHUNK
    dk, dv = MLSTM_QK_DIM, MLSTM_V_DIM

    @pl.when(pl.program_id(1) == 0)
    def _():
        ct_ref[...] = jnp.zeros_like(ct_ref)
        n_ref[...] = jnp.zeros_like(n_ref)
        m_ref[...] = jnp.zeros_like(m_ref)

    pre = gate_ref[0] + bias_ref[...]
    capped = GATE_SOFTCAP * jnp.tanh(pre / GATE_SOFTCAP)
    logf = -_softplus(-capped)
    row = lax.broadcasted_iota(jnp.int32, (L, L), 0)
    col = lax.broadcasted_iota(jnp.int32, (L, L), 1)
    causal = row >= col
    bcum = _mm_exact_lhs(causal.astype(F32), logf)
    ig_t = capped.T
    bcum_t = bcum.T
    scale = MLSTM_QK_DIM ** -0.5

    H = range(MLSTM_HEADS)
    q = [q_ref[0, :, h * dk:(h + 1) * dk] for h in H]
    k = [k_ref[0, :, h * dk:(h + 1) * dk] for h in H]
    v = [v_ref[0, :, h * dv:(h + 1) * dv] for h in H]
    b_c = [bcum[:, MLSTM_HEADS + h:MLSTM_HEADS + h + 1] for h in H]
    b_r = [bcum_t[MLSTM_HEADS + h:MLSTM_HEADS + h + 1, :] for h in H]
    i_c = [capped[:, h:h + 1] for h in H]
    i_r = [ig_t[h:h + 1, :] for h in H]
    m_prev = [m_ref[h][0:1, 0:1] for h in H]
    ct = [ct_ref[h] for h in H]
    nrow = [n_ref[h][0:1, :] for h in H]

    dmat = [jnp.where(causal, b_c[h] - b_r[h] + i_r[h], -jnp.inf) for h in H]
    inter = [b_c[h] + m_prev[h] for h in H]
    m_t = [jnp.maximum(inter[h], jnp.max(dmat[h], axis=-1, keepdims=True)) for h in H]
    dexp = [jnp.exp(dmat[h] - m_t[h]) * scale for h in H]
    w_inter = [jnp.exp(inter[h] - m_t[h]) for h in H]
    qk = [_mm_nt(q[h], k[h]) for h in H]
    qc = [_mm(q[h], ct[h]) for h in H]
    s = [qk[h] * dexp[h] for h in H]
    sv = [_mm(s[h], v[h]) for h in H]
    num = [w_inter[h] * qc[h] + sv[h] for h in H]
    den = [w_inter[h] * jnp.sum(q[h].astype(F32) * nrow[h], axis=-1, keepdims=True)
           + jnp.sum(s[h], axis=-1, keepdims=True) for h in H]
    hh = [num[h] / jnp.maximum(jnp.abs(den[h]), jnp.exp(-m_t[h])) for h in H]

    b_last = [b_c[h][L - 1:L, :] for h in H]
    gs = [b_last[h] - b_c[h] + i_c[h] for h in H]
    m_new = [jnp.maximum(b_last[h] + m_prev[h], jnp.max(gs[h], axis=0, keepdims=True)) for h in H]
    carry_w = [jnp.exp(b_last[h] + m_prev[h] - m_new[h]) for h in H]
    ws = [jnp.exp(gs[h] - m_new[h]) * scale for h in H]
    kv = [_mm_tn(k[h], ws[h] * v[h].astype(F32)) for h in H]
    for h in H:
        ct_ref[h] = carry_w[h] * ct[h] + kv[h]
        n_ref[h] = jnp.broadcast_to(
            carry_w[h] * nrow[h] + jnp.sum(ws[h] * k[h].astype(F32), axis=0, keepdims=True), n_ref.shape[1:])
        m_ref[h] = jnp.broadcast_to(m_new[h], m_ref.shape[1:])

    ms = [jnp.mean(hh[h] * hh[h], axis=-1, keepdims=True) for h in H]
    for h in H:
        hm = hh[h] * lax.rsqrt(ms[h] + RMS_EPS) * hn_ref[:, h * dv:(h + 1) * dv]
        og = o_ref[0, :, h * dv:(h + 1) * dv].astype(F32)
        out_ref[0, :, h * dv:(h + 1) * dv] = (_sigmoid(og) * hm).astype(out_ref.dtype)


def _mlstm(main3, small3, gate_bias, head_norm, cast_arrays=()):
    b, s, _ = main3.shape
    L = min(MLSTM_CHUNK, s)
    assert L == MLSTM_CHUNK and s % L == 0
    qw, vw = MLSTM_QK_WIDTH, MLSTM_WIDTH
    return _call_with_casts(
        _mlstm_kernel, n_in=7, cast_arrays=cast_arrays,
        out_shape=[jax.ShapeDtypeStruct((b, s, MLSTM_WIDTH), BF16)],
        grid=(b, s // L),
        in_specs=[pl.BlockSpec((1, L, qw), lambda i, c: (i, c, 0)),
                  pl.BlockSpec((1, L, qw), lambda i, c: (i, c, 1)),
                  pl.BlockSpec((1, L, vw), lambda i, c: (i, c, 1)),
                  pl.BlockSpec((1, L, vw), lambda i, c: (i, c, 2)),
                  pl.BlockSpec((1, L, IF_W), lambda i, c: (i, c, COL_IF // IF_W)),
                  pl.BlockSpec((1, IF_W), lambda i, c: (0, 0)),
                  pl.BlockSpec((1, vw), lambda i, c: (0, 0))],
        out_specs=[pl.BlockSpec((1, L, vw), lambda i, c: (i, c, 0))],
        scratch_shapes=[pltpu.VMEM((MLSTM_HEADS, MLSTM_QK_DIM, MLSTM_V_DIM), F32),
                        pltpu.VMEM((MLSTM_HEADS, SUBLANES, MLSTM_QK_DIM), F32),
                        pltpu.VMEM((MLSTM_HEADS, SUBLANES, LANES), F32)],
        compiler_params=_params(("parallel", "arbitrary")),
        name="mlstm",
    )(main3, main3, main3, main3, small3, gate_bias, head_norm)


PV_MU_R, PV_MU_K, PV_MU_V, PV_W0, PV_A0, PV_KK, PV_KA, PV_RK, PV_LNG, PV_LNB = range(10)
PV_ROWS = 16


def _rwkv_kernel(r_ref, k_ref, v_ref, g_ref, wa_ref, pv_ref, mug_ref, muwa_ref, wup_ref, aup_ref, gup_ref, seg_ref,
                 out_ref, s_ref, pr_ref, pk_ref, pvv_ref, pg_ref, pwa_ref):
    L = RWKV_CHUNK
    CH = RWKV_CHUNKS_PER_STEP
    R = CH * L
    N = RWKV_HEAD
    GW = RWKV_GROUP_W
    GH = GW // N

    @pl.when(pl.program_id(1) == 0)
    def _():
        s_ref[...] = jnp.zeros_like(s_ref)
        pr_ref[...] = jnp.zeros_like(pr_ref)
        pk_ref[...] = jnp.zeros_like(pk_ref)
        pvv_ref[...] = jnp.zeros_like(pvv_ref)
        pg_ref[...] = jnp.zeros_like(pg_ref)
        pwa_ref[...] = jnp.zeros_like(pwa_ref)

    groups = range(RWKV_GROUPS)
    rows = lambda c: slice(c * L, (c + 1) * L)
    lanes = lambda g: slice(g * GW, (g + 1) * GW)
    cat = lambda u, w_: jnp.concatenate([u, w_], axis=0)
    seg = seg_ref[...]
    pv = pv_ref[...]

    def shift_lerp(x_ref, prev_ref, mu, ls=slice(None)):
        x = x_ref[0, :, ls].astype(F32)
        rid = lax.broadcasted_iota(jnp.int32, x.shape, 0)
        xs = jnp.where(rid == 0, prev_ref[0:1, ls], pltpu.roll(x, 1, 0))
        prev_ref[0:1, ls] = x[R - 1:R, :]
        return x + (xs - x) * mu

    gl = shift_lerp(g_ref, pg_ref, mug_ref[...])
    wa = shift_lerp(wa_ref, pwa_ref, muwa_ref[...])
    wl_b = _bf(jnp.tanh(wa[:, 0:RWKV_DECAY_RANK]))
    al_b = _bf(wa[:, RWKV_DECAY_RANK:WA_W])
    sgl_b = _bf(_sigmoid(gl))
    tr = lax.broadcasted_iota(jnp.int32, (R, R), 0)
    tc = lax.broadcasted_iota(jnp.int32, (R, R), 1)
    tri = ((tr >= tc) & ((tr // L) == (tc // L))).astype(F32)

    ti = lax.broadcasted_iota(jnp.int32, (L, GW), 0)
    lane = lax.broadcasted_iota(jnp.int32, (L, GW), 1)
    si = lane & (N - 1)
    lane_head = lane // N
    incl = ti >= si
    strict = ti > si
    eye = (ti == si).astype(F32)
    lvl_masks = []
    bsz = 1
    while bsz < L:
        same = (ti // (2 * bsz)) == (si // (2 * bsz))
        lvl_masks.append(jnp.where(same & ((ti & bsz) != 0) & ((si & bsz) == 0), 1.0, 0.0))
        bsz *= 2

    def bd(x):
        return jnp.concatenate([_bf(x)] * GH, axis=0) * seg

    def fold(z):
        acc = z[0:N]
        for hh in range(1, GH):
            acc = jnp.where(lane_head == hh, z[hh * N:(hh + 1) * N], acc)
        return acc

    def mmb(x, b):
        return jnp.dot(_bf(x), b, preferred_element_type=F32)

    def mmb_nt(x, b):
        return lax.dot_general(_bf(x), b, (((1,), (1,)), ((), ())), preferred_element_type=F32)

    prep = []
    for g in groups:
        ls = lanes(g)
        row = lambda i, ls=ls: pv[i:i + 1, ls]
        r = shift_lerp(r_ref, pr_ref, row(PV_MU_R), ls)
        kr = shift_lerp(k_ref, pk_ref, row(PV_MU_K), ls)
        v = shift_lerp(v_ref, pvv_ref, row(PV_MU_V), ls)
        w_log = -_softplus(-(row(PV_W0) + jnp.dot(wl_b, wup_ref[:, ls], preferred_element_type=F32))) - 0.5
        lw = -jnp.exp(w_log)
        a = _sigmoid(row(PV_A0) + jnp.dot(al_b, aup_ref[:, ls], preferred_element_type=F32))
        gate = jnp.dot(sgl_b, gup_ref[:, ls], preferred_element_type=F32)
        cs = _mm_exact_lhs(tri, lw)
        g_incl = jnp.exp(cs)
        g_inv = jnp.exp(-cs)
        kk0 = kr * row(PV_KK)
        kr2 = kr * (1.0 + (a - 1.0) * row(PV_KA))
        sums = jnp.dot(_bf(cat(kk0 * kk0, r * kr2 * row(PV_RK))), seg, preferred_element_type=F32)
        kk = kk0 * lax.rsqrt(jnp.maximum(sums[:R], 1e-24))
        prep.append(dict(p=jnp.exp(cs - lw) * kk, qt=kk * a * g_inv, kt=kr2 * g_inv, rt=r * g_incl, v=v,
                         bonus_v=sums[R:] * v, gate=gate, g_incl=g_incl,
                         ln_g=row(PV_LNG), ln_b=row(PV_LNB)))

    E = [(c, g) for c in range(CH) for g in groups]
    n_e = range(len(E))
    blk = lambda name: [prep[g][name][rows(c), :] for c, g in E]
    p, qt, kt, rt, vv, bonus_v, gate = (blk(n_) for n_ in ("p", "qt", "kt", "rt", "v", "bonus_v", "gate"))
    gl_ = [prep[g]["g_incl"][c * L + L - 1:c * L + L, :] for c, g in E]
    qg = [qt[e] * gl_[e] for e in n_e]
    kg = [kt[e] * gl_[e] for e in n_e]

    lhs_pr = [cat(p[e], rt[e]) for e in n_e]
    gq = [mmb_nt(lhs_pr[e], bd(qt[e])) for e in n_e]
    gk = [mmb_nt(lhs_pr[e], bd(kt[e])) for e in n_e]
    n_pq = [jnp.where(strict, gq[e][:L], 0.0) for e in n_e]
    a_rq = [jnp.where(incl, gq[e][L:], 0.0) for e in n_e]
    a_pk = [jnp.where(strict, gk[e][:L], 0.0) for e in n_e]
    a_rk = [jnp.where(incl, gk[e][L:], 0.0) for e in n_e]

    x = [eye - n_pq[e] * lvl_masks[0] for e in n_e]
    bd_n = [bd(n_pq[e]) for e in n_e]
    for msk in lvl_masks[1:]:
        t1 = [mmb(x[e], bd_n[e]) for e in n_e]
        x = [x[e] - msk * mmb(t1[e], bd(x[e])) for e in n_e]

    av = [mmb(cat(a_pk[e], a_rk[e]), bd(vv[e])) for e in n_e]
    w = [mmb(x[e], bd(p[e])) for e in n_e]
    u0 = [mmb(x[e], bd(av[e][:L])) for e in n_e]
    gmat = [rt[e] - mmb(a_rq[e], bd(w[e])) for e in n_e]
    y0 = [av[e][L:] - mmb(a_rq[e], bd(u0[e])) for e in n_e]
    mt = [eye * gl_[e] - fold(_mm_tn(qg[e], w[e])) for e in n_e]
    bt = [fold(_mm_tn(cat(kg[e], -qg[e]), cat(vv[e], u0[e]))) for e in n_e]

    state = [s_ref[g] for g in groups]
    y = [None] * len(E)
    for c in range(CH):
        es = [c * RWKV_GROUPS + g for g in groups]
        ys = [mmb(cat(gmat[e], mt[e]), bd(state[g])) for g, e in zip(groups, es)]
        for g, e in zip(groups, es):
            y[e] = ys[g][:L] + y0[e]
            state[g] = ys[g][L:] + bt[e]
    for g in groups:
        s_ref[g] = state[g]

    inv_n = 1.0 / N
    sums1 = [mmb(cat(y[e], y[e] * y[e]), seg) for e in n_e]
    for e, (c, g) in enumerate(E):
        mean = sums1[e][:L] * inv_n
        var = sums1[e][L:] * inv_n - mean * mean
        yn = (y[e] - mean) * lax.rsqrt(var + RWKV_GN_EPS) * prep[g]["ln_g"] + prep[g]["ln_b"]
        out_ref[0, rows(c), lanes(g)] = ((yn + bonus_v[e]) * gate[e]).astype(out_ref.dtype)


def _rwkv(main3, small3, pvec, mu_g, mu_wa, w_up, a_up, g_up, cast_arrays=()):
    b, s, _ = main3.shape
    L = RWKV_CHUNK * RWKV_CHUNKS_PER_STEP
    rw = RWKV_WIDTH
    head_of = jnp.arange(RWKV_GROUP_W) // RWKV_HEAD
    seg = (head_of[:, None] == head_of[None, :]).astype(BF16)
    const = lambda shape: pl.BlockSpec(shape, lambda i, c: (0, 0))
    return _call_with_casts(
        _rwkv_kernel, n_in=12, cast_arrays=cast_arrays,
        out_shape=[jax.ShapeDtypeStruct((b, s, rw), BF16)],
        grid=(b, s // L),
        in_specs=[pl.BlockSpec((1, L, rw), lambda i, c: (i, c, 0)),
                  pl.BlockSpec((1, L, rw), lambda i, c: (i, c, 1)),
                  pl.BlockSpec((1, L, rw), lambda i, c: (i, c, 2)),
                  pl.BlockSpec((1, L, RG_PAD), lambda i, c: (i, c, COL_RG // RG_PAD)),
                  pl.BlockSpec((1, L, WA_W), lambda i, c: (i, c, COL_WA // WA_W)),
                  const((PV_ROWS, rw)), const((1, RG_PAD)), const((1, WA_W)),
                  const((RWKV_DECAY_RANK, rw)), const((RWKV_A_RANK, rw)), const((RG_PAD, rw)),
                  const((RWKV_GROUP_W, RWKV_GROUP_W))],
        out_specs=[pl.BlockSpec((1, L, rw), lambda i, c: (i, c, 0))],
        scratch_shapes=[pltpu.VMEM((RWKV_GROUPS, RWKV_HEAD, RWKV_GROUP_W), F32),
                        pltpu.VMEM((SUBLANES, rw), F32), pltpu.VMEM((SUBLANES, rw), F32),
                        pltpu.VMEM((SUBLANES, rw), F32), pltpu.VMEM((SUBLANES, RG_PAD), F32),
                        pltpu.VMEM((SUBLANES, WA_W), F32)],
        compiler_params=_params(("parallel", "arbitrary")),
        name="rwkv7",
    )(main3, main3, main3, small3, small3, pvec, mu_g, mu_wa, w_up, a_up, g_up, seg)


def _mix_out_kernel(ha_ref, hb_ref, ga_ref, gb_ref, wa_ref, wb_ref, wo_ref, g_ref, x_ref, o_ref):
    ya = jnp.dot(ha_ref[...], wa_ref[...], preferred_element_type=F32)
    yb = jnp.dot(hb_ref[...], wb_ref[...], preferred_element_type=F32)
    merged = _bf(_sigmoid(ga_ref[...].astype(F32)) * ya + _sigmoid(gb_ref[...].astype(F32)) * yb)
    y = jnp.dot(merged, wo_ref[...], preferred_element_type=F32)
    _store_residual_norm(o_ref, x_ref, y, g_ref[...])


def _mix_out(ha, hb, gates, wa, wb, wo, g, x, *, tm):
    t, kdim = ha.shape
    d = wo.shape[1]
    resident = lambda shape: pl.BlockSpec(shape, lambda i: (0, 0), pipeline_mode=pl.Buffered(1))
    return pl.pallas_call(
        _mix_out_kernel,
        out_shape=jax.ShapeDtypeStruct((t, d), F32),
        grid=(t // tm,),
        in_specs=[pl.BlockSpec((tm, kdim), lambda i: (i, 0)),
                  pl.BlockSpec((tm, kdim), lambda i: (i, 0)),
                  pl.BlockSpec((tm, d), lambda i: (i, 0)),
                  pl.BlockSpec((tm, d), lambda i: (i, 1)),
                  resident((kdim, d)), resident((kdim, d)), resident((d, d)),
                  pl.BlockSpec((1, d), lambda i: (0, 0)),
                  pl.BlockSpec((tm, d), lambda i: (i, 0))],
        out_specs=pl.BlockSpec((tm, d), lambda i: (i, 0)),
        compiler_params=_params(("parallel",)),
        name="mix_out",
    )(ha, hb, gates, gates, wa, wb, wo, g, x)


def _xattn_kernel(x_ref, gpre_ref, wq_ref, kv_ref, wo_ref, gpost_ref, o_ref):
    h = jnp.concatenate([_bf(_rms(x_ref[r0:r0 + NORM_ROWS, :], gpre_ref[...]))
                         for r0 in range(0, x_ref.shape[0], NORM_ROWS)], axis=0)
    q = jnp.dot(h, wq_ref[...], preferred_element_type=F32)
    scale = XATTN_HEAD_DIM ** -0.5
    heads = range(XATTN_HEADS)
    sl = lambda hd: slice(hd * XATTN_HEAD_DIM, (hd + 1) * XATTN_HEAD_DIM)
    sc = [_mm_nt(q[:, sl(hd)], kv_ref[0, :, sl(hd)]) * scale for hd in heads]
    e = [jnp.exp(sc[hd] - jnp.max(sc[hd], axis=-1, keepdims=True)) for hd in heads]
    p = [e[hd] / jnp.sum(e[hd], axis=-1, keepdims=True) for hd in heads]
    outs = [_mm(p[hd], kv_ref[0, :, XATTN_WIDTH + hd * XATTN_HEAD_DIM:XATTN_WIDTH + (hd + 1) * XATTN_HEAD_DIM])
            for hd in heads]
    o = jnp.concatenate(outs, axis=-1)
    y = jnp.dot(_bf(o), wo_ref[...], preferred_element_type=F32)
    _store_residual_norm(o_ref, x_ref, y, gpost_ref[...])


def _xattn(x, gpre, wq, kv3, wo, gpost, *, tm, seq):
    t, d = x.shape
    per_seq = seq // tm
    return pl.pallas_call(
        _xattn_kernel,
        out_shape=jax.ShapeDtypeStruct((t, d), F32),
        grid=(t // tm,),
        in_specs=[pl.BlockSpec((tm, d), lambda i: (i, 0)),
                  pl.BlockSpec((1, d), lambda i: (0, 0)),
                  pl.BlockSpec((d, XATTN_WIDTH), lambda i: (0, 0)),
                  pl.BlockSpec((1, MEM_LEN, 2 * XATTN_WIDTH), lambda i: (i // per_seq, 0, 0)),
                  pl.BlockSpec((XATTN_WIDTH, d), lambda i: (0, 0)),
                  pl.BlockSpec((1, d), lambda i: (0, 0))],
        out_specs=pl.BlockSpec((tm, d), lambda i: (i, 0)),
        compiler_params=_params(("parallel",)),
        name="xattn",
    )(x, gpre, wq, kv3, wo, gpost)


FFN_TAIL = SUBLANES


def _gelu_tanh(x):
    return 0.5 * x * (1.0 + jnp.tanh(0.7978845608028654 * (x + 0.044715 * x * x * x)))


def _ffn_kernel(x_ref, gpre_ref, wg_ref, wu_ref, cwg_ref, cwu_ref, cbg_ref, cbu_ref, wd_ref, gpost_ref,
                o_ref, h_ref, tail_ref, *, tiles_per_seq):
    i = pl.program_id(0)
    j = pl.program_id(1)
    tm, tn = x_ref.shape[0], wg_ref.shape[1]

    @pl.when((i == 0) & (j == 0))
    def _():
        tail_ref[...] = jnp.zeros_like(tail_ref)

    @pl.when(j == 0)
    def _():
        for r0 in range(0, tm, NORM_ROWS):
            rs = slice(r0, r0 + NORM_ROWS)
            h_ref[rs, :] = _bf(_rms(x_ref[rs, :], gpre_ref[...]))
        o_ref[...] = jnp.zeros_like(o_ref)

    prev = jnp.where(i % tiles_per_seq == 0, 0.0, tail_ref[j])
    ug = jnp.dot(h_ref[...], wg_ref[...], preferred_element_type=F32)
    uu = jnp.dot(h_ref[...], wu_ref[...], preferred_element_type=F32)
    tail_ref[j] = jnp.concatenate([ug[tm - FFN_TAIL:, :], uu[tm - FFN_TAIL:, :]], axis=1)

    def conv(u, u_prev, cw_ref, cb_ref):
        u = jnp.concatenate([u_prev, u], axis=0)
        u1 = pltpu.roll(u, 1, 0)
        u2 = pltpu.roll(u, 2, 0)
        cw = cw_ref[...]
        full = cb_ref[...] + cw[2:3, :] * u + cw[1:2, :] * u1 + cw[0:1, :] * u2
        return full[FFN_TAIL:, :]

    gate = conv(ug, prev[:, :tn], cwg_ref, cbg_ref)
    up = conv(uu, prev[:, tn:], cwu_ref, cbu_ref)
    act = _bf(_gelu_tanh(gate) * up)
    o_ref[...] += jnp.dot(act, wd_ref[...], preferred_element_type=F32)

    @pl.when(j == pl.num_programs(1) - 1)
    def _():
        _store_residual_norm(o_ref, x_ref, o_ref, gpost_ref[...])


def _ffn(x, gpre, w_up, conv_w, conv_b, w_down, gpost, *, tm, tn, seq):
    t, d = x.shape
    nj = D_FF // tn
    return pl.pallas_call(
        functools.partial(_ffn_kernel, tiles_per_seq=seq // tm),
        out_shape=jax.ShapeDtypeStruct((t, d), F32),
        grid=(t // tm, nj),
        in_specs=[pl.BlockSpec((tm, d), lambda i, j: (i, 0)),
                  pl.BlockSpec((1, d), lambda i, j: (0, 0)),
                  pl.BlockSpec((d, tn), lambda i, j: (0, j)),
                  pl.BlockSpec((d, tn), lambda i, j: (0, nj + j)),
                  pl.BlockSpec((CONV_WIDTH, tn), lambda i, j: (0, j)),
                  pl.BlockSpec((CONV_WIDTH, tn), lambda i, j: (0, nj + j)),
                  pl.BlockSpec((1, tn), lambda i, j: (0, j)),
                  pl.BlockSpec((1, tn), lambda i, j: (0, nj + j)),
                  pl.BlockSpec((tn, d), lambda i, j: (j, 0)),
                  pl.BlockSpec((1, d), lambda i, j: (0, 0))],
        out_specs=pl.BlockSpec((tm, d), lambda i, j: (i, 0)),
        scratch_shapes=[pltpu.VMEM((tm, d), BF16), pltpu.VMEM((nj, FFN_TAIL, 2 * tn), F32)],
        compiler_params=_params(("arbitrary", "arbitrary")),
        name="conv_glu_ffn",
    )(x, gpre, w_up, w_up, conv_w, conv_w, conv_b, conv_b, w_down, gpost)


def _small_in_proj(w_in):
    d = w_in.shape[1]
    r0, g0 = MLSTM_TOTAL, MLSTM_TOTAL + RWKV_TOTAL
    mlstm_if = w_in[MLSTM_MAIN:r0]
    rwkv_wa = w_in[r0 + RWKV_MAIN:r0 + RWKV_MAIN + WA_W]
    rwkv_g = w_in[r0 + RWKV_MAIN + WA_W:g0]
    z = lambda n: jnp.zeros((n, d), w_in.dtype)
    return jnp.concatenate([rwkv_g, z(RG_PAD - RWKV_GATE_RANK), rwkv_wa, mlstm_if, z(IF_W - 2 * MLSTM_HEADS)], axis=0)


def _layer(x, mem, mix_pre_norm, w_in, mlstm_b_i, mlstm_b_f, mlstm_head_norm, rwkv_mu, rwkv_w0, rwkv_w_up, rwkv_a0,
           rwkv_a_up, rwkv_g_up, rwkv_k_k, rwkv_k_a, rwkv_r_k, rwkv_ln_g, rwkv_ln_b, w_branch_a, w_branch_b,
           w_mix_out, mix_post_norm, xattn_pre_norm, mem_norm, xattn_wq, xattn_wkv, xattn_wo, xattn_post_norm,
           ffn_pre_norm, ffn_w_up, ffn_conv_w, ffn_conv_b, ffn_w_down, ffn_post_norm):
    b, s, d = x.shape
    t = b * s
    x2d = x.reshape(t, d)
    row = lambda p: p.reshape(1, -1).astype(F32)
    tm_row = min(TM_ROW, s)

    w_in_t = w_in.T
    tm_in = min(TM_IN_PROJ, t)
    small, h = _norm_matmul(x2d, row(mix_pre_norm), _small_in_proj(w_in_t), tm=tm_in, tn=SMALL_COLS, out_dtype=F32,
                            name="in_proj_small", w_is_transposed=True)
    small3 = small.reshape(b, s, SMALL_COLS)
    mlstm_main, rwkv_main, gates = _rowranges_matmul(
        h, w_in_t, [(0, MLSTM_MAIN), (MLSTM_TOTAL, RWKV_MAIN), (MLSTM_TOTAL + RWKV_TOTAL, 2 * D_MODEL)],
        tm=tm_in, tn=TN_IN_PROJ, out_dtype=BF16, name="in_proj_wide")

    gate_bias = jnp.concatenate([mlstm_b_i, mlstm_b_f, jnp.zeros((IF_W - 2 * MLSTM_HEADS,), F32)]).reshape(1, IF_W)
    ha, ffn_w_down_b, w_a_b, w_b_b, w_out_b = _mlstm(
        mlstm_main.reshape(b, s, MLSTM_MAIN), small3, gate_bias, row(mlstm_head_norm),
        cast_arrays=(ffn_w_down, w_branch_a, w_branch_b, w_mix_out))

    mu_r, mu_k, mu_v = (rwkv_mu[i * RWKV_WIDTH:(i + 1) * RWKV_WIDTH] for i in range(3))
    mu_wa = rwkv_mu[RWKV_MAIN:RWKV_MAIN + WA_W]
    mu_g = jnp.pad(rwkv_mu[RWKV_MAIN + WA_W:], (0, RG_PAD - RWKV_GATE_RANK))
    pvec = jnp.stack([mu_r, mu_k, mu_v, rwkv_w0, rwkv_a0, rwkv_k_k, rwkv_k_a, rwkv_r_k.reshape(-1), rwkv_ln_g,
                      rwkv_ln_b] + [jnp.zeros((RWKV_WIDTH,), F32)] * (PV_ROWS - 10))
    g_up = jnp.pad(rwkv_g_up, ((0, RG_PAD - RWKV_GATE_RANK), (0, 0)))
    hb, ffn_w_up_b = _rwkv(rwkv_main.reshape(b, s, RWKV_MAIN), small3, pvec, row(mu_g), row(mu_wa), _bf(rwkv_w_up),
                           _bf(rwkv_a_up), _bf(g_up), cast_arrays=(ffn_w_up,))

    x1 = _mix_out(ha.reshape(t, -1), hb.reshape(t, -1), gates, w_a_b, w_b_b, w_out_b, row(mix_post_norm), x2d,
                  tm=tm_row)

    mem2d = mem.reshape(b * MEM_LEN, d)
    kv, _ = _norm_matmul(mem2d, row(mem_norm), _bf(xattn_wkv), tm=MEM_LEN, tn=TN, out_dtype=F32, name="mem_kv")
    x2 = _xattn(x1, row(xattn_pre_norm), _bf(xattn_wq), kv.reshape(b, MEM_LEN, 2 * XATTN_WIDTH), _bf(xattn_wo),
                row(xattn_post_norm), tm=min(TM_XATTN, s), seq=s)

    x3 = _ffn(x2, row(ffn_pre_norm), ffn_w_up_b, ffn_conv_w, row(ffn_conv_b), ffn_w_down_b, row(ffn_post_norm),
              tm=min(TM_FFN, s), tn=TN, seq=s)
    return x3.reshape(b, s, d)


def kernel(x, mem, mix_pre_norm, w_in, mlstm_b_i, mlstm_b_f, mlstm_head_norm, rwkv_mu, rwkv_w0, rwkv_w_up, rwkv_a0, rwkv_a_up, rwkv_g_up, rwkv_k_k, rwkv_k_a, rwkv_r_k, rwkv_ln_g, rwkv_ln_b, w_branch_a, w_branch_b, w_mix_out, mix_post_norm, xattn_pre_norm, mem_norm, xattn_wq, xattn_wkv, xattn_wo, xattn_post_norm, ffn_pre_norm, ffn_w_up, ffn_conv_w, ffn_conv_b, ffn_w_down, ffn_post_norm):
    for l in range(mix_pre_norm.shape[0]):
        x = _layer(x, mem, mix_pre_norm[l], w_in[l], mlstm_b_i[l], mlstm_b_f[l], mlstm_head_norm[l], rwkv_mu[l],
                   rwkv_w0[l], rwkv_w_up[l], rwkv_a0[l], rwkv_a_up[l], rwkv_g_up[l], rwkv_k_k[l], rwkv_k_a[l],
                   rwkv_r_k[l], rwkv_ln_g[l], rwkv_ln_b[l], w_branch_a[l], w_branch_b[l], w_mix_out[l],
                   mix_post_norm[l], xattn_pre_norm[l], mem_norm[l], xattn_wq[l], xattn_wkv[l], xattn_wo[l],
                   xattn_post_norm[l], ffn_pre_norm[l], ffn_w_up[l], ffn_conv_w[l], ffn_conv_b[l], ffn_w_down[l],
                   ffn_post_norm[l])
    return x
```

```python
import functools

import jax
import jax.numpy as jnp
from jax import lax
from jax.experimental import pallas as pl
from jax.experimental.pallas import tpu as pltpu

D_MODEL = 2048
MEM_LEN = 256
RMS_EPS = 1e-6

MLSTM_HEADS = 4
MLSTM_WIDTH = D_MODEL // 2
MLSTM_V_DIM = MLSTM_WIDTH // MLSTM_HEADS
MLSTM_QK_DIM = MLSTM_V_DIM // 2
MLSTM_QK_WIDTH = MLSTM_HEADS * MLSTM_QK_DIM
GATE_SOFTCAP = 15.0

RWKV_WIDTH = D_MODEL // 2
RWKV_HEAD = 64
RWKV_HEADS = RWKV_WIDTH // RWKV_HEAD
RWKV_DECAY_RANK = 64
RWKV_A_RANK = 64
RWKV_GATE_RANK = 160
RWKV_GN_EPS = 64e-5

MLSTM_MAIN = 2 * MLSTM_QK_WIDTH + 2 * MLSTM_WIDTH
MLSTM_TOTAL = MLSTM_MAIN + 2 * MLSTM_HEADS
RWKV_MAIN = 3 * RWKV_WIDTH
RWKV_TOTAL = RWKV_MAIN + RWKV_DECAY_RANK + RWKV_A_RANK + RWKV_GATE_RANK

XATTN_HEADS = 4
XATTN_HEAD_DIM = 128
XATTN_WIDTH = XATTN_HEADS * XATTN_HEAD_DIM

D_FF = 4 * D_MODEL
CONV_WIDTH = 3

RG_PAD = 256
WA_W = RWKV_DECAY_RANK + RWKV_A_RANK
IF_W = 128
COL_RG = 0
COL_WA = COL_RG + RG_PAD
COL_IF = COL_WA + WA_W
SMALL_COLS = COL_IF + IF_W

MLSTM_CHUNK = 256
RWKV_CHUNK = 64
RWKV_CHUNKS_PER_STEP = 2
RWKV_GROUP_W = 256
RWKV_GROUPS = RWKV_WIDTH // RWKV_GROUP_W
VMEM_LIMIT = 56 * 1024 * 1024

LANES = 128
SUBLANES = 8
BF16_SUBLANES = 16
TM_NORM_PASS = 512
TM_IN_PROJ = 1024
TN_IN_PROJ = 1024
TM_ROW = 512
TM_XATTN = 1024
TM_FFN = 1024
TN = 512

F32 = jnp.float32
BF16 = jnp.bfloat16


def _bf(x):
    return x.astype(BF16)


def _mm(a, b):
    return jnp.dot(_bf(a), _bf(b), preferred_element_type=F32)


def _mm_nt(a, b):
    return lax.dot_general(_bf(a), _bf(b), (((1,), (1,)), ((), ())), preferred_element_type=F32)


def _mm_tn(a, b):
    return lax.dot_general(_bf(a), _bf(b), (((0,), (0,)), ((), ())), preferred_element_type=F32)


def _mm_exact_lhs(tri, x):
    hi = _bf(x)
    r1 = x - hi.astype(F32)
    mid = _bf(r1)
    lo = _bf(r1 - mid.astype(F32))
    t = _bf(tri)
    return (jnp.dot(t, hi, preferred_element_type=F32) + jnp.dot(t, mid, preferred_element_type=F32)
            + jnp.dot(t, lo, preferred_element_type=F32))


def _rms(x, g):
    return x * lax.rsqrt(jnp.mean(x * x, axis=-1, keepdims=True) + RMS_EPS) * g


NORM_ROWS = 64


def _store_residual_norm(o_ref, x_ref, y, g):
    for r0 in range(0, o_ref.shape[0], NORM_ROWS):
        rs = slice(r0, r0 + NORM_ROWS)
        o_ref[rs, :] = x_ref[rs, :] + _rms(y[rs, :], g)


def _softplus(x):
    return jnp.maximum(x, 0.0) + jnp.log(1.0 + jnp.exp(-jnp.abs(x)))


def _sigmoid(x):
    return 1.0 / (1.0 + jnp.exp(-x))


def _params(sem):
    return pltpu.CompilerParams(dimension_semantics=sem, vmem_limit_bytes=VMEM_LIMIT)


def _call_with_casts(body, *, n_in, out_shape, grid, in_specs, out_specs, cast_arrays=(), **kwargs):
    n_out = len(out_shape)
    k = len(cast_arrays)
    nsteps = 1
    for extent in grid:
        nsteps *= extent

    def step(*ids):
        lin = ids[0]
        for extent, idx in zip(grid[1:], ids[1:]):
            lin = lin * extent + idx
        return lin

    band_specs, band_shapes = [], []
    for arr in cast_arrays:
        rows, cols = arr.shape
        assert rows % nsteps == 0 and (rows // nsteps) % BF16_SUBLANES == 0, (arr.shape, nsteps)
        band_specs.append(pl.BlockSpec((rows // nsteps, cols), lambda *ids: (step(*ids), 0)))
        band_shapes.append(jax.ShapeDtypeStruct(arr.shape, BF16))

    def kernel(*refs):
        ins, src = refs[:n_in], refs[n_in:n_in + k]
        outs, dst = refs[n_in + k:n_in + k + n_out], refs[n_in + k + n_out:n_in + 2 * k + n_out]
        for s_ref, d_ref in zip(src, dst):
            d_ref[...] = s_ref[...].astype(d_ref.dtype)
        body(*ins, *outs, *refs[n_in + 2 * k + n_out:])

    def run(*operands):
        assert len(operands) == n_in
        return pl.pallas_call(kernel, out_shape=tuple(out_shape) + tuple(band_shapes), grid=grid,
                              in_specs=list(in_specs) + band_specs, out_specs=tuple(out_specs) + tuple(band_specs),
                              **kwargs)(*operands, *cast_arrays)

    return run


def _norm_matmul_kernel(x_ref, g_ref, w_ref, o_ref, h_ref, *, w_is_transposed):
    @pl.when(pl.program_id(1) == 0)
    def _():
        h_ref[...] = _bf(_rms(x_ref[...], g_ref[...]))

    w = w_ref[...].T if w_is_transposed else w_ref[...]
    o_ref[...] = jnp.dot(h_ref[...], _bf(w), preferred_element_type=F32).astype(o_ref.dtype)


def _norm_matmul(x, g, w, *, tm, tn, out_dtype, name, w_is_transposed=False):
    t, d = x.shape
    n = w.shape[0] if w_is_transposed else w.shape[1]
    w_spec = pl.BlockSpec((tn, d), lambda i, j: (j, 0)) if w_is_transposed else pl.BlockSpec((d, tn), lambda i, j: (0, j))
    return pl.pallas_call(
        functools.partial(_norm_matmul_kernel, w_is_transposed=w_is_transposed),
        out_shape=(jax.ShapeDtypeStruct((t, n), out_dtype), jax.ShapeDtypeStruct((t, d), BF16)),
        grid=(t // tm, n // tn),
        in_specs=[pl.BlockSpec((tm, d), lambda i, j: (i, 0)),
                  pl.BlockSpec((1, d), lambda i, j: (0, 0)),
                  w_spec],
        out_specs=(pl.BlockSpec((tm, tn), lambda i, j: (i, j)),
                   pl.BlockSpec((tm, d), lambda i, j: (i, 0))),
        compiler_params=_params(("parallel", "arbitrary")),
        name=name,
    )(x, g, w)


NORM_PASS_IN_BUFFERS = 3
NORM_PASS_OUT_BUFFERS = 2


def _norm_pass_kernel(x_hbm, g_ref, wt_ref, small_hbm, h_hbm, xbuf, hbuf, sbuf, wb_ref, in_sem, h_sem, s_sem, *, tm):
    n_tiles = x_hbm.shape[0] // tm
    rows = lambda i: pl.ds(i * tm, tm)
    copy_in = lambda i: pltpu.make_async_copy(x_hbm.at[rows(i)], xbuf.at[i % NORM_PASS_IN_BUFFERS],
                                              in_sem.at[i % NORM_PASS_IN_BUFFERS])
    copy_h = lambda i: pltpu.make_async_copy(hbuf.at[i % NORM_PASS_OUT_BUFFERS], h_hbm.at[rows(i)],
                                             h_sem.at[i % NORM_PASS_OUT_BUFFERS])
    copy_s = lambda i: pltpu.make_async_copy(sbuf.at[i % NORM_PASS_OUT_BUFFERS], small_hbm.at[rows(i)],
                                             s_sem.at[i % NORM_PASS_OUT_BUFFERS])

    for i in range(min(NORM_PASS_IN_BUFFERS, n_tiles)):
        copy_in(i).start()
    wb_ref[...] = _bf(wt_ref[...].T)
    for i in range(n_tiles):
        slot_in, slot_out = i % NORM_PASS_IN_BUFFERS, i % NORM_PASS_OUT_BUFFERS
        copy_in(i).wait()
        if i >= NORM_PASS_OUT_BUFFERS:
            copy_h(i - NORM_PASS_OUT_BUFFERS).wait()
            copy_s(i - NORM_PASS_OUT_BUFFERS).wait()
        h = _bf(_rms(xbuf[slot_in], g_ref[...]))
        hbuf[slot_out] = h
        sbuf[slot_out] = jnp.dot(h, wb_ref[...], preferred_element_type=F32)
        copy_h(i).start()
        copy_s(i).start()
        if i + NORM_PASS_IN_BUFFERS < n_tiles:
            copy_in(i + NORM_PASS_IN_BUFFERS).start()
    for i in range(max(n_tiles - NORM_PASS_OUT_BUFFERS, 0), n_tiles):
        copy_h(i).wait()
        copy_s(i).wait()


def _norm_pass(x, g, w_t, *, tm, name):
    t, d = x.shape
    n = w_t.shape[0]
    assert t % tm == 0
    hbm = pl.BlockSpec(memory_space=pl.ANY)
    vmem = pl.BlockSpec(memory_space=pltpu.VMEM)
    return pl.pallas_call(
        functools.partial(_norm_pass_kernel, tm=tm),
        out_shape=(jax.ShapeDtypeStruct((t, n), F32), jax.ShapeDtypeStruct((t, d), BF16)),
        in_specs=[hbm, vmem, vmem],
        out_specs=(hbm, hbm),
        scratch_shapes=[pltpu.VMEM((NORM_PASS_IN_BUFFERS, tm, d), F32),
                        pltpu.VMEM((NORM_PASS_OUT_BUFFERS, tm, d), BF16),
                        pltpu.VMEM((NORM_PASS_OUT_BUFFERS, tm, n), F32),
                        pltpu.VMEM((d, n), BF16),
                        pltpu.SemaphoreType.DMA((NORM_PASS_IN_BUFFERS,)),
                        pltpu.SemaphoreType.DMA((NORM_PASS_OUT_BUFFERS,)),
                        pltpu.SemaphoreType.DMA((NORM_PASS_OUT_BUFFERS,))],
        compiler_params=pltpu.CompilerParams(vmem_limit_bytes=VMEM_LIMIT),
        name=name,
    )(x, g, w_t)


def _rowranges_matmul_kernel(a_ref, wt_ref, *refs, tile_ranges):
    o_refs, wb_ref = refs[:-1], refs[-1]
    j = pl.program_id(0)

    @pl.when(pl.program_id(1) == 0)
    def _():
        wb_ref[...] = _bf(wt_ref[...].T)

    for o_ref, (first, stop) in zip(o_refs, tile_ranges):
        @pl.when((j >= first) & (j < stop))
        def _(o_ref=o_ref):
            o_ref[...] = jnp.dot(a_ref[...], wb_ref[...], preferred_element_type=F32).astype(o_ref.dtype)


def _rowranges_matmul(a, w_t, ranges, *, tm, tn, out_dtype, name):
    t, d = a.shape
    n_i = t // tm
    tile_ranges, first = [], 0
    for row0, nrows in ranges:
        assert row0 % SUBLANES == 0 and nrows % tn == 0
        tile_ranges.append((first, first + nrows // tn))
        first += nrows // tn

    def weight_row(j, i):
        row = 0
        for (row0, _), (lo, hi) in zip(ranges, tile_ranges):
            row = row + jnp.where((j >= lo) & (j < hi), row0 + tn * (j - lo), 0)
        return pl.multiple_of(row, SUBLANES), 0

    def out_index(lo, hi):
        return lambda j, i: (jnp.where(j < lo, 0, jnp.where(j >= hi, n_i - 1, i)), jnp.clip(j - lo, 0, hi - lo - 1))

    return pl.pallas_call(
        functools.partial(_rowranges_matmul_kernel, tile_ranges=tuple(tile_ranges)),
        out_shape=[jax.ShapeDtypeStruct((t, nrows), out_dtype) for _, nrows in ranges],
        grid=(first, n_i),
        in_specs=[pl.BlockSpec((tm, d), lambda j, i: (i, 0)),
                  pl.BlockSpec((pl.Element(tn), pl.Element(d)), weight_row)],
        out_specs=[pl.BlockSpec((tm, tn), out_index(lo, hi)) for lo, hi in tile_ranges],
        scratch_shapes=[pltpu.VMEM((d, tn), BF16)],
        compiler_params=_params(("arbitrary", "arbitrary")),
        name=name,
    )(a, w_t)


def _mlstm_kernel(q_ref, k_ref, v_ref, o_ref, gate_ref, bias_ref, hn_ref, out_ref, ct_ref, n_ref, m_ref):
    L = MLSTM_CHUNK
    dk, dv = MLSTM_QK_DIM, MLSTM_V_DIM

    @pl.when(pl.program_id(1) == 0)
    def _():
        ct_ref[...] = jnp.zeros_like(ct_ref)
        n_ref[...] = jnp.zeros_like(n_ref)
        m_ref[...] = jnp.zeros_like(m_ref)

    pre = gate_ref[0] + bias_ref[...]
    capped = GATE_SOFTCAP * jnp.tanh(pre / GATE_SOFTCAP)
    logf = -_softplus(-capped)
    row = lax.broadcasted_iota(jnp.int32, (L, L), 0)
    col = lax.broadcasted_iota(jnp.int32, (L, L), 1)
    causal = row >= col
    bcum = _mm_exact_lhs(causal.astype(F32), logf)
    ig_t = capped.T
    bcum_t = bcum.T
    scale = MLSTM_QK_DIM ** -0.5

    H = range(MLSTM_HEADS)
    q = [q_ref[0, :, h * dk:(h + 1) * dk] for h in H]
    k = [k_ref[0, :, h * dk:(h + 1) * dk] for h in H]
    v = [v_ref[0, :, h * dv:(h + 1) * dv] for h in H]
    b_c = [bcum[:, MLSTM_HEADS + h:MLSTM_HEADS + h + 1] for h in H]
    b_r = [bcum_t[MLSTM_HEADS + h:MLSTM_HEADS + h + 1, :] for h in H]
    i_c = [capped[:, h:h + 1] for h in H]
    i_r = [ig_t[h:h + 1, :] for h in H]
    m_prev = [m_ref[h][0:1, 0:1] for h in H]
    ct = [ct_ref[h] for h in H]
    nrow = [n_ref[h][0:1, :] for h in H]

    dmat = [jnp.where(causal, b_c[h] - b_r[h] + i_r[h], -jnp.inf) for h in H]
    inter = [b_c[h] + m_prev[h] for h in H]
    m_t = [jnp.maximum(inter[h], jnp.max(dmat[h], axis=-1, keepdims=True)) for h in H]
    dexp = [jnp.exp(dmat[h] - m_t[h]) * scale for h in H]
    w_inter = [jnp.exp(inter[h] - m_t[h]) for h in H]
    qk = [_mm_nt(q[h], k[h]) for h in H]
    qc = [_mm(q[h], ct[h]) for h in H]
    s = [qk[h] * dexp[h] for h in H]
    sv = [_mm(s[h], v[h]) for h in H]
    num = [w_inter[h] * qc[h] + sv[h] for h in H]
    den = [w_inter[h] * jnp.sum(q[h].astype(F32) * nrow[h], axis=-1, keepdims=True)
           + jnp.sum(s[h], axis=-1, keepdims=True) for h in H]
    hh = [num[h] / jnp.maximum(jnp.abs(den[h]), jnp.exp(-m_t[h])) for h in H]

    b_last = [b_c[h][L - 1:L, :] for h in H]
    gs = [b_last[h] - b_c[h] + i_c[h] for h in H]
    m_new = [jnp.maximum(b_last[h] + m_prev[h], jnp.max(gs[h], axis=0, keepdims=True)) for h in H]
    carry_w = [jnp.exp(b_last[h] + m_prev[h] - m_new[h]) for h in H]
    ws = [jnp.exp(gs[h] - m_new[h]) * scale for h in H]
    kv = [_mm_tn(k[h], ws[h] * v[h].astype(F32)) for h in H]
    for h in H:
        ct_ref[h] = carry_w[h] * ct[h] + kv[h]
        n_ref[h] = jnp.broadcast_to(
            carry_w[h] * nrow[h] + jnp.sum(ws[h] * k[h].astype(F32), axis=0, keepdims=True), n_ref.shape[1:])
        m_ref[h] = jnp.broadcast_to(m_new[h], m_ref.shape[1:])

    ms = [jnp.mean(hh[h] * hh[h], axis=-1, keepdims=True) for h in H]
    for h in H:
        hm = hh[h] * lax.rsqrt(ms[h] + RMS_EPS) * hn_ref[:, h * dv:(h + 1) * dv]
        og = o_ref[0, :, h * dv:(h + 1) * dv].astype(F32)
        out_ref[0, :, h * dv:(h + 1) * dv] = (_sigmoid(og) * hm).astype(out_ref.dtype)


def _mlstm(main3, small3, gate_bias, head_norm, cast_arrays=()):
    b, s, _ = main3.shape
    L = min(MLSTM_CHUNK, s)
    assert L == MLSTM_CHUNK and s % L == 0
    qw, vw = MLSTM_QK_WIDTH, MLSTM_WIDTH
    return _call_with_casts(
        _mlstm_kernel, n_in=7, cast_arrays=cast_arrays,
        out_shape=[jax.ShapeDtypeStruct((b, s, MLSTM_WIDTH), BF16)],
        grid=(b, s // L),
        in_specs=[pl.BlockSpec((1, L, qw), lambda i, c: (i, c, 0)),
                  pl.BlockSpec((1, L, qw), lambda i, c: (i, c, 1)),
                  pl.BlockSpec((1, L, vw), lambda i, c: (i, c, 1)),
                  pl.BlockSpec((1, L, vw), lambda i, c: (i, c, 2)),
                  pl.BlockSpec((1, L, IF_W), lambda i, c: (i, c, COL_IF // IF_W)),
                  pl.BlockSpec((1, IF_W), lambda i, c: (0, 0)),
                  pl.BlockSpec((1, vw), lambda i, c: (0, 0))],
        out_specs=[pl.BlockSpec((1, L, vw), lambda i, c: (i, c, 0))],
        scratch_shapes=[pltpu.VMEM((MLSTM_HEADS, MLSTM_QK_DIM, MLSTM_V_DIM), F32),
                        pltpu.VMEM((MLSTM_HEADS, SUBLANES, MLSTM_QK_DIM), F32),
                        pltpu.VMEM((MLSTM_HEADS, SUBLANES, LANES), F32)],
        compiler_params=_params(("parallel", "arbitrary")),
        name="mlstm",
    )(main3, main3, main3, main3, small3, gate_bias, head_norm)


PV_MU_R, PV_MU_K, PV_MU_V, PV_W0, PV_A0, PV_KK, PV_KA, PV_RK, PV_LNG, PV_LNB = range(10)
PV_ROWS = 16


def _rwkv_kernel(r_ref, k_ref, v_ref, g_ref, wa_ref, pv_ref, mug_ref, muwa_ref, wup_ref, aup_ref, gup_ref, seg_ref,
                 out_ref, s_ref, pr_ref, pk_ref, pvv_ref, pg_ref, pwa_ref):
    L = RWKV_CHUNK
    CH = RWKV_CHUNKS_PER_STEP
    R = CH * L
    N = RWKV_HEAD
    GW = RWKV_GROUP_W
    GH = GW // N

    @pl.when(pl.program_id(1) == 0)
    def _():
        s_ref[...] = jnp.zeros_like(s_ref)
        pr_ref[...] = jnp.zeros_like(pr_ref)
        pk_ref[...] = jnp.zeros_like(pk_ref)
        pvv_ref[...] = jnp.zeros_like(pvv_ref)
        pg_ref[...] = jnp.zeros_like(pg_ref)
        pwa_ref[...] = jnp.zeros_like(pwa_ref)

    groups = range(RWKV_GROUPS)
    rows = lambda c: slice(c * L, (c + 1) * L)
    lanes = lambda g: slice(g * GW, (g + 1) * GW)
    cat = lambda u, w_: jnp.concatenate([u, w_], axis=0)
    seg = seg_ref[...]
    pv = pv_ref[...]

    def shift_lerp(x_ref, prev_ref, mu, ls=slice(None)):
        x = x_ref[0, :, ls].astype(F32)
        rid = lax.broadcasted_iota(jnp.int32, x.shape, 0)
        xs = jnp.where(rid == 0, prev_ref[0:1, ls], pltpu.roll(x, 1, 0))
        prev_ref[0:1, ls] = x[R - 1:R, :]
        return x + (xs - x) * mu

    gl = shift_lerp(g_ref, pg_ref, mug_ref[...])
    wa = shift_lerp(wa_ref, pwa_ref, muwa_ref[...])
    wl_b = _bf(jnp.tanh(wa[:, 0:RWKV_DECAY_RANK]))
    al_b = _bf(wa[:, RWKV_DECAY_RANK:WA_W])
    sgl_b = _bf(_sigmoid(gl))
    tr = lax.broadcasted_iota(jnp.int32, (R, R), 0)
    tc = lax.broadcasted_iota(jnp.int32, (R, R), 1)
    tri = ((tr >= tc) & ((tr // L) == (tc // L))).astype(F32)

    ti = lax.broadcasted_iota(jnp.int32, (L, GW), 0)
    lane = lax.broadcasted_iota(jnp.int32, (L, GW), 1)
    si = lane & (N - 1)
    lane_head = lane // N
    incl = ti >= si
    strict = ti > si
    eye = (ti == si).astype(F32)
    lvl_masks = []
    bsz = 1
    while bsz < L:
        same = (ti // (2 * bsz)) == (si // (2 * bsz))
        lvl_masks.append(jnp.where(same & ((ti & bsz) != 0) & ((si & bsz) == 0), 1.0, 0.0))
        bsz *= 2

    def bd(x):
        return jnp.concatenate([_bf(x)] * GH, axis=0) * seg

    def fold(z):
        acc = z[0:N]
        for hh in range(1, GH):
            acc = jnp.where(lane_head == hh, z[hh * N:(hh + 1) * N], acc)
        return acc

    def mmb(x, b):
        return jnp.dot(_bf(x), b, preferred_element_type=F32)

    def mmb_nt(x, b):
        return lax.dot_general(_bf(x), b, (((1,), (1,)), ((), ())), preferred_element_type=F32)

    prep = []
    for g in groups:
        ls = lanes(g)
        row = lambda i, ls=ls: pv[i:i + 1, ls]
        r = shift_lerp(r_ref, pr_ref, row(PV_MU_R), ls)
        kr = shift_lerp(k_ref, pk_ref, row(PV_MU_K), ls)
        v = shift_lerp(v_ref, pvv_ref, row(PV_MU_V), ls)
        w_log = -_softplus(-(row(PV_W0) + jnp.dot(wl_b, wup_ref[:, ls], preferred_element_type=F32))) - 0.5
        lw = -jnp.exp(w_log)
        a = _sigmoid(row(PV_A0) + jnp.dot(al_b, aup_ref[:, ls], preferred_element_type=F32))
        gate = jnp.dot(sgl_b, gup_ref[:, ls], preferred_element_type=F32)
        cs = _mm_exact_lhs(tri, lw)
        g_incl = jnp.exp(cs)
        g_inv = jnp.exp(-cs)
        kk0 = kr * row(PV_KK)
        kr2 = kr * (1.0 + (a - 1.0) * row(PV_KA))
        sums = jnp.dot(_bf(cat(kk0 * kk0, r * kr2 * row(PV_RK))), seg, preferred_element_type=F32)
        kk = kk0 * lax.rsqrt(jnp.maximum(sums[:R], 1e-24))
        prep.append(dict(p=jnp.exp(cs - lw) * kk, qt=kk * a * g_inv, kt=kr2 * g_inv, rt=r * g_incl, v=v,
                         bonus_v=sums[R:] * v, gate=gate, g_incl=g_incl,
                         ln_g=row(PV_LNG), ln_b=row(PV_LNB)))

    E = [(c, g) for c in range(CH) for g in groups]
    n_e = range(len(E))
    blk = lambda name: [prep[g][name][rows(c), :] for c, g in E]
    p, qt, kt, rt, vv, bonus_v, gate = (blk(n_) for n_ in ("p", "qt", "kt", "rt", "v", "bonus_v", "gate"))
    gl_ = [prep[g]["g_incl"][c * L + L - 1:c * L + L, :] for c, g in E]
    qg = [qt[e] * gl_[e] for e in n_e]
    kg = [kt[e] * gl_[e] for e in n_e]

    lhs_pr = [cat(p[e], rt[e]) for e in n_e]
    gq = [mmb_nt(lhs_pr[e], bd(qt[e])) for e in n_e]
    gk = [mmb_nt(lhs_pr[e], bd(kt[e])) for e in n_e]
    n_pq = [jnp.where(strict, gq[e][:L], 0.0) for e in n_e]
    a_rq = [jnp.where(incl, gq[e][L:], 0.0) for e in n_e]
    a_pk = [jnp.where(strict, gk[e][:L], 0.0) for e in n_e]
    a_rk = [jnp.where(incl, gk[e][L:], 0.0) for e in n_e]

    x = [eye - n_pq[e] * lvl_masks[0] for e in n_e]
    bd_n = [bd(n_pq[e]) for e in n_e]
    for msk in lvl_masks[1:]:
        t1 = [mmb(x[e], bd_n[e]) for e in n_e]
        x = [x[e] - msk * mmb(t1[e], bd(x[e])) for e in n_e]

    av = [mmb(cat(a_pk[e], a_rk[e]), bd(vv[e])) for e in n_e]
    w = [mmb(x[e], bd(p[e])) for e in n_e]
    u0 = [mmb(x[e], bd(av[e][:L])) for e in n_e]
    gmat = [rt[e] - mmb(a_rq[e], bd(w[e])) for e in n_e]
    y0 = [av[e][L:] - mmb(a_rq[e], bd(u0[e])) for e in n_e]
    mt = [eye * gl_[e] - fold(_mm_tn(qg[e], w[e])) for e in n_e]
    bt = [fold(_mm_tn(cat(kg[e], -qg[e]), cat(vv[e], u0[e]))) for e in n_e]

    state = [s_ref[g] for g in groups]
    y = [None] * len(E)
    for c in range(CH):
        es = [c * RWKV_GROUPS + g for g in groups]
        ys = [mmb(cat(gmat[e], mt[e]), bd(state[g])) for g, e in zip(groups, es)]
        for g, e in zip(groups, es):
            y[e] = ys[g][:L] + y0[e]
            state[g] = ys[g][L:] + bt[e]
    for g in groups:
        s_ref[g] = state[g]

    inv_n = 1.0 / N
    sums1 = [mmb(cat(y[e], y[e] * y[e]), seg) for e in n_e]
    for e, (c, g) in enumerate(E):
        mean = sums1[e][:L] * inv_n
        var = sums1[e][L:] * inv_n - mean * mean
        yn = (y[e] - mean) * lax.rsqrt(var + RWKV_GN_EPS) * prep[g]["ln_g"] + prep[g]["ln_b"]
        out_ref[0, rows(c), lanes(g)] = ((yn + bonus_v[e]) * gate[e]).astype(out_ref.dtype)


def _rwkv(main3, small3, pvec, mu_g, mu_wa, w_up, a_up, g_up, cast_arrays=()):
    b, s, _ = main3.shape
    L = RWKV_CHUNK * RWKV_CHUNKS_PER_STEP
    rw = RWKV_WIDTH
    head_of = jnp.arange(RWKV_GROUP_W) // RWKV_HEAD
    seg = (head_of[:, None] == head_of[None, :]).astype(BF16)
    const = lambda shape: pl.BlockSpec(shape, lambda i, c: (0, 0))
    return _call_with_casts(
        _rwkv_kernel, n_in=12, cast_arrays=cast_arrays,
        out_shape=[jax.ShapeDtypeStruct((b, s, rw), BF16)],
        grid=(b, s // L),
        in_specs=[pl.BlockSpec((1, L, rw), lambda i, c: (i, c, 0)),
                  pl.BlockSpec((1, L, rw), lambda i, c: (i, c, 1)),
                  pl.BlockSpec((1, L, rw), lambda i, c: (i, c, 2)),
                  pl.BlockSpec((1, L, RG_PAD), lambda i, c: (i, c, COL_RG // RG_PAD)),
                  pl.BlockSpec((1, L, WA_W), lambda i, c: (i, c, COL_WA // WA_W)),
                  const((PV_ROWS, rw)), const((1, RG_PAD)), const((1, WA_W)),
                  const((RWKV_DECAY_RANK, rw)), const((RWKV_A_RANK, rw)), const((RG_PAD, rw)),
                  const((RWKV_GROUP_W, RWKV_GROUP_W))],
        out_specs=[pl.BlockSpec((1, L, rw), lambda i, c: (i, c, 0))],
        scratch_shapes=[pltpu.VMEM((RWKV_GROUPS, RWKV_HEAD, RWKV_GROUP_W), F32),
                        pltpu.VMEM((SUBLANES, rw), F32), pltpu.VMEM((SUBLANES, rw), F32),
                        pltpu.VMEM((SUBLANES, rw), F32), pltpu.VMEM((SUBLANES, RG_PAD), F32),
                        pltpu.VMEM((SUBLANES, WA_W), F32)],
        compiler_params=_params(("parallel", "arbitrary")),
        name="rwkv7",
    )(main3, main3, main3, small3, small3, pvec, mu_g, mu_wa, w_up, a_up, g_up, seg)


def _mix_out_kernel(ha_ref, hb_ref, ga_ref, gb_ref, wa_ref, wb_ref, wo_ref, g_ref, x_ref, o_ref):
    ya = jnp.dot(ha_ref[...], wa_ref[...], preferred_element_type=F32)
    yb = jnp.dot(hb_ref[...], wb_ref[...], preferred_element_type=F32)
    merged = _bf(_sigmoid(ga_ref[...].astype(F32)) * ya + _sigmoid(gb_ref[...].astype(F32)) * yb)
    y = jnp.dot(merged, wo_ref[...], preferred_element_type=F32)
    _store_residual_norm(o_ref, x_ref, y, g_ref[...])


def _mix_out(ha, hb, gates, wa, wb, wo, g, x, *, tm):
    t, kdim = ha.shape
    d = wo.shape[1]
    resident = lambda shape: pl.BlockSpec(shape, lambda i: (0, 0), pipeline_mode=pl.Buffered(1))
    return pl.pallas_call(
        _mix_out_kernel,
        out_shape=jax.ShapeDtypeStruct((t, d), F32),
        grid=(t // tm,),
        in_specs=[pl.BlockSpec((tm, kdim), lambda i: (i, 0)),
                  pl.BlockSpec((tm, kdim), lambda i: (i, 0)),
                  pl.BlockSpec((tm, d), lambda i: (i, 0)),
                  pl.BlockSpec((tm, d), lambda i: (i, 1)),
                  resident((kdim, d)), resident((kdim, d)), resident((d, d)),
                  pl.BlockSpec((1, d), lambda i: (0, 0)),
                  pl.BlockSpec((tm, d), lambda i: (i, 0))],
        out_specs=pl.BlockSpec((tm, d), lambda i: (i, 0)),
        compiler_params=_params(("parallel",)),
        name="mix_out",
    )(ha, hb, gates, gates, wa, wb, wo, g, x)


def _xattn_kernel(x_ref, gpre_ref, wq_ref, kv_ref, wo_ref, gpost_ref, o_ref):
    h = jnp.concatenate([_bf(_rms(x_ref[r0:r0 + NORM_ROWS, :], gpre_ref[...]))
                         for r0 in range(0, x_ref.shape[0], NORM_ROWS)], axis=0)
    q = jnp.dot(h, wq_ref[...], preferred_element_type=F32)
    scale = XATTN_HEAD_DIM ** -0.5
    outs = []
    for hd in range(XATTN_HEADS):
        sl = slice(hd * XATTN_HEAD_DIM, (hd + 1) * XATTN_HEAD_DIM)
        k = kv_ref[0, :, sl]
        v = kv_ref[0, :, XATTN_WIDTH + hd * XATTN_HEAD_DIM:XATTN_WIDTH + (hd + 1) * XATTN_HEAD_DIM]
        sc = _mm_nt(q[:, sl], k) * scale
        sc = sc - jnp.max(sc, axis=-1, keepdims=True)
        e = jnp.exp(sc)
        p = e / jnp.sum(e, axis=-1, keepdims=True)
        outs.append(_mm(p, v))
    o = jnp.concatenate(outs, axis=-1)
    y = jnp.dot(_bf(o), wo_ref[...], preferred_element_type=F32)
    _store_residual_norm(o_ref, x_ref, y, gpost_ref[...])


def _xattn(x, gpre, wq, kv3, wo, gpost, *, tm, seq):
    t, d = x.shape
    per_seq = seq // tm
    return pl.pallas_call(
        _xattn_kernel,
        out_shape=jax.ShapeDtypeStruct((t, d), F32),
        grid=(t // tm,),
        in_specs=[pl.BlockSpec((tm, d), lambda i: (i, 0)),
                  pl.BlockSpec((1, d), lambda i: (0, 0)),
                  pl.BlockSpec((d, XATTN_WIDTH), lambda i: (0, 0)),
                  pl.BlockSpec((1, MEM_LEN, 2 * XATTN_WIDTH), lambda i: (i // per_seq, 0, 0)),
                  pl.BlockSpec((XATTN_WIDTH, d), lambda i: (0, 0)),
                  pl.BlockSpec((1, d), lambda i: (0, 0))],
        out_specs=pl.BlockSpec((tm, d), lambda i: (i, 0)),
        compiler_params=_params(("parallel",)),
        name="xattn",
    )(x, gpre, wq, kv3, wo, gpost)


FFN_TAIL = SUBLANES


def _gelu_tanh(x):
    return 0.5 * x * (1.0 + jnp.tanh(0.7978845608028654 * (x + 0.044715 * x * x * x)))


def _ffn_kernel(x_ref, gpre_ref, wg_ref, wu_ref, cwg_ref, cwu_ref, cbg_ref, cbu_ref, wd_ref, gpost_ref,
                o_ref, h_ref, tail_ref, *, tiles_per_seq):
    i = pl.program_id(0)
    j = pl.program_id(1)
    tm, tn = x_ref.shape[0], wg_ref.shape[1]

    @pl.when((i == 0) & (j == 0))
    def _():
        tail_ref[...] = jnp.zeros_like(tail_ref)

    @pl.when(j == 0)
    def _():
        for r0 in range(0, tm, NORM_ROWS):
            rs = slice(r0, r0 + NORM_ROWS)
            h_ref[rs, :] = _bf(_rms(x_ref[rs, :], gpre_ref[...]))
        o_ref[...] = jnp.zeros_like(o_ref)

    prev = jnp.where(i % tiles_per_seq == 0, 0.0, tail_ref[j])
    ug = jnp.dot(h_ref[...], wg_ref[...], preferred_element_type=F32)
    uu = jnp.dot(h_ref[...], wu_ref[...], preferred_element_type=F32)
    tail_ref[j] = jnp.concatenate([ug[tm - FFN_TAIL:, :], uu[tm - FFN_TAIL:, :]], axis=1)

    def conv(u, u_prev, cw_ref, cb_ref):
        u = jnp.concatenate([u_prev, u], axis=0)
        u1 = pltpu.roll(u, 1, 0)
        u2 = pltpu.roll(u, 2, 0)
        cw = cw_ref[...]
        full = cb_ref[...] + cw[2:3, :] * u + cw[1:2, :] * u1 + cw[0:1, :] * u2
        return full[FFN_TAIL:, :]

    gate = conv(ug, prev[:, :tn], cwg_ref, cbg_ref)
    up = conv(uu, prev[:, tn:], cwu_ref, cbu_ref)
    act = _bf(_gelu_tanh(gate) * up)
    o_ref[...] += jnp.dot(act, wd_ref[...], preferred_element_type=F32)

    @pl.when(j == pl.num_programs(1) - 1)
    def _():
        _store_residual_norm(o_ref, x_ref, o_ref, gpost_ref[...])


def _ffn(x, gpre, w_up, conv_w, conv_b, w_down, gpost, *, tm, tn, seq):
    t, d = x.shape
    nj = D_FF // tn
    return pl.pallas_call(
        functools.partial(_ffn_kernel, tiles_per_seq=seq // tm),
        out_shape=jax.ShapeDtypeStruct((t, d), F32),
        grid=(t // tm, nj),
        in_specs=[pl.BlockSpec((tm, d), lambda i, j: (i, 0)),
                  pl.BlockSpec((1, d), lambda i, j: (0, 0)),
                  pl.BlockSpec((d, tn), lambda i, j: (0, j)),
                  pl.BlockSpec((d, tn), lambda i, j: (0, nj + j)),
                  pl.BlockSpec((CONV_WIDTH, tn), lambda i, j: (0, j)),
                  pl.BlockSpec((CONV_WIDTH, tn), lambda i, j: (0, nj + j)),
                  pl.BlockSpec((1, tn), lambda i, j: (0, j)),
                  pl.BlockSpec((1, tn), lambda i, j: (0, nj + j)),
                  pl.BlockSpec((tn, d), lambda i, j: (j, 0)),
                  pl.BlockSpec((1, d), lambda i, j: (0, 0))],
        out_specs=pl.BlockSpec((tm, d), lambda i, j: (i, 0)),
        scratch_shapes=[pltpu.VMEM((tm, d), BF16), pltpu.VMEM((nj, FFN_TAIL, 2 * tn), F32)],
        compiler_params=_params(("arbitrary", "arbitrary")),
        name="conv_glu_ffn",
    )(x, gpre, w_up, w_up, conv_w, conv_w, conv_b, conv_b, w_down, gpost)


def _small_in_proj(w_in):
    d = w_in.shape[1]
    r0, g0 = MLSTM_TOTAL, MLSTM_TOTAL + RWKV_TOTAL
    mlstm_if = w_in[MLSTM_MAIN:r0]
    rwkv_wa = w_in[r0 + RWKV_MAIN:r0 + RWKV_MAIN + WA_W]
    rwkv_g = w_in[r0 + RWKV_MAIN + WA_W:g0]
    z = lambda n: jnp.zeros((n, d), w_in.dtype)
    return jnp.concatenate([rwkv_g, z(RG_PAD - RWKV_GATE_RANK), rwkv_wa, mlstm_if, z(IF_W - 2 * MLSTM_HEADS)], axis=0)


def _layer(x, mem, mix_pre_norm, w_in, mlstm_b_i, mlstm_b_f, mlstm_head_norm, rwkv_mu, rwkv_w0, rwkv_w_up, rwkv_a0,
           rwkv_a_up, rwkv_g_up, rwkv_k_k, rwkv_k_a, rwkv_r_k, rwkv_ln_g, rwkv_ln_b, w_branch_a, w_branch_b,
           w_mix_out, mix_post_norm, xattn_pre_norm, mem_norm, xattn_wq, xattn_wkv, xattn_wo, xattn_post_norm,
           ffn_pre_norm, ffn_w_up, ffn_conv_w, ffn_conv_b, ffn_w_down, ffn_post_norm):
    b, s, d = x.shape
    t = b * s
    x2d = x.reshape(t, d)
    row = lambda p: p.reshape(1, -1).astype(F32)
    tm_row = min(TM_ROW, s)

    w_in_t = w_in.T
    tm_in = min(TM_IN_PROJ, t)
    small, h = _norm_pass(x2d, row(mix_pre_norm), _small_in_proj(w_in_t), tm=min(TM_NORM_PASS, t),
                          name="in_proj_small")
    small3 = small.reshape(b, s, SMALL_COLS)
    mlstm_main, rwkv_main, gates = _rowranges_matmul(
        h, w_in_t, [(0, MLSTM_MAIN), (MLSTM_TOTAL, RWKV_MAIN), (MLSTM_TOTAL + RWKV_TOTAL, 2 * D_MODEL)],
        tm=tm_in, tn=TN_IN_PROJ, out_dtype=BF16, name="in_proj_wide")

    gate_bias = jnp.concatenate([mlstm_b_i, mlstm_b_f, jnp.zeros((IF_W - 2 * MLSTM_HEADS,), F32)]).reshape(1, IF_W)
    ha, ffn_w_down_b, w_a_b, w_b_b, w_out_b = _mlstm(
        mlstm_main.reshape(b, s, MLSTM_MAIN), small3, gate_bias, row(mlstm_head_norm),
        cast_arrays=(ffn_w_down, w_branch_a, w_branch_b, w_mix_out))

    mu_r, mu_k, mu_v = (rwkv_mu[i * RWKV_WIDTH:(i + 1) * RWKV_WIDTH] for i in range(3))
    mu_wa = rwkv_mu[RWKV_MAIN:RWKV_MAIN + WA_W]
    mu_g = jnp.pad(rwkv_mu[RWKV_MAIN + WA_W:], (0, RG_PAD - RWKV_GATE_RANK))
    pvec = jnp.stack([mu_r, mu_k, mu_v, rwkv_w0, rwkv_a0, rwkv_k_k, rwkv_k_a, rwkv_r_k.reshape(-1), rwkv_ln_g,
                      rwkv_ln_b] + [jnp.zeros((RWKV_WIDTH,), F32)] * (PV_ROWS - 10))
    g_up = jnp.pad(rwkv_g_up, ((0, RG_PAD - RWKV_GATE_RANK), (0, 0)))
    hb, ffn_w_up_b = _rwkv(rwkv_main.reshape(b, s, RWKV_MAIN), small3, pvec, row(mu_g), row(mu_wa), _bf(rwkv_w_up),
                           _bf(rwkv_a_up), _bf(g_up), cast_arrays=(ffn_w_up,))

    x1 = _mix_out(ha.reshape(t, -1), hb.reshape(t, -1), gates, w_a_b, w_b_b, w_out_b, row(mix_post_norm), x2d,
                  tm=tm_row)

    mem2d = mem.reshape(b * MEM_LEN, d)
    kv, _ = _norm_matmul(mem2d, row(mem_norm), _bf(xattn_wkv), tm=MEM_LEN, tn=TN, out_dtype=F32, name="mem_kv")
    x2 = _xattn(x1, row(xattn_pre_norm), _bf(xattn_wq), kv.reshape(b, MEM_LEN, 2 * XATTN_WIDTH), _bf(xattn_wo),
                row(xattn_post_norm), tm=min(TM_XATTN, s), seq=s)

    x3 = _ffn(x2, row(ffn_pre_norm), ffn_w_up_b, ffn_conv_w, row(ffn_conv_b), ffn_w_down_b, row(ffn_post_norm),
              tm=min(TM_FFN, s), tn=TN, seq=s)
    return x3.reshape(b, s, d)


def kernel(x, mem, mix_pre_norm, w_in, mlstm_b_i, mlstm_b_f, mlstm_head_norm, rwkv_mu, rwkv_w0, rwkv_w_up, rwkv_a0, rwkv_a_up, rwkv_g_up, rwkv_k_k, rwkv_k_a, rwkv_r_k, rwkv_ln_g, rwkv_ln_b, w_branch_a, w_branch_b, w_mix_out, mix_post_norm, xattn_pre_norm, mem_norm, xattn_wq, xattn_wkv, xattn_wo, xattn_post_norm, ffn_pre_norm, ffn_w_up, ffn_conv_w, ffn_conv_b, ffn_w_down, ffn_post_norm):
    for l in range(mix_pre_norm.shape[0]):
        x = _layer(x, mem, mix_pre_norm[l], w_in[l], mlstm_b_i[l], mlstm_b_f[l], mlstm_head_norm[l], rwkv_mu[l],
                   rwkv_w0[l], rwkv_w_up[l], rwkv_a0[l], rwkv_a_up[l], rwkv_g_up[l], rwkv_k_k[l], rwkv_k_a[l],
                   rwkv_r_k[l], rwkv_ln_g[l], rwkv_ln_b[l], w_branch_a[l], w_branch_b[l], w_mix_out[l],
                   mix_post_norm[l], xattn_pre_norm[l], mem_norm[l], xattn_wq[l], xattn_wkv[l], xattn_wo[l],
                   xattn_post_norm[l], ffn_pre_norm[l], ffn_w_up[l], ffn_conv_w[l], ffn_conv_b[l], ffn_w_down[l],
                   ffn_post_norm[l])
    return x
```

```python
import functools

import jax
import jax.numpy as jnp
from jax import lax
from jax.experimental import pallas as pl
from jax.experimental.pallas import tpu as pltpu

D_MODEL = 2048
MEM_LEN = 256
RMS_EPS = 1e-6

MLSTM_HEADS = 4
MLSTM_WIDTH = D_MODEL // 2
MLSTM_V_DIM = MLSTM_WIDTH // MLSTM_HEADS
MLSTM_QK_DIM = MLSTM_V_DIM // 2
MLSTM_QK_WIDTH = MLSTM_HEADS * MLSTM_QK_DIM
GATE_SOFTCAP = 15.0

RWKV_WIDTH = D_MODEL // 2
RWKV_HEAD = 64
RWKV_HEADS = RWKV_WIDTH // RWKV_HEAD
RWKV_DECAY_RANK = 64
RWKV_A_RANK = 64
RWKV_GATE_RANK = 160
RWKV_GN_EPS = 64e-5

MLSTM_MAIN = 2 * MLSTM_QK_WIDTH + 2 * MLSTM_WIDTH
MLSTM_TOTAL = MLSTM_MAIN + 2 * MLSTM_HEADS
RWKV_MAIN = 3 * RWKV_WIDTH
RWKV_TOTAL = RWKV_MAIN + RWKV_DECAY_RANK + RWKV_A_RANK + RWKV_GATE_RANK

XATTN_HEADS = 4
XATTN_HEAD_DIM = 128
XATTN_WIDTH = XATTN_HEADS * XATTN_HEAD_DIM

D_FF = 4 * D_MODEL
CONV_WIDTH = 3

RG_PAD = 256
WA_W = RWKV_DECAY_RANK + RWKV_A_RANK
IF_W = 128
COL_RG = 0
COL_WA = COL_RG + RG_PAD
COL_IF = COL_WA + WA_W
SMALL_COLS = COL_IF + IF_W

MLSTM_CHUNK = 256
RWKV_CHUNK = 64
RWKV_CHUNKS_PER_STEP = 2
RWKV_GROUP_W = 256
RWKV_GROUPS = RWKV_WIDTH // RWKV_GROUP_W
VMEM_LIMIT = 56 * 1024 * 1024
VMEM_LIMIT_MIX_XATTN = 60 * 1024 * 1024

LANES = 128
SUBLANES = 8
BF16_SUBLANES = 16
TM_IN_PROJ = 1024
TN_IN_PROJ = 1024
TM_ROW = 512
TM_XATTN = 1024
TM_MIX_XATTN = 512
TM_FFN = 1024
TN = 512

F32 = jnp.float32
BF16 = jnp.bfloat16


def _bf(x):
    return x.astype(BF16)


def _mm(a, b):
    return jnp.dot(_bf(a), _bf(b), preferred_element_type=F32)


def _mm_nt(a, b):
    return lax.dot_general(_bf(a), _bf(b), (((1,), (1,)), ((), ())), preferred_element_type=F32)


def _mm_tn(a, b):
    return lax.dot_general(_bf(a), _bf(b), (((0,), (0,)), ((), ())), preferred_element_type=F32)


def _mm_exact_lhs(tri, x):
    hi = _bf(x)
    r1 = x - hi.astype(F32)
    mid = _bf(r1)
    lo = _bf(r1 - mid.astype(F32))
    t = _bf(tri)
    return (jnp.dot(t, hi, preferred_element_type=F32) + jnp.dot(t, mid, preferred_element_type=F32)
            + jnp.dot(t, lo, preferred_element_type=F32))


def _rms(x, g):
    return x * lax.rsqrt(jnp.mean(x * x, axis=-1, keepdims=True) + RMS_EPS) * g


NORM_ROWS = 64


def _store_residual_norm(o_ref, x_ref, y, g):
    for r0 in range(0, o_ref.shape[0], NORM_ROWS):
        rs = slice(r0, r0 + NORM_ROWS)
        o_ref[rs, :] = x_ref[rs, :] + _rms(y[rs, :], g)


def _softplus(x):
    return jnp.maximum(x, 0.0) + jnp.log(1.0 + jnp.exp(-jnp.abs(x)))


def _sigmoid(x):
    return 1.0 / (1.0 + jnp.exp(-x))


def _params(sem):
    return pltpu.CompilerParams(dimension_semantics=sem, vmem_limit_bytes=VMEM_LIMIT)


def _call_with_casts(body, *, n_in, out_shape, grid, in_specs, out_specs, cast_arrays=(), **kwargs):
    n_out = len(out_shape)
    k = len(cast_arrays)
    nsteps = 1
    for extent in grid:
        nsteps *= extent

    def step(*ids):
        lin = ids[0]
        for extent, idx in zip(grid[1:], ids[1:]):
            lin = lin * extent + idx
        return lin

    band_specs, band_shapes = [], []
    for arr in cast_arrays:
        rows, cols = arr.shape
        assert rows % nsteps == 0 and (rows // nsteps) % BF16_SUBLANES == 0, (arr.shape, nsteps)
        band_specs.append(pl.BlockSpec((rows // nsteps, cols), lambda *ids: (step(*ids), 0)))
        band_shapes.append(jax.ShapeDtypeStruct(arr.shape, BF16))

    def kernel(*refs):
        ins, src = refs[:n_in], refs[n_in:n_in + k]
        outs, dst = refs[n_in + k:n_in + k + n_out], refs[n_in + k + n_out:n_in + 2 * k + n_out]
        for s_ref, d_ref in zip(src, dst):
            d_ref[...] = s_ref[...].astype(d_ref.dtype)
        body(*ins, *outs, *refs[n_in + 2 * k + n_out:])

    def run(*operands):
        assert len(operands) == n_in
        return pl.pallas_call(kernel, out_shape=tuple(out_shape) + tuple(band_shapes), grid=grid,
                              in_specs=list(in_specs) + band_specs, out_specs=tuple(out_specs) + tuple(band_specs),
                              **kwargs)(*operands, *cast_arrays)

    return run


def _norm_matmul_kernel(x_ref, g_ref, w_ref, o_ref, h_ref, *, w_is_transposed):
    @pl.when(pl.program_id(1) == 0)
    def _():
        h_ref[...] = _bf(_rms(x_ref[...], g_ref[...]))

    w = w_ref[...].T if w_is_transposed else w_ref[...]
    o_ref[...] = jnp.dot(h_ref[...], _bf(w), preferred_element_type=F32).astype(o_ref.dtype)


def _norm_matmul(x, g, w, *, tm, tn, out_dtype, name, w_is_transposed=False):
    t, d = x.shape
    n = w.shape[0] if w_is_transposed else w.shape[1]
    w_spec = pl.BlockSpec((tn, d), lambda i, j: (j, 0)) if w_is_transposed else pl.BlockSpec((d, tn), lambda i, j: (0, j))
    return pl.pallas_call(
        functools.partial(_norm_matmul_kernel, w_is_transposed=w_is_transposed),
        out_shape=(jax.ShapeDtypeStruct((t, n), out_dtype), jax.ShapeDtypeStruct((t, d), BF16)),
        grid=(t // tm, n // tn),
        in_specs=[pl.BlockSpec((tm, d), lambda i, j: (i, 0)),
                  pl.BlockSpec((1, d), lambda i, j: (0, 0)),
                  w_spec],
        out_specs=(pl.BlockSpec((tm, tn), lambda i, j: (i, j)),
                   pl.BlockSpec((tm, d), lambda i, j: (i, 0))),
        compiler_params=_params(("parallel", "arbitrary")),
        name=name,
    )(x, g, w)


def _rowranges_matmul_kernel(a_ref, wt_ref, *refs, tile_ranges):
    o_refs, wb_ref = refs[:-1], refs[-1]
    j = pl.program_id(0)

    @pl.when(pl.program_id(1) == 0)
    def _():
        wb_ref[...] = _bf(wt_ref[...].T)

    for o_ref, (first, stop) in zip(o_refs, tile_ranges):
        @pl.when((j >= first) & (j < stop))
        def _(o_ref=o_ref):
            o_ref[...] = jnp.dot(a_ref[...], wb_ref[...], preferred_element_type=F32).astype(o_ref.dtype)


def _rowranges_matmul(a, w_t, ranges, *, tm, tn, out_dtype, name):
    t, d = a.shape
    n_i = t // tm
    tile_ranges, first = [], 0
    for row0, nrows in ranges:
        assert row0 % SUBLANES == 0 and nrows % tn == 0
        tile_ranges.append((first, first + nrows // tn))
        first += nrows // tn

    def weight_row(j, i):
        row = 0
        for (row0, _), (lo, hi) in zip(ranges, tile_ranges):
            row = row + jnp.where((j >= lo) & (j < hi), row0 + tn * (j - lo), 0)
        return pl.multiple_of(row, SUBLANES), 0

    def out_index(lo, hi):
        return lambda j, i: (jnp.where(j < lo, 0, jnp.where(j >= hi, n_i - 1, i)), jnp.clip(j - lo, 0, hi - lo - 1))

    return pl.pallas_call(
        functools.partial(_rowranges_matmul_kernel, tile_ranges=tuple(tile_ranges)),
        out_shape=[jax.ShapeDtypeStruct((t, nrows), out_dtype) for _, nrows in ranges],
        grid=(first, n_i),
        in_specs=[pl.BlockSpec((tm, d), lambda j, i: (i, 0)),
                  pl.BlockSpec((pl.Element(tn), pl.Element(d)), weight_row)],
        out_specs=[pl.BlockSpec((tm, tn), out_index(lo, hi)) for lo, hi in tile_ranges],
        scratch_shapes=[pltpu.VMEM((d, tn), BF16)],
        compiler_params=_params(("arbitrary", "arbitrary")),
        name=name,
    )(a, w_t)


def _mlstm_kernel(q_ref, k_ref, v_ref, o_ref, gate_ref, bias_ref, hn_ref, out_ref, ct_ref, n_ref, m_ref):
    L = MLSTM_CHUNK
    dk, dv = MLSTM_QK_DIM, MLSTM_V_DIM

    @pl.when(pl.program_id(1) == 0)
    def _():
        ct_ref[...] = jnp.zeros_like(ct_ref)
        n_ref[...] = jnp.zeros_like(n_ref)
        m_ref[...] = jnp.zeros_like(m_ref)

    pre = gate_ref[0] + bias_ref[...]
    capped = GATE_SOFTCAP * jnp.tanh(pre / GATE_SOFTCAP)
    logf = -_softplus(-capped)
    row = lax.broadcasted_iota(jnp.int32, (L, L), 0)
    col = lax.broadcasted_iota(jnp.int32, (L, L), 1)
    causal = row >= col
    bcum = _mm_exact_lhs(causal.astype(F32), logf)
    ig_t = capped.T
    bcum_t = bcum.T
    scale = MLSTM_QK_DIM ** -0.5

    H = range(MLSTM_HEADS)
    q = [q_ref[0, :, h * dk:(h + 1) * dk] for h in H]
    k = [k_ref[0, :, h * dk:(h + 1) * dk] for h in H]
    v = [v_ref[0, :, h * dv:(h + 1) * dv] for h in H]
    b_c = [bcum[:, MLSTM_HEADS + h:MLSTM_HEADS + h + 1] for h in H]
    b_r = [bcum_t[MLSTM_HEADS + h:MLSTM_HEADS + h + 1, :] for h in H]
    i_c = [capped[:, h:h + 1] for h in H]
    i_r = [ig_t[h:h + 1, :] for h in H]
    m_prev = [m_ref[h][0:1, 0:1] for h in H]
    ct = [ct_ref[h] for h in H]
    nrow = [n_ref[h][0:1, :] for h in H]

    dmat = [jnp.where(causal, b_c[h] - b_r[h] + i_r[h], -jnp.inf) for h in H]
    inter = [b_c[h] + m_prev[h] for h in H]
    m_t = [jnp.maximum(inter[h], jnp.max(dmat[h], axis=-1, keepdims=True)) for h in H]
    dexp = [jnp.exp(dmat[h] - m_t[h]) * scale for h in H]
    w_inter = [jnp.exp(inter[h] - m_t[h]) for h in H]
    qk = [_mm_nt(q[h], k[h]) for h in H]
    qc = [_mm(q[h], ct[h]) for h in H]
    s = [qk[h] * dexp[h] for h in H]
    sv = [_mm(s[h], v[h]) for h in H]
    num = [w_inter[h] * qc[h] + sv[h] for h in H]
    den = [w_inter[h] * jnp.sum(q[h].astype(F32) * nrow[h], axis=-1, keepdims=True)
           + jnp.sum(s[h], axis=-1, keepdims=True) for h in H]
    hh = [num[h] / jnp.maximum(jnp.abs(den[h]), jnp.exp(-m_t[h])) for h in H]

    b_last = [b_c[h][L - 1:L, :] for h in H]
    gs = [b_last[h] - b_c[h] + i_c[h] for h in H]
    m_new = [jnp.maximum(b_last[h] + m_prev[h], jnp.max(gs[h], axis=0, keepdims=True)) for h in H]
    carry_w = [jnp.exp(b_last[h] + m_prev[h] - m_new[h]) for h in H]
    ws = [jnp.exp(gs[h] - m_new[h]) * scale for h in H]
    kv = [_mm_tn(k[h], ws[h] * v[h].astype(F32)) for h in H]
    for h in H:
        ct_ref[h] = carry_w[h] * ct[h] + kv[h]
        n_ref[h] = jnp.broadcast_to(
            carry_w[h] * nrow[h] + jnp.sum(ws[h] * k[h].astype(F32), axis=0, keepdims=True), n_ref.shape[1:])
        m_ref[h] = jnp.broadcast_to(m_new[h], m_ref.shape[1:])

    ms = [jnp.mean(hh[h] * hh[h], axis=-1, keepdims=True) for h in H]
    for h in H:
        hm = hh[h] * lax.rsqrt(ms[h] + RMS_EPS) * hn_ref[:, h * dv:(h + 1) * dv]
        og = o_ref[0, :, h * dv:(h + 1) * dv].astype(F32)
        out_ref[0, :, h * dv:(h + 1) * dv] = (_sigmoid(og) * hm).astype(out_ref.dtype)


def _mlstm(main3, small3, gate_bias, head_norm, cast_arrays=()):
    b, s, _ = main3.shape
    L = min(MLSTM_CHUNK, s)
    assert L == MLSTM_CHUNK and s % L == 0
    qw, vw = MLSTM_QK_WIDTH, MLSTM_WIDTH
    return _call_with_casts(
        _mlstm_kernel, n_in=7, cast_arrays=cast_arrays,
        out_shape=[jax.ShapeDtypeStruct((b, s, MLSTM_WIDTH), BF16)],
        grid=(b, s // L),
        in_specs=[pl.BlockSpec((1, L, qw), lambda i, c: (i, c, 0)),
                  pl.BlockSpec((1, L, qw), lambda i, c: (i, c, 1)),
                  pl.BlockSpec((1, L, vw), lambda i, c: (i, c, 1)),
                  pl.BlockSpec((1, L, vw), lambda i, c: (i, c, 2)),
                  pl.BlockSpec((1, L, IF_W), lambda i, c: (i, c, COL_IF // IF_W)),
                  pl.BlockSpec((1, IF_W), lambda i, c: (0, 0)),
                  pl.BlockSpec((1, vw), lambda i, c: (0, 0))],
        out_specs=[pl.BlockSpec((1, L, vw), lambda i, c: (i, c, 0))],
        scratch_shapes=[pltpu.VMEM((MLSTM_HEADS, MLSTM_QK_DIM, MLSTM_V_DIM), F32),
                        pltpu.VMEM((MLSTM_HEADS, SUBLANES, MLSTM_QK_DIM), F32),
                        pltpu.VMEM((MLSTM_HEADS, SUBLANES, LANES), F32)],
        compiler_params=_params(("parallel", "arbitrary")),
        name="mlstm",
    )(main3, main3, main3, main3, small3, gate_bias, head_norm)


PV_MU_R, PV_MU_K, PV_MU_V, PV_W0, PV_A0, PV_KK, PV_KA, PV_RK, PV_LNG, PV_LNB = range(10)
PV_ROWS = 16


def _rwkv_kernel(r_ref, k_ref, v_ref, g_ref, wa_ref, pv_ref, mug_ref, muwa_ref, wup_ref, aup_ref, gup_ref, seg_ref,
                 out_ref, s_ref, pr_ref, pk_ref, pvv_ref, pg_ref, pwa_ref):
    L = RWKV_CHUNK
    CH = RWKV_CHUNKS_PER_STEP
    R = CH * L
    N = RWKV_HEAD
    GW = RWKV_GROUP_W
    GH = GW // N

    @pl.when(pl.program_id(1) == 0)
    def _():
        s_ref[...] = jnp.zeros_like(s_ref)
        pr_ref[...] = jnp.zeros_like(pr_ref)
        pk_ref[...] = jnp.zeros_like(pk_ref)
        pvv_ref[...] = jnp.zeros_like(pvv_ref)
        pg_ref[...] = jnp.zeros_like(pg_ref)
        pwa_ref[...] = jnp.zeros_like(pwa_ref)

    groups = range(RWKV_GROUPS)
    rows = lambda c: slice(c * L, (c + 1) * L)
    lanes = lambda g: slice(g * GW, (g + 1) * GW)
    cat = lambda u, w_: jnp.concatenate([u, w_], axis=0)
    seg = seg_ref[...]
    pv = pv_ref[...]

    def shift_lerp(x_ref, prev_ref, mu, ls=slice(None)):
        x = x_ref[0, :, ls].astype(F32)
        rid = lax.broadcasted_iota(jnp.int32, x.shape, 0)
        xs = jnp.where(rid == 0, prev_ref[0:1, ls], pltpu.roll(x, 1, 0))
        prev_ref[0:1, ls] = x[R - 1:R, :]
        return x + (xs - x) * mu

    gl = shift_lerp(g_ref, pg_ref, mug_ref[...])
    wa = shift_lerp(wa_ref, pwa_ref, muwa_ref[...])
    wl_b = _bf(jnp.tanh(wa[:, 0:RWKV_DECAY_RANK]))
    al_b = _bf(wa[:, RWKV_DECAY_RANK:WA_W])
    sgl_b = _bf(_sigmoid(gl))
    tr = lax.broadcasted_iota(jnp.int32, (R, R), 0)
    tc = lax.broadcasted_iota(jnp.int32, (R, R), 1)
    tri = ((tr >= tc) & ((tr // L) == (tc // L))).astype(F32)

    ti = lax.broadcasted_iota(jnp.int32, (L, GW), 0)
    lane = lax.broadcasted_iota(jnp.int32, (L, GW), 1)
    si = lane & (N - 1)
    lane_head = lane // N
    incl = ti >= si
    strict = ti > si
    eye = (ti == si).astype(F32)
    lvl_masks = []
    bsz = 1
    while bsz < L:
        same = (ti // (2 * bsz)) == (si // (2 * bsz))
        lvl_masks.append(jnp.where(same & ((ti & bsz) != 0) & ((si & bsz) == 0), 1.0, 0.0))
        bsz *= 2

    def bd(x):
        return jnp.concatenate([_bf(x)] * GH, axis=0) * seg

    def fold(z):
        acc = z[0:N]
        for hh in range(1, GH):
            acc = jnp.where(lane_head == hh, z[hh * N:(hh + 1) * N], acc)
        return acc

    def mmb(x, b):
        return jnp.dot(_bf(x), b, preferred_element_type=F32)

    def mmb_nt(x, b):
        return lax.dot_general(_bf(x), b, (((1,), (1,)), ((), ())), preferred_element_type=F32)

    prep = []
    for g in groups:
        ls = lanes(g)
        row = lambda i, ls=ls: pv[i:i + 1, ls]
        r = shift_lerp(r_ref, pr_ref, row(PV_MU_R), ls)
        kr = shift_lerp(k_ref, pk_ref, row(PV_MU_K), ls)
        v = shift_lerp(v_ref, pvv_ref, row(PV_MU_V), ls)
        w_log = -_softplus(-(row(PV_W0) + jnp.dot(wl_b, wup_ref[:, ls], preferred_element_type=F32))) - 0.5
        lw = -jnp.exp(w_log)
        a = _sigmoid(row(PV_A0) + jnp.dot(al_b, aup_ref[:, ls], preferred_element_type=F32))
        gate = jnp.dot(sgl_b, gup_ref[:, ls], preferred_element_type=F32)
        cs = _mm_exact_lhs(tri, lw)
        g_incl = jnp.exp(cs)
        g_inv = jnp.exp(-cs)
        kk0 = kr * row(PV_KK)
        kr2 = kr * (1.0 + (a - 1.0) * row(PV_KA))
        sums = jnp.dot(_bf(cat(kk0 * kk0, r * kr2 * row(PV_RK))), seg, preferred_element_type=F32)
        kk = kk0 * lax.rsqrt(jnp.maximum(sums[:R], 1e-24))
        prep.append(dict(p=jnp.exp(cs - lw) * kk, qt=kk * a * g_inv, kt=kr2 * g_inv, rt=r * g_incl, v=v,
                         bonus_v=sums[R:] * v, gate=gate, g_incl=g_incl,
                         ln_g=row(PV_LNG), ln_b=row(PV_LNB)))

    E = [(c, g) for c in range(CH) for g in groups]
    n_e = range(len(E))
    blk = lambda name: [prep[g][name][rows(c), :] for c, g in E]
    p, qt, kt, rt, vv, bonus_v, gate = (blk(n_) for n_ in ("p", "qt", "kt", "rt", "v", "bonus_v", "gate"))
    gl_ = [prep[g]["g_incl"][c * L + L - 1:c * L + L, :] for c, g in E]
    qg = [qt[e] * gl_[e] for e in n_e]
    kg = [kt[e] * gl_[e] for e in n_e]

    lhs_pr = [cat(p[e], rt[e]) for e in n_e]
    gq = [mmb_nt(lhs_pr[e], bd(qt[e])) for e in n_e]
    gk = [mmb_nt(lhs_pr[e], bd(kt[e])) for e in n_e]
    n_pq = [jnp.where(strict, gq[e][:L], 0.0) for e in n_e]
    a_rq = [jnp.where(incl, gq[e][L:], 0.0) for e in n_e]
    a_pk = [jnp.where(strict, gk[e][:L], 0.0) for e in n_e]
    a_rk = [jnp.where(incl, gk[e][L:], 0.0) for e in n_e]

    x = [eye - n_pq[e] * lvl_masks[0] for e in n_e]
    bd_n = [bd(n_pq[e]) for e in n_e]
    for msk in lvl_masks[1:]:
        t1 = [mmb(x[e], bd_n[e]) for e in n_e]
        x = [x[e] - msk * mmb(t1[e], bd(x[e])) for e in n_e]

    av = [mmb(cat(a_pk[e], a_rk[e]), bd(vv[e])) for e in n_e]
    w = [mmb(x[e], bd(p[e])) for e in n_e]
    u0 = [mmb(x[e], bd(av[e][:L])) for e in n_e]
    gmat = [rt[e] - mmb(a_rq[e], bd(w[e])) for e in n_e]
    y0 = [av[e][L:] - mmb(a_rq[e], bd(u0[e])) for e in n_e]
    mt = [eye * gl_[e] - fold(_mm_tn(qg[e], w[e])) for e in n_e]
    bt = [fold(_mm_tn(cat(kg[e], -qg[e]), cat(vv[e], u0[e]))) for e in n_e]

    state = [s_ref[g] for g in groups]
    y = [None] * len(E)
    for c in range(CH):
        es = [c * RWKV_GROUPS + g for g in groups]
        ys = [mmb(cat(gmat[e], mt[e]), bd(state[g])) for g, e in zip(groups, es)]
        for g, e in zip(groups, es):
            y[e] = ys[g][:L] + y0[e]
            state[g] = ys[g][L:] + bt[e]
    for g in groups:
        s_ref[g] = state[g]

    inv_n = 1.0 / N
    sums1 = [mmb(cat(y[e], y[e] * y[e]), seg) for e in n_e]
    for e, (c, g) in enumerate(E):
        mean = sums1[e][:L] * inv_n
        var = sums1[e][L:] * inv_n - mean * mean
        yn = (y[e] - mean) * lax.rsqrt(var + RWKV_GN_EPS) * prep[g]["ln_g"] + prep[g]["ln_b"]
        out_ref[0, rows(c), lanes(g)] = ((yn + bonus_v[e]) * gate[e]).astype(out_ref.dtype)


def _rwkv(main3, small3, pvec, mu_g, mu_wa, w_up, a_up, g_up, cast_arrays=()):
    b, s, _ = main3.shape
    L = RWKV_CHUNK * RWKV_CHUNKS_PER_STEP
    rw = RWKV_WIDTH
    head_of = jnp.arange(RWKV_GROUP_W) // RWKV_HEAD
    seg = (head_of[:, None] == head_of[None, :]).astype(BF16)
    const = lambda shape: pl.BlockSpec(shape, lambda i, c: (0, 0))
    return _call_with_casts(
        _rwkv_kernel, n_in=12, cast_arrays=cast_arrays,
        out_shape=[jax.ShapeDtypeStruct((b, s, rw), BF16)],
        grid=(b, s // L),
        in_specs=[pl.BlockSpec((1, L, rw), lambda i, c: (i, c, 0)),
                  pl.BlockSpec((1, L, rw), lambda i, c: (i, c, 1)),
                  pl.BlockSpec((1, L, rw), lambda i, c: (i, c, 2)),
                  pl.BlockSpec((1, L, RG_PAD), lambda i, c: (i, c, COL_RG // RG_PAD)),
                  pl.BlockSpec((1, L, WA_W), lambda i, c: (i, c, COL_WA // WA_W)),
                  const((PV_ROWS, rw)), const((1, RG_PAD)), const((1, WA_W)),
                  const((RWKV_DECAY_RANK, rw)), const((RWKV_A_RANK, rw)), const((RG_PAD, rw)),
                  const((RWKV_GROUP_W, RWKV_GROUP_W))],
        out_specs=[pl.BlockSpec((1, L, rw), lambda i, c: (i, c, 0))],
        scratch_shapes=[pltpu.VMEM((RWKV_GROUPS, RWKV_HEAD, RWKV_GROUP_W), F32),
                        pltpu.VMEM((SUBLANES, rw), F32), pltpu.VMEM((SUBLANES, rw), F32),
                        pltpu.VMEM((SUBLANES, rw), F32), pltpu.VMEM((SUBLANES, RG_PAD), F32),
                        pltpu.VMEM((SUBLANES, WA_W), F32)],
        compiler_params=_params(("parallel", "arbitrary")),
        name="rwkv7",
    )(main3, main3, main3, small3, small3, pvec, mu_g, mu_wa, w_up, a_up, g_up, seg)


def _mix_out_kernel(ha_ref, hb_ref, ga_ref, gb_ref, wa_ref, wb_ref, wo_ref, g_ref, x_ref, o_ref):
    ya = jnp.dot(ha_ref[...], wa_ref[...], preferred_element_type=F32)
    yb = jnp.dot(hb_ref[...], wb_ref[...], preferred_element_type=F32)
    merged = _bf(_sigmoid(ga_ref[...].astype(F32)) * ya + _sigmoid(gb_ref[...].astype(F32)) * yb)
    y = jnp.dot(merged, wo_ref[...], preferred_element_type=F32)
    _store_residual_norm(o_ref, x_ref, y, g_ref[...])


def _mix_out(ha, hb, gates, wa, wb, wo, g, x, *, tm):
    t, kdim = ha.shape
    d = wo.shape[1]
    resident = lambda shape: pl.BlockSpec(shape, lambda i: (0, 0), pipeline_mode=pl.Buffered(1))
    return pl.pallas_call(
        _mix_out_kernel,
        out_shape=jax.ShapeDtypeStruct((t, d), F32),
        grid=(t // tm,),
        in_specs=[pl.BlockSpec((tm, kdim), lambda i: (i, 0)),
                  pl.BlockSpec((tm, kdim), lambda i: (i, 0)),
                  pl.BlockSpec((tm, d), lambda i: (i, 0)),
                  pl.BlockSpec((tm, d), lambda i: (i, 1)),
                  resident((kdim, d)), resident((kdim, d)), resident((d, d)),
                  pl.BlockSpec((1, d), lambda i: (0, 0)),
                  pl.BlockSpec((tm, d), lambda i: (i, 0))],
        out_specs=pl.BlockSpec((tm, d), lambda i: (i, 0)),
        compiler_params=_params(("parallel",)),
        name="mix_out",
    )(ha, hb, gates, gates, wa, wb, wo, g, x)


def _xattn_kernel(x_ref, gpre_ref, wq_ref, kv_ref, wo_ref, gpost_ref, o_ref):
    h = jnp.concatenate([_bf(_rms(x_ref[r0:r0 + NORM_ROWS, :], gpre_ref[...]))
                         for r0 in range(0, x_ref.shape[0], NORM_ROWS)], axis=0)
    q = jnp.dot(h, wq_ref[...], preferred_element_type=F32)
    scale = XATTN_HEAD_DIM ** -0.5
    outs = []
    for hd in range(XATTN_HEADS):
        sl = slice(hd * XATTN_HEAD_DIM, (hd + 1) * XATTN_HEAD_DIM)
        k = kv_ref[0, :, sl]
        v = kv_ref[0, :, XATTN_WIDTH + hd * XATTN_HEAD_DIM:XATTN_WIDTH + (hd + 1) * XATTN_HEAD_DIM]
        sc = _mm_nt(q[:, sl], k) * scale
        sc = sc - jnp.max(sc, axis=-1, keepdims=True)
        e = jnp.exp(sc)
        p = e / jnp.sum(e, axis=-1, keepdims=True)
        outs.append(_mm(p, v))
    o = jnp.concatenate(outs, axis=-1)
    y = jnp.dot(_bf(o), wo_ref[...], preferred_element_type=F32)
    _store_residual_norm(o_ref, x_ref, y, gpost_ref[...])


def _xattn(x, gpre, wq, kv3, wo, gpost, *, tm, seq):
    t, d = x.shape
    per_seq = seq // tm
    return pl.pallas_call(
        _xattn_kernel,
        out_shape=jax.ShapeDtypeStruct((t, d), F32),
        grid=(t // tm,),
        in_specs=[pl.BlockSpec((tm, d), lambda i: (i, 0)),
                  pl.BlockSpec((1, d), lambda i: (0, 0)),
                  pl.BlockSpec((d, XATTN_WIDTH), lambda i: (0, 0)),
                  pl.BlockSpec((1, MEM_LEN, 2 * XATTN_WIDTH), lambda i: (i // per_seq, 0, 0)),
                  pl.BlockSpec((XATTN_WIDTH, d), lambda i: (0, 0)),
                  pl.BlockSpec((1, d), lambda i: (0, 0))],
        out_specs=pl.BlockSpec((tm, d), lambda i: (i, 0)),
        compiler_params=_params(("parallel",)),
        name="xattn",
    )(x, gpre, wq, kv3, wo, gpost)


def _mix_xattn_kernel(ha_ref, hb_ref, ga_ref, gb_ref, wa_ref, wb_ref, wo_ref, g_ref, x_ref,
                      gpre_ref, wq_ref, kv_ref, xwo_ref, gpost_ref, o_ref):
    _mix_out_kernel(ha_ref, hb_ref, ga_ref, gb_ref, wa_ref, wb_ref, wo_ref, g_ref, x_ref, o_ref)
    _xattn_kernel(o_ref, gpre_ref, wq_ref, kv_ref, xwo_ref, gpost_ref, o_ref)


def _mix_xattn(ha, hb, gates, wa, wb, wo, g, x, gpre, wq, kv3, xwo, gpost, *, tm, seq):
    t, kdim = ha.shape
    d = wo.shape[1]
    per_seq = seq // tm
    resident = lambda shape: pl.BlockSpec(shape, lambda i: (0, 0), pipeline_mode=pl.Buffered(1))
    return pl.pallas_call(
        _mix_xattn_kernel,
        out_shape=jax.ShapeDtypeStruct((t, d), F32),
        grid=(t // tm,),
        in_specs=[pl.BlockSpec((tm, kdim), lambda i: (i, 0)),
                  pl.BlockSpec((tm, kdim), lambda i: (i, 0)),
                  pl.BlockSpec((tm, d), lambda i: (i, 0)),
                  pl.BlockSpec((tm, d), lambda i: (i, 1)),
                  resident((kdim, d)), resident((kdim, d)), resident((d, d)),
                  pl.BlockSpec((1, d), lambda i: (0, 0)),
                  pl.BlockSpec((tm, d), lambda i: (i, 0)),
                  pl.BlockSpec((1, d), lambda i: (0, 0)),
                  resident((d, XATTN_WIDTH)),
                  pl.BlockSpec((1, MEM_LEN, 2 * XATTN_WIDTH), lambda i: (i // per_seq, 0, 0)),
                  resident((XATTN_WIDTH, d)),
                  pl.BlockSpec((1, d), lambda i: (0, 0))],
        out_specs=pl.BlockSpec((tm, d), lambda i: (i, 0)),
        compiler_params=pltpu.CompilerParams(dimension_semantics=("parallel",),
                                             vmem_limit_bytes=VMEM_LIMIT_MIX_XATTN),
        name="mix_xattn",
    )(ha, hb, gates, gates, wa, wb, wo, g, x, gpre, wq, kv3, xwo, gpost)


FFN_TAIL = SUBLANES


def _gelu_tanh(x):
    return 0.5 * x * (1.0 + jnp.tanh(0.7978845608028654 * (x + 0.044715 * x * x * x)))


def _ffn_kernel(x_ref, gpre_ref, wg_ref, wu_ref, cwg_ref, cwu_ref, cbg_ref, cbu_ref, wd_ref, gpost_ref,
                o_ref, h_ref, tail_ref, *, tiles_per_seq):
    i = pl.program_id(0)
    j = pl.program_id(1)
    tm, tn = x_ref.shape[0], wg_ref.shape[1]

    @pl.when((i == 0) & (j == 0))
    def _():
        tail_ref[...] = jnp.zeros_like(tail_ref)

    @pl.when(j == 0)
    def _():
        for r0 in range(0, tm, NORM_ROWS):
            rs = slice(r0, r0 + NORM_ROWS)
            h_ref[rs, :] = _bf(_rms(x_ref[rs, :], gpre_ref[...]))
        o_ref[...] = jnp.zeros_like(o_ref)

    prev = jnp.where(i % tiles_per_seq == 0, 0.0, tail_ref[j])
    ug = jnp.dot(h_ref[...], wg_ref[...], preferred_element_type=F32)
    uu = jnp.dot(h_ref[...], wu_ref[...], preferred_element_type=F32)
    tail_ref[j] = jnp.concatenate([ug[tm - FFN_TAIL:, :], uu[tm - FFN_TAIL:, :]], axis=1)

    def conv(u, u_prev, cw_ref, cb_ref):
        u = jnp.concatenate([u_prev, u], axis=0)
        u1 = pltpu.roll(u, 1, 0)
        u2 = pltpu.roll(u, 2, 0)
        cw = cw_ref[...]
        full = cb_ref[...] + cw[2:3, :] * u + cw[1:2, :] * u1 + cw[0:1, :] * u2
        return full[FFN_TAIL:, :]

    gate = conv(ug, prev[:, :tn], cwg_ref, cbg_ref)
    up = conv(uu, prev[:, tn:], cwu_ref, cbu_ref)
    act = _bf(_gelu_tanh(gate) * up)
    o_ref[...] += jnp.dot(act, wd_ref[...], preferred_element_type=F32)

    @pl.when(j == pl.num_programs(1) - 1)
    def _():
        _store_residual_norm(o_ref, x_ref, o_ref, gpost_ref[...])


def _ffn(x, gpre, w_up, conv_w, conv_b, w_down, gpost, *, tm, tn, seq):
    t, d = x.shape
    nj = D_FF // tn
    return pl.pallas_call(
        functools.partial(_ffn_kernel, tiles_per_seq=seq // tm),
        out_shape=jax.ShapeDtypeStruct((t, d), F32),
        grid=(t // tm, nj),
        in_specs=[pl.BlockSpec((tm, d), lambda i, j: (i, 0)),
                  pl.BlockSpec((1, d), lambda i, j: (0, 0)),
                  pl.BlockSpec((d, tn), lambda i, j: (0, j)),
                  pl.BlockSpec((d, tn), lambda i, j: (0, nj + j)),
                  pl.BlockSpec((CONV_WIDTH, tn), lambda i, j: (0, j)),
                  pl.BlockSpec((CONV_WIDTH, tn), lambda i, j: (0, nj + j)),
                  pl.BlockSpec((1, tn), lambda i, j: (0, j)),
                  pl.BlockSpec((1, tn), lambda i, j: (0, nj + j)),
                  pl.BlockSpec((tn, d), lambda i, j: (j, 0)),
                  pl.BlockSpec((1, d), lambda i, j: (0, 0))],
        out_specs=pl.BlockSpec((tm, d), lambda i, j: (i, 0)),
        scratch_shapes=[pltpu.VMEM((tm, d), BF16), pltpu.VMEM((nj, FFN_TAIL, 2 * tn), F32)],
        compiler_params=_params(("arbitrary", "arbitrary")),
        name="conv_glu_ffn",
    )(x, gpre, w_up, w_up, conv_w, conv_w, conv_b, conv_b, w_down, gpost)


def _small_in_proj(w_in):
    d = w_in.shape[1]
    r0, g0 = MLSTM_TOTAL, MLSTM_TOTAL + RWKV_TOTAL
    mlstm_if = w_in[MLSTM_MAIN:r0]
    rwkv_wa = w_in[r0 + RWKV_MAIN:r0 + RWKV_MAIN + WA_W]
    rwkv_g = w_in[r0 + RWKV_MAIN + WA_W:g0]
    z = lambda n: jnp.zeros((n, d), w_in.dtype)
    return jnp.concatenate([rwkv_g, z(RG_PAD - RWKV_GATE_RANK), rwkv_wa, mlstm_if, z(IF_W - 2 * MLSTM_HEADS)], axis=0)


def _layer(x, mem, mix_pre_norm, w_in, mlstm_b_i, mlstm_b_f, mlstm_head_norm, rwkv_mu, rwkv_w0, rwkv_w_up, rwkv_a0,
           rwkv_a_up, rwkv_g_up, rwkv_k_k, rwkv_k_a, rwkv_r_k, rwkv_ln_g, rwkv_ln_b, w_branch_a, w_branch_b,
           w_mix_out, mix_post_norm, xattn_pre_norm, mem_norm, xattn_wq, xattn_wkv, xattn_wo, xattn_post_norm,
           ffn_pre_norm, ffn_w_up, ffn_conv_w, ffn_conv_b, ffn_w_down, ffn_post_norm):
    b, s, d = x.shape
    t = b * s
    x2d = x.reshape(t, d)
    row = lambda p: p.reshape(1, -1).astype(F32)
    tm_row = min(TM_ROW, s)

    w_in_t = w_in.T
    tm_in = min(TM_IN_PROJ, t)
    small, h = _norm_matmul(x2d, row(mix_pre_norm), _small_in_proj(w_in_t), tm=tm_in, tn=SMALL_COLS, out_dtype=F32,
                            name="in_proj_small", w_is_transposed=True)
    small3 = small.reshape(b, s, SMALL_COLS)
    mlstm_main, rwkv_main, gates = _rowranges_matmul(
        h, w_in_t, [(0, MLSTM_MAIN), (MLSTM_TOTAL, RWKV_MAIN), (MLSTM_TOTAL + RWKV_TOTAL, 2 * D_MODEL)],
        tm=tm_in, tn=TN_IN_PROJ, out_dtype=BF16, name="in_proj_wide")

    gate_bias = jnp.concatenate([mlstm_b_i, mlstm_b_f, jnp.zeros((IF_W - 2 * MLSTM_HEADS,), F32)]).reshape(1, IF_W)
    ha, ffn_w_down_b, w_a_b, w_b_b, w_out_b = _mlstm(
        mlstm_main.reshape(b, s, MLSTM_MAIN), small3, gate_bias, row(mlstm_head_norm),
        cast_arrays=(ffn_w_down, w_branch_a, w_branch_b, w_mix_out))

    mu_r, mu_k, mu_v = (rwkv_mu[i * RWKV_WIDTH:(i + 1) * RWKV_WIDTH] for i in range(3))
    mu_wa = rwkv_mu[RWKV_MAIN:RWKV_MAIN + WA_W]
    mu_g = jnp.pad(rwkv_mu[RWKV_MAIN + WA_W:], (0, RG_PAD - RWKV_GATE_RANK))
    pvec = jnp.stack([mu_r, mu_k, mu_v, rwkv_w0, rwkv_a0, rwkv_k_k, rwkv_k_a, rwkv_r_k.reshape(-1), rwkv_ln_g,
                      rwkv_ln_b] + [jnp.zeros((RWKV_WIDTH,), F32)] * (PV_ROWS - 10))
    g_up = jnp.pad(rwkv_g_up, ((0, RG_PAD - RWKV_GATE_RANK), (0, 0)))
    hb, ffn_w_up_b = _rwkv(rwkv_main.reshape(b, s, RWKV_MAIN), small3, pvec, row(mu_g), row(mu_wa), _bf(rwkv_w_up),
                           _bf(rwkv_a_up), _bf(g_up), cast_arrays=(ffn_w_up,))

    mem2d = mem.reshape(b * MEM_LEN, d)
    kv, _ = _norm_matmul(mem2d, row(mem_norm), _bf(xattn_wkv), tm=MEM_LEN, tn=TN, out_dtype=F32, name="mem_kv")
    x2 = _mix_xattn(ha.reshape(t, -1), hb.reshape(t, -1), gates, w_a_b, w_b_b, w_out_b, row(mix_post_norm), x2d,
                    row(xattn_pre_norm), _bf(xattn_wq), kv.reshape(b, MEM_LEN, 2 * XATTN_WIDTH), _bf(xattn_wo),
                    row(xattn_post_norm), tm=min(TM_MIX_XATTN, s), seq=s)

    x3 = _ffn(x2, row(ffn_pre_norm), ffn_w_up_b, ffn_conv_w, row(ffn_conv_b), ffn_w_down_b, row(ffn_post_norm),
              tm=min(TM_FFN, s), tn=TN, seq=s)
    return x3.reshape(b, s, d)


def kernel(x, mem, mix_pre_norm, w_in, mlstm_b_i, mlstm_b_f, mlstm_head_norm, rwkv_mu, rwkv_w0, rwkv_w_up, rwkv_a0, rwkv_a_up, rwkv_g_up, rwkv_k_k, rwkv_k_a, rwkv_r_k, rwkv_ln_g, rwkv_ln_b, w_branch_a, w_branch_b, w_mix_out, mix_post_norm, xattn_pre_norm, mem_norm, xattn_wq, xattn_wkv, xattn_wo, xattn_post_norm, ffn_pre_norm, ffn_w_up, ffn_conv_w, ffn_conv_b, ffn_w_down, ffn_post_norm):
    for l in range(mix_pre_norm.shape[0]):
        x = _layer(x, mem, mix_pre_norm[l], w_in[l], mlstm_b_i[l], mlstm_b_f[l], mlstm_head_norm[l], rwkv_mu[l],
                   rwkv_w0[l], rwkv_w_up[l], rwkv_a0[l], rwkv_a_up[l], rwkv_g_up[l], rwkv_k_k[l], rwkv_k_a[l],
                   rwkv_r_k[l], rwkv_ln_g[l], rwkv_ln_b[l], w_branch_a[l], w_branch_b[l], w_mix_out[l],
                   mix_post_norm[l], xattn_pre_norm[l], mem_norm[l], xattn_wq[l], xattn_wkv[l], xattn_wo[l],
                   xattn_post_norm[l], ffn_pre_norm[l], ffn_w_up[l], ffn_conv_w[l], ffn_conv_b[l], ffn_w_down[l],
                   ffn_post_norm[l])
    return x
```
